```python
import math
import jax, jax.numpy as jnp
from jax import lax
import numpy as np


D_MODEL = 1024
BATCH = 8
SEQ = 4096
DEPTH = 2

N_HEADS = 8
HEAD_DIM = 128
N_KV_HEADS = 2
Q_PER_KV = N_HEADS // N_KV_HEADS
ATTN_WIDTH = N_HEADS * HEAD_DIM
KV_WIDTH = N_KV_HEADS * HEAD_DIM
IDX_HEADS = 8
IDX_DIM = 64
TOPK_MAX = 256
Q_BLOCK = 128
ROPE_THETA = 500000.0
ROPE_FRACTION = 4

SSD_EXPAND = 2
SSD_INNER = SSD_EXPAND * D_MODEL
SSD_HEAD_DIM = 64
SSD_HEADS = SSD_INNER // SSD_HEAD_DIM
SSD_GROUPS = 4
SSD_HEADS_PER_GROUP = SSD_HEADS // SSD_GROUPS
SSD_STATE = 128
SSD_CONV = 4
SSD_CHUNK = 128
SSD_CONV_DIM = SSD_INNER + 2 * SSD_GROUPS * SSD_STATE

FFN_DIM = 2816
FFN_CONV = 3
NORM_EPS = 1e-6

IN_SPLITS = (ATTN_WIDTH, KV_WIDTH, KV_WIDTH, IDX_HEADS * IDX_DIM, IDX_DIM, IDX_HEADS,
             SSD_INNER, SSD_CONV_DIM, SSD_HEADS, D_MODEL, D_MODEL)
IN_COLS = (ATTN_WIDTH + 2 * KV_WIDTH + IDX_HEADS * IDX_DIM + IDX_DIM + IDX_HEADS
           + SSD_INNER + SSD_CONV_DIM + SSD_HEADS + 2 * D_MODEL)

kernel_name = 'hybrid_dsa_ssd_convffn'


def rms_norm(x, w):
    xf = x.astype(jnp.float32)
    y = xf * lax.rsqrt(jnp.mean(xf * xf, axis=-1, keepdims=True) + NORM_EPS)
    return (y * w.astype(jnp.float32)).astype(x.dtype)


def rope_tables(seq, rot_dim):
    inv = ROPE_THETA ** (-jnp.arange(0, rot_dim, 2, dtype=jnp.float32) / rot_dim)
    ang = jnp.arange(seq, dtype=jnp.float32)[:, None] * inv[None, :]
    return jnp.cos(ang), jnp.sin(ang)


def apply_partial_rope(t, cos, sin):
    half = cos.shape[-1]
    rot = 2 * half
    c = cos[None, :, None, :]
    s = sin[None, :, None, :]
    tf = t[..., :rot].astype(jnp.float32)
    x1, x2 = tf[..., :half], tf[..., half:]
    r = jnp.concatenate([x1 * c - x2 * s, x2 * c + x1 * s], axis=-1).astype(t.dtype)
    return jnp.concatenate([r, t[..., rot:]], axis=-1)


def causal_dwconv(u, w, b):
    width = w.shape[0]
    seq = u.shape[1]
    up = jnp.pad(u, ((0, 0), (width - 1, 0), (0, 0)))
    out = b
    for j in range(width):
        out = out + up[:, j:j + seq] * w[j]
    return out


def dsa_attention(q, k, v, q_idx, k_idx, w_idx):
    bsz, seq = q.shape[0], q.shape[1]
    topk = min(TOPK_MAX, seq // 4)
    n_blk = seq // Q_BLOCK
    kv = jnp.concatenate([k, v], axis=-1)
    k_idx_f = k_idx.astype(jnp.float32)
    kpos = jnp.arange(seq)
    scale = HEAD_DIM ** -0.5

    def block(i):
        t0 = i * Q_BLOCK
        sl = lambda a: lax.dynamic_slice_in_dim(a, t0, Q_BLOCK, axis=1)
        qpos = t0 + jnp.arange(Q_BLOCK)
        qi = sl(q_idx).astype(jnp.float32)
        wi = sl(w_idx).astype(jnp.float32)
        rel = jax.nn.relu(jnp.einsum('bqhd,bsd->bqhs', qi, k_idx_f))
        score = jnp.einsum('bqhs,bqh->bqs', rel, wi)
        causal = kpos[None, :] <= qpos[:, None]
        score = jnp.where(causal[None], score, -jnp.inf)
        _, sel = lax.top_k(score, topk)
        valid = sel <= qpos[None, :, None]
        kv_sel = jax.vmap(lambda a, ix: a[ix])(kv, sel)
        k_sel, v_sel = kv_sel[..., :HEAD_DIM], kv_sel[..., HEAD_DIM:]
        qb = sl(q).reshape(bsz, Q_BLOCK, N_KV_HEADS, Q_PER_KV, HEAD_DIM)
        logits = jnp.einsum('bqhgd,bqkhd->bqhgk', qb, k_sel).astype(jnp.float32) * scale
        logits = jnp.where(valid[:, :, None, None, :], logits, -jnp.inf)
        p = jax.nn.softmax(logits, axis=-1).astype(v.dtype)
        o = jnp.einsum('bqhgk,bqkhd->bqhgd', p, v_sel)
        return o.reshape(bsz, Q_BLOCK, ATTN_WIDTH)

    out = lax.map(block, jnp.arange(n_blk))
    return out.transpose(1, 0, 2, 3).reshape(bsz, seq, ATTN_WIDTH)


def ssd_scan(xdt, adt, bm, cm):
    bsz, seq = xdt.shape[0], xdt.shape[1]
    nc = seq // SSD_CHUNK
    X = xdt.reshape(bsz, nc, SSD_CHUNK, SSD_GROUPS, SSD_HEADS_PER_GROUP, SSD_HEAD_DIM)
    Bc = bm.reshape(bsz, nc, SSD_CHUNK, SSD_GROUPS, SSD_STATE)
    Cc = cm.reshape(bsz, nc, SSD_CHUNK, SSD_GROUPS, SSD_STATE)
    A = adt.reshape(bsz, nc, SSD_CHUNK, SSD_GROUPS, SSD_HEADS_PER_GROUP).transpose(0, 3, 4, 1, 2)
    A_cs = jnp.cumsum(A, axis=-1)
    tril = jnp.tril(jnp.ones((SSD_CHUNK, SSD_CHUNK), dtype=bool))
    seg = A_cs[..., :, None] - A_cs[..., None, :]
    Lmat = jnp.exp(jnp.where(tril, seg, -jnp.inf))
    CB = jnp.einsum('bclgn,bcsgn->bcgls', Cc, Bc)
    y_diag = jnp.einsum('bcgls,bgecls,bcsgep->bclgep', CB, Lmat, X)
    decay_states = jnp.exp(A_cs[..., -1:] - A_cs)
    states = jnp.einsum('bclgn,bgecl,bclgep->bcgepn', Bc, decay_states, X)
    chunk_decay = jnp.exp(A_cs[..., -1])

    def step(h, inp):
        s_c, d_c = inp
        return h * d_c[..., None, None] + s_c, h

    h0 = jnp.zeros(states.shape[:1] + states.shape[2:], states.dtype)
    _, states_in = lax.scan(step, h0, (states.transpose(1, 0, 2, 3, 4, 5), chunk_decay.transpose(3, 0, 1, 2)))
    states_in = states_in.transpose(1, 0, 2, 3, 4, 5)
    y_off = jnp.einsum('bclgn,bcgepn,bgecl->bclgep', Cc, states_in, jnp.exp(A_cs))
    return (y_diag + y_off).reshape(bsz, seq, SSD_HEADS, SSD_HEAD_DIM)


def mamba2_branch(z, xbc, dt_raw, conv_w, conv_b, dt_bias, a_log, d_skip, norm_w):
    bsz, seq = z.shape[0], z.shape[1]
    xbc = jax.nn.silu(causal_dwconv(xbc, conv_w, conv_b))
    gn = SSD_GROUPS * SSD_STATE
    xs = xbc[..., :SSD_INNER].reshape(bsz, seq, SSD_HEADS, SSD_HEAD_DIM)
    bm = xbc[..., SSD_INNER:SSD_INNER + gn].reshape(bsz, seq, SSD_GROUPS, SSD_STATE)
    cm = xbc[..., SSD_INNER + gn:].reshape(bsz, seq, SSD_GROUPS, SSD_STATE)
    dt = jax.nn.softplus(dt_raw.astype(jnp.float32) + dt_bias.astype(jnp.float32))
    A = -jnp.exp(a_log.astype(jnp.float32))
    y = ssd_scan(xs * dt[..., None], dt * A, bm, cm) + xs * d_skip[:, None]
    y = y.reshape(bsz, seq, SSD_INNER) * jax.nn.silu(z.astype(jnp.float32))
    yg = y.astype(jnp.float32).reshape(bsz, seq, SSD_GROUPS, SSD_INNER // SSD_GROUPS)
    yg = yg * lax.rsqrt(jnp.mean(yg * yg, axis=-1, keepdims=True) + NORM_EPS)
    y = yg.reshape(bsz, seq, SSD_INNER) * norm_w.astype(jnp.float32)
    return y.astype(z.dtype)


def hybrid_mixer(h, w_in, ssd_conv_w, ssd_conv_b, ssd_dt_bias, ssd_a_log, ssd_d, ssd_norm_w,
                 w_proj_attn, w_proj_ssd, w_out):
    bsz, seq = h.shape[0], h.shape[1]
    proj = h @ w_in
    offsets = np.cumsum(np.array(IN_SPLITS))[:-1].tolist()
    (q, k, v, qi, ki, wi, z, xbc, dt_raw, g_attn, g_ssd) = jnp.split(proj, offsets, axis=-1)
    cos_a, sin_a = rope_tables(seq, HEAD_DIM // ROPE_FRACTION)
    cos_i, sin_i = rope_tables(seq, IDX_DIM // ROPE_FRACTION)
    q = apply_partial_rope(q.reshape(bsz, seq, N_HEADS, HEAD_DIM), cos_a, sin_a)
    k = apply_partial_rope(k.reshape(bsz, seq, N_KV_HEADS, HEAD_DIM), cos_a, sin_a)
    v = v.reshape(bsz, seq, N_KV_HEADS, HEAD_DIM)
    qi = apply_partial_rope(qi.reshape(bsz, seq, IDX_HEADS, IDX_DIM), cos_i, sin_i)
    ki = apply_partial_rope(ki.reshape(bsz, seq, 1, IDX_DIM), cos_i, sin_i)[:, :, 0]
    a = dsa_attention(q, k, v, qi, ki, wi)
    b = mamba2_branch(z, xbc, dt_raw, ssd_conv_w, ssd_conv_b, ssd_dt_bias, ssd_a_log, ssd_d, ssd_norm_w)
    merged = jax.nn.sigmoid(g_attn) * (a @ w_proj_attn) + jax.nn.sigmoid(g_ssd) * (b @ w_proj_ssd)
    return merged @ w_out


def conv_glu_ffn(h, w_up, conv_w, conv_b, w_down):
    u = causal_dwconv(h @ w_up, conv_w, conv_b)
    gate, val = u[..., :FFN_DIM], u[..., FFN_DIM:]
    return (jax.nn.silu(gate) * val) @ w_down


def setup_inputs(seed: int = 0) -> dict:
    key = jax.random.key(seed)
    ks = jax.random.split(key, 20)
    nrm = lambda k, shape, s: jax.random.normal(k, shape, jnp.float32) * s
    x = nrm(ks[0], (BATCH, SEQ, D_MODEL), 1.0)
    norm_mix_w = 1.0 + nrm(ks[1], (DEPTH, D_MODEL), 0.02)
    w_in = nrm(ks[2], (DEPTH, D_MODEL, IN_COLS), D_MODEL ** -0.5)
    ssd_conv_w = nrm(ks[3], (DEPTH, SSD_CONV, SSD_CONV_DIM), SSD_CONV ** -0.5)
    ssd_conv_b = nrm(ks[4], (DEPTH, SSD_CONV_DIM), 0.01)
    u = jax.random.uniform(ks[5], (DEPTH, SSD_HEADS), jnp.float32)
    dt0 = jnp.exp(u * (math.log(0.1) - math.log(0.001)) + math.log(0.001))
    ssd_dt_bias = dt0 + jnp.log(-jnp.expm1(-dt0))
    ssd_a_log = jnp.log(jax.random.uniform(ks[6], (DEPTH, SSD_HEADS), jnp.float32, minval=1.0, maxval=16.0))
    ssd_d = 1.0 + nrm(ks[7], (DEPTH, SSD_HEADS), 0.1)
    ssd_norm_w = 1.0 + nrm(ks[8], (DEPTH, SSD_INNER), 0.02)
    w_proj_attn = nrm(ks[9], (DEPTH, ATTN_WIDTH, D_MODEL), ATTN_WIDTH ** -0.5)
    w_proj_ssd = nrm(ks[10], (DEPTH, SSD_INNER, D_MODEL), SSD_INNER ** -0.5)
    w_out = nrm(ks[11], (DEPTH, D_MODEL, D_MODEL), D_MODEL ** -0.5)
    norm_ffn_w = 1.0 + nrm(ks[12], (DEPTH, D_MODEL), 0.02)
    ffn_w_up = nrm(ks[13], (DEPTH, D_MODEL, 2 * FFN_DIM), D_MODEL ** -0.5)
    ffn_conv_w = nrm(ks[14], (DEPTH, FFN_CONV, 2 * FFN_DIM), FFN_CONV ** -0.5)
    ffn_conv_b = nrm(ks[15], (DEPTH, 2 * FFN_DIM), 0.01)
    ffn_w_down = nrm(ks[16], (DEPTH, FFN_DIM, D_MODEL), FFN_DIM ** -0.5)
    norm_final_w = 1.0 + nrm(ks[17], (D_MODEL,), 0.02)
    return {'x': x, 'norm_mix_w': norm_mix_w, 'w_in': w_in, 'ssd_conv_w': ssd_conv_w,
            'ssd_conv_b': ssd_conv_b, 'ssd_dt_bias': ssd_dt_bias, 'ssd_a_log': ssd_a_log,
            'ssd_d': ssd_d, 'ssd_norm_w': ssd_norm_w, 'w_proj_attn': w_proj_attn,
            'w_proj_ssd': w_proj_ssd, 'w_out': w_out, 'norm_ffn_w': norm_ffn_w,
            'ffn_w_up': ffn_w_up, 'ffn_conv_w': ffn_conv_w, 'ffn_conv_b': ffn_conv_b,
            'ffn_w_down': ffn_w_down, 'norm_final_w': norm_final_w}


def reference(x, norm_mix_w, w_in, ssd_conv_w, ssd_conv_b, ssd_dt_bias, ssd_a_log, ssd_d, ssd_norm_w,
              w_proj_attn, w_proj_ssd, w_out, norm_ffn_w, ffn_w_up, ffn_conv_w, ffn_conv_b,
              ffn_w_down, norm_final_w):
    for l in range(DEPTH):
        h = rms_norm(x, norm_mix_w[l])
        x = x + hybrid_mixer(h, w_in[l], ssd_conv_w[l], ssd_conv_b[l], ssd_dt_bias[l], ssd_a_log[l],
                             ssd_d[l], ssd_norm_w[l], w_proj_attn[l], w_proj_ssd[l], w_out[l])
        h = rms_norm(x, norm_ffn_w[l])
        x = x + conv_glu_ffn(h, ffn_w_up[l], ffn_conv_w[l], ffn_conv_b[l], ffn_w_down[l])
    return rms_norm(x, norm_final_w)
```

```python
import functools

import numpy as np
import jax
import jax.numpy as jnp
from jax import lax
from jax.experimental import pallas as pl
from jax.experimental.pallas import tpu as pltpu

D_MODEL = 1024
N_HEADS = 8
HEAD_DIM = 128
N_KV_HEADS = 2
Q_PER_KV = N_HEADS // N_KV_HEADS
ATTN_WIDTH = N_HEADS * HEAD_DIM
KV_WIDTH = N_KV_HEADS * HEAD_DIM
IDX_HEADS = 8
IDX_DIM = 64
TOPK_MAX = 256
ROPE_THETA = 500000.0
ROPE_FRACTION = 4
SSD_INNER = 2 * D_MODEL
SSD_HEAD_DIM = 64
SSD_HEADS = SSD_INNER // SSD_HEAD_DIM
SSD_GROUPS = 4
SSD_HEADS_PER_GROUP = SSD_HEADS // SSD_GROUPS
SSD_STATE = 128
SSD_CONV = 4
SSD_CHUNK = 128
SSD_GN = SSD_GROUPS * SSD_STATE
SSD_CONV_DIM = SSD_INNER + 2 * SSD_GN
SSD_GROUP_WIDTH = SSD_INNER // SSD_GROUPS
FFN_DIM = 2816
FFN_CONV = 3
NORM_EPS = 1e-6

LANES = 128
SUBLANES = 8
VMEM_LIMIT = 56 * 1024 * 1024
Q_BLOCK = 128
IDX_TILE = 512
ATT_TILE = 256
PROJ_ROWS_A = 512
PROJ_ROWS_B = 256
MERGE_ROWS = 512
FFN_ROWS = 512
FFN_TILE = 256
MASKED = -1e30
INT_MIN = -(2 ** 31)

F32 = jnp.float32
BF16 = jnp.bfloat16
_NT = (((1,), (1,)), ((), ()))
_HI = lax.Precision.HIGHEST


def _params(*sem):
    return pltpu.CompilerParams(dimension_semantics=sem, vmem_limit_bytes=VMEM_LIMIT)


def _resident(shape):
    nd = len(shape)
    return pl.BlockSpec(shape, lambda *_: (0,) * nd, pipeline_mode=pl.Buffered(1))


def _rms(x, w):
    return x * lax.rsqrt(jnp.mean(x * x, axis=-1, keepdims=True) + NORM_EPS) * w


def _sigmoid(x):
    return 1.0 / (1.0 + jnp.exp(-x))


def _dot(a, b):
    return jnp.dot(a, b, preferred_element_type=F32)


def _rope_tables(seq, rot_dim):
    half = rot_dim // 2
    inv = ROPE_THETA ** (-jnp.arange(0, rot_dim, 2, dtype=F32) / rot_dim)
    ang = jnp.arange(seq, dtype=F32)[:, None] * inv[None, :]
    cos, sin = jnp.cos(ang), jnp.sin(ang)
    pad = LANES - rot_dim
    c = jnp.concatenate([cos, cos, jnp.ones((seq, pad), F32)], axis=1)
    s1 = jnp.concatenate([-sin, jnp.zeros((seq, half + pad), F32)], axis=1)
    s2 = jnp.concatenate([jnp.zeros((seq, half), F32), sin, jnp.zeros((seq, pad), F32)], axis=1)
    return c, s1, s2


def _rope(t, c, s1, s2, half):
    return t * c + pltpu.roll(t, LANES - half, 1) * s1 + pltpu.roll(t, half, 1) * s2


_A_Q = 0
_A_K = _A_Q + ATTN_WIDTH
_A_V = _A_K + KV_WIDTH
_A_QI = _A_V + KV_WIDTH
_A_KIW = _A_QI + IDX_HEADS * LANES
_A_COLS = _A_KIW + LANES


def _proj_attn_kernel(x_ref, nw_ref, w_ref, ca_ref, sa1_ref, sa2_ref, ci_ref, si1_ref, si2_ref,
                      q_ref, k_ref, v_ref, qi_ref, kiwf_ref, kiwb_ref):
    h = _rms(x_ref[...], nw_ref[...]).astype(BF16)
    ca, sa1, sa2 = ca_ref[...], sa1_ref[...], sa2_ref[...]
    ci, si1, si2 = ci_ref[...], si1_ref[...], si2_ref[...]
    ha = HEAD_DIM // ROPE_FRACTION // 2
    hi = IDX_DIM // ROPE_FRACTION // 2
    scale = HEAD_DIM ** -0.5
    for j in range(N_HEADS):
        t = _dot(h, w_ref[:, _A_Q + j * LANES:_A_Q + (j + 1) * LANES])
        q_ref[:, j * LANES:(j + 1) * LANES] = (_rope(t, ca, sa1, sa2, ha) * scale).astype(BF16)
    for j in range(N_KV_HEADS):
        t = _dot(h, w_ref[:, _A_K + j * LANES:_A_K + (j + 1) * LANES])
        k_ref[:, j * LANES:(j + 1) * LANES] = _rope(t, ca, sa1, sa2, ha).astype(BF16)
    v_ref[...] = _dot(h, w_ref[:, _A_V:_A_V + KV_WIDTH]).astype(BF16)
    for j in range(IDX_HEADS):
        t = _dot(h, w_ref[:, _A_QI + j * LANES:_A_QI + (j + 1) * LANES])
        qi_ref[:, j * LANES:(j + 1) * LANES] = _rope(t, ci, si1, si2, hi).astype(BF16)
    t = _rope(_dot(h, w_ref[:, _A_KIW:_A_KIW + LANES]), ci, si1, si2, hi)
    kiwf_ref[...] = t
    kiwb_ref[...] = t.astype(BF16)


def _proj_attn(x2, nw, w_a, tabs_a, tabs_i, seq):
    tokens = x2.shape[0]
    tm = PROJ_ROWS_A
    per_seq = seq // tm
    row = lambda w: pl.BlockSpec((tm, w), lambda i: (i, 0))
    tab = pl.BlockSpec((tm, LANES), lambda i: (i % per_seq, 0))
    outs = [(ATTN_WIDTH, BF16), (KV_WIDTH, BF16), (KV_WIDTH, BF16), (IDX_HEADS * LANES, BF16),
            (LANES, F32), (LANES, BF16)]
    return pl.pallas_call(
        _proj_attn_kernel,
        grid=(tokens // tm,),
        in_specs=[row(D_MODEL), _resident((1, D_MODEL)), _resident((D_MODEL, _A_COLS))] + [tab] * 6,
        out_specs=[row(w) for w, _ in outs],
        out_shape=[jax.ShapeDtypeStruct((tokens, w), d) for w, d in outs],
        compiler_params=_params("parallel"),
        name="proj_attn",
    )(x2, nw, w_a, *tabs_a, *tabs_i)


_B_Z = 0
_B_XBC = _B_Z + SSD_INNER
_B_DT = _B_XBC + SSD_CONV_DIM
_B_COLS = _B_DT + LANES


def _proj_ssd_kernel(x_ref, nw_ref, w_ref, z_ref, xbc_ref, dt_ref):
    h = _rms(x_ref[...], nw_ref[...]).astype(BF16)
    z_ref[...] = _dot(h, w_ref[:, _B_Z:_B_Z + SSD_INNER])
    xbc_ref[...] = _dot(h, w_ref[:, _B_XBC:_B_XBC + SSD_CONV_DIM])
    dt_ref[...] = _dot(h, w_ref[:, _B_DT:_B_DT + LANES])


def _proj_ssd(x2, nw, w_b):
    tokens = x2.shape[0]
    tm = PROJ_ROWS_B
    row = lambda w: pl.BlockSpec((tm, w), lambda i: (i, 0))
    widths = (SSD_INNER, SSD_CONV_DIM, LANES)
    return pl.pallas_call(
        _proj_ssd_kernel,
        grid=(tokens // tm,),
        in_specs=[row(D_MODEL), _resident((1, D_MODEL)), _resident((D_MODEL, _B_COLS))],
        out_specs=[row(w) for w in widths],
        out_shape=[jax.ShapeDtypeStruct((tokens, w), F32) for w in widths],
        compiler_params=_params("parallel"),
        name="proj_ssd",
    )(x2, nw, w_b)


def _dsa_kernel(q_ref, qi_ref, wq_ref, k_ref, v_ref, kib_ref, o_ref, key_ref, bias_ref, *, topk, seq):
    i = pl.program_id(1)
    n_idx = (i + IDX_TILE // Q_BLOCK) // (IDX_TILE // Q_BLOCK)
    n_att = (i + ATT_TILE // Q_BLOCK) // (ATT_TILE // Q_BLOCK)
    qpos = lax.broadcasted_iota(jnp.int32, (Q_BLOCK, IDX_TILE), 0) + i * Q_BLOCK
    lane = lax.broadcasted_iota(jnp.int32, (Q_BLOCK, IDX_TILE), 1)
    wq = wq_ref[...]
    kf = float(topk)

    def idx_body(j, carry):
        off = pl.multiple_of(j * IDX_TILE, IDX_TILE)
        kt = kib_ref[pl.ds(off, IDX_TILE), :]
        acc = jnp.zeros((Q_BLOCK, IDX_TILE), F32)
        for h in range(IDX_HEADS):
            s = lax.dot_general(qi_ref[:, h * LANES:(h + 1) * LANES], kt, _NT, preferred_element_type=F32)
            acc = acc + wq[:, IDX_DIM + h:IDX_DIM + h + 1] * jnp.maximum(s, 0.0)
        bits = pltpu.bitcast(acc, jnp.int32)
        key = bits ^ ((bits >> 31) & jnp.int32(0x7FFFFFFF))
        key_ref[:, pl.ds(off, IDX_TILE)] = jnp.where(lane + off <= qpos, key, jnp.int32(INT_MIN))
        return carry

    lax.fori_loop(0, n_idx, idx_body, 0)

    def count(pred):
        def body(j, cnt):
            off = pl.multiple_of(j * IDX_TILE, IDX_TILE)
            m = jnp.where(pred(key_ref[:, pl.ds(off, IDX_TILE)], lane + off), 1.0, 0.0)
            for c in range(IDX_TILE // LANES):
                cnt = cnt + m[:, c * LANES:(c + 1) * LANES]
            return cnt
        cnt = lax.fori_loop(0, n_idx, body, jnp.zeros((Q_BLOCK, LANES), F32))
        return jnp.sum(cnt, axis=1, keepdims=True)

    def bit_body(p, thr):
        cand = thr + lax.shift_left(jnp.int32(1), 31 - p)
        c = count(lambda kt, pos: kt >= cand)
        return jnp.where(c >= kf, cand, thr)

    thr = lax.fori_loop(0, 32, bit_body, jnp.full((Q_BLOCK, 1), INT_MIN, jnp.int32))
    thr = jnp.maximum(thr, jnp.int32(INT_MIN + 1))
    c_ge = count(lambda kt, pos: kt >= thr)
    tied = jnp.max(c_ge) > kf

    def write_bias(pred):
        def body(j, carry):
            off = pl.multiple_of(j * IDX_TILE, IDX_TILE)
            sel = pred(key_ref[:, pl.ds(off, IDX_TILE)], lane + off)
            bias_ref[:, pl.ds(off, IDX_TILE)] = jnp.where(sel, 0.0, MASKED)
            return carry
        lax.fori_loop(0, n_idx, body, 0)

    @pl.when(jnp.logical_not(tied))
    def _():
        write_bias(lambda kt, pos: kt >= thr)

    @pl.when(tied)
    def _():
        need = kf - count(lambda kt, pos: kt > thr)
        nbits = max(1, int(np.ceil(np.log2(seq))))

        def pos_body(p, lim):
            cand = lim + lax.shift_left(jnp.int32(1), nbits - 1 - p)
            c = count(lambda kt, pos: jnp.where(kt == thr, pos, jnp.int32(seq)) < cand)
            return jnp.where(c < need, cand, lim)

        lim = lax.fori_loop(0, nbits, pos_body, jnp.zeros((Q_BLOCK, 1), jnp.int32))
        lim = jnp.where(c_ge > kf, lim, jnp.int32(seq))
        write_bias(lambda kt, pos: (kt > thr) | ((kt == thr) & (pos <= lim)))

    rows = Q_PER_KV * Q_BLOCK
    for g in range(N_KV_HEADS):
        qg = jnp.concatenate(
            [q_ref[:, (Q_PER_KV * g + e) * LANES:(Q_PER_KV * g + e + 1) * LANES] for e in range(Q_PER_KV)], axis=0)

        def att_body(j, carry, g=g, qg=qg):
            m, l, acc = carry
            off = pl.multiple_of(j * ATT_TILE, ATT_TILE)
            kt = k_ref[pl.ds(off, ATT_TILE), g * LANES:(g + 1) * LANES]
            vt = v_ref[pl.ds(off, ATT_TILE), g * LANES:(g + 1) * LANES]
            b = bias_ref[:, pl.ds(off, ATT_TILE)]
            s = lax.dot_general(qg, kt, _NT, preferred_element_type=F32) + jnp.concatenate([b] * Q_PER_KV, axis=0)
            m_new = jnp.maximum(m, jnp.max(s, axis=1, keepdims=True))
            alpha = jnp.exp(m - m_new)
            p = jnp.exp(s - m_new)
            l = alpha * l + jnp.sum(p, axis=1, keepdims=True)
            acc = alpha * acc + _dot(p.astype(BF16), vt)
            return m_new, l, acc

        init = (jnp.full((rows, 1), MASKED, F32), jnp.zeros((rows, 1), F32), jnp.zeros((rows, LANES), F32))
        _, l, acc = lax.fori_loop(0, n_att, att_body, init)
        o = acc / l
        for e in range(Q_PER_KV):
            col = (Q_PER_KV * g + e) * LANES
            o_ref[:, col:col + LANES] = o[e * Q_BLOCK:(e + 1) * Q_BLOCK].astype(BF16)


def _dsa(q, k, v, qi, kiwf, kiwb, bsz, seq):
    assert seq % IDX_TILE == 0
    nq = seq // Q_BLOCK
    topk = min(TOPK_MAX, seq // 4)
    qrow = lambda w: pl.BlockSpec((Q_BLOCK, w), lambda b, i: (b * nq + i, 0))
    full = lambda w: pl.BlockSpec((seq, w), lambda b, i: (b, 0))
    return pl.pallas_call(
        functools.partial(_dsa_kernel, topk=topk, seq=seq),
        grid=(bsz, nq),
        in_specs=[qrow(ATTN_WIDTH), qrow(IDX_HEADS * LANES), qrow(LANES), full(KV_WIDTH), full(KV_WIDTH), full(LANES)],
        out_specs=qrow(ATTN_WIDTH),
        out_shape=jax.ShapeDtypeStruct((bsz * seq, ATTN_WIDTH), BF16),
        scratch_shapes=[pltpu.VMEM((Q_BLOCK, seq), jnp.int32), pltpu.VMEM((Q_BLOCK, seq), F32)],
        compiler_params=_params("parallel", "arbitrary"),
        name="dsa_attention",
    )(q, qi, kiwf, k, v, kiwb)


def _ssd_kernel(z_ref, xbc_ref, dt_ref, cw_ref, cb_ref, dtb_ref, alog_ref, dexp_ref, nw_ref, e_ref,
                o_ref, ext_ref, xa_ref, st_ref):
    L = SSD_CHUNK
    tail = SUBLANES

    @pl.when(pl.program_id(1) == 0)
    def _():
        ext_ref[0:tail, :] = jnp.zeros((tail, SSD_CONV_DIM), F32)
        st_ref[...] = jnp.zeros_like(st_ref)

    ext_ref[tail:tail + L, :] = xbc_ref[...]
    cstep = 512
    for c0 in range(0, SSD_CONV_DIM, cstep):
        cs = slice(c0, c0 + cstep)
        u = cb_ref[:, cs]
        for j in range(SSD_CONV):
            r0 = tail - (SSD_CONV - 1) + j
            u = u + cw_ref[j:j + 1, cs] * ext_ref[r0:r0 + L, cs]
        xa_ref[:, cs] = u * _sigmoid(u)
    ext_ref[0:tail, :] = ext_ref[L:L + tail, :]

    dtv = dt_ref[...] + dtb_ref[...]
    dt = jnp.maximum(dtv, 0.0) + jnp.log1p(jnp.exp(-jnp.abs(dtv)))
    adt = dt * (-jnp.exp(alog_ref[...]))
    ri = lax.broadcasted_iota(jnp.int32, (L, L), 0)
    ci = lax.broadcasted_iota(jnp.int32, (L, L), 1)
    tril = ci <= ri
    acs = jnp.dot(jnp.where(tril, 1.0, 0.0), adt, precision=_HI, preferred_element_type=F32)
    last = acs[L - 1:L, :]
    acs_t = acs.T
    dt_t = dt.T
    stack = jnp.concatenate([dt * jnp.exp(last - acs), jnp.exp(acs)], axis=0)
    first_half = ci < SSD_HEAD_DIM

    gw = SSD_GROUP_WIDTH
    for g in range(SSD_GROUPS):
        gs = slice(g * gw, (g + 1) * gw)
        b_g = xa_ref[:, SSD_INNER + g * SSD_STATE:SSD_INNER + (g + 1) * SSD_STATE]
        c_g = xa_ref[:, SSD_INNER + SSD_GN + g * SSD_STATE:SSD_INNER + SSD_GN + (g + 1) * SSD_STATE]
        b_b, c_b = b_g.astype(BF16), c_g.astype(BF16)
        cb = lax.dot_general(c_b, b_b, _NT, preferred_element_type=F32)
        b_t = b_g.T.astype(BF16)
        ex = jnp.dot(stack, e_ref[:, gs], precision=_HI, preferred_element_type=F32)
        dtdec, eacs = ex[0:L], ex[L:2 * L]
        xs = xa_ref[:, gs]
        s_in = st_ref[:, gs]
        y = _dot(c_b, s_in.astype(BF16)) * eacs + xs * dexp_ref[:, gs]
        st_ref[:, gs] = s_in * eacs[L - 1:L, :] + _dot(b_t, (xs * dtdec).astype(BF16))
        ys = []
        for c2 in range(gw // LANES):
            xp = xs[:, c2 * LANES:(c2 + 1) * LANES].astype(BF16)
            pair = []
            for hh in range(LANES // SSD_HEAD_DIM):
                h = g * SSD_HEADS_PER_GROUP + c2 * (LANES // SSD_HEAD_DIM) + hh
                seg = acs[:, h:h + 1] - acs_t[h:h + 1, :]
                lm = jnp.exp(jnp.where(tril, seg, -jnp.inf))
                pair.append(_dot((cb * lm * dt_t[h:h + 1, :]).astype(BF16), xp))
            ys.append(jnp.where(first_half, pair[0], pair[1]))
        y = y + jnp.concatenate(ys, axis=1)
        zg = z_ref[:, gs]
        yz = y * (zg * _sigmoid(zg))
        ms = jnp.mean(yz * yz, axis=1, keepdims=True)
        o_ref[:, gs] = (yz * lax.rsqrt(ms + NORM_EPS) * nw_ref[:, gs]).astype(BF16)


def _ssd(z, xbc, dtp, cw, cb, dtb, alog, dexp, nw, emat, bsz, seq):
    L = SSD_CHUNK
    nc = seq // L
    row = lambda w: pl.BlockSpec((L, w), lambda b, c: (b * nc + c, 0))
    return pl.pallas_call(
        _ssd_kernel,
        grid=(bsz, nc),
        in_specs=[row(SSD_INNER), row(SSD_CONV_DIM), row(LANES),
                  _resident((SSD_CONV, SSD_CONV_DIM)), _resident((1, SSD_CONV_DIM)),
                  _resident((1, LANES)), _resident((1, LANES)), _resident((1, SSD_INNER)),
                  _resident((1, SSD_INNER)), _resident((LANES, SSD_INNER))],
        out_specs=row(SSD_INNER),
        out_shape=jax.ShapeDtypeStruct((bsz * seq, SSD_INNER), BF16),
        scratch_shapes=[pltpu.VMEM((L + SUBLANES, SSD_CONV_DIM), F32),
                        pltpu.VMEM((L, SSD_CONV_DIM), F32),
                        pltpu.VMEM((SSD_STATE, SSD_INNER), F32)],
        compiler_params=_params("parallel", "arbitrary"),
        name="ssd_branch",
    )(z, xbc, dtp, cw, cb, dtb, alog, dexp, nw, emat)


def _merge_kernel(x_ref, a_ref, b_ref, nw_ref, wg_ref, wpa_ref, wps_ref, wo_ref, o_ref):
    x = x_ref[...]
    h = _rms(x, nw_ref[...]).astype(BF16)
    merged = _sigmoid(_dot(h, wg_ref[:, 0:D_MODEL])) * _dot(a_ref[...], wpa_ref[...])
    merged = merged + _sigmoid(_dot(h, wg_ref[:, D_MODEL:2 * D_MODEL])) * _dot(b_ref[...], wps_ref[...])
    o_ref[...] = x + _dot(merged.astype(BF16), wo_ref[...])


def _merge(x2, a, b, nw, wg, wpa, wps, wo):
    tokens = x2.shape[0]
    tm = MERGE_ROWS
    row = lambda w: pl.BlockSpec((tm, w), lambda i: (i, 0))
    return pl.pallas_call(
        _merge_kernel,
        grid=(tokens // tm,),
        in_specs=[row(D_MODEL), row(ATTN_WIDTH), row(SSD_INNER), _resident((1, D_MODEL)),
                  _resident((D_MODEL, 2 * D_MODEL)), _resident((ATTN_WIDTH, D_MODEL)),
                  _resident((SSD_INNER, D_MODEL)), _resident((D_MODEL, D_MODEL))],
        out_specs=row(D_MODEL),
        out_shape=jax.ShapeDtypeStruct((tokens, D_MODEL), F32),
        compiler_params=_params("parallel"),
        name="merge_out",
    )(x2, a, b, nw, wg, wpa, wps, wo)


def _ffn_kernel(x_ref, nw_ref, wup_ref, cw_ref, cb_ref, wdn_ref, fnw_ref, o_ref, ext_ref, carry_ref, *, final_norm):
    tm = FFN_ROWS
    tail = SUBLANES
    nf = FFN_DIM // FFN_TILE

    @pl.when(pl.program_id(1) == 0)
    def _():
        carry_ref[...] = jnp.zeros_like(carry_ref)

    x = x_ref[...]
    h = _rms(x, nw_ref[...]).astype(BF16)
    acc = x
    for f in range(nf):
        parts = []
        for part in range(2):
            col = part * FFN_DIM + f * FFN_TILE
            cs = slice(col, col + FFN_TILE)
            ct = part * nf + f
            u = _dot(h, wup_ref[:, cs])
            ext_ref[part, 0:tail, :] = carry_ref[ct]
            ext_ref[part, tail:tail + tm, :] = u
            cv = cb_ref[:, cs] + cw_ref[2:3, cs] * u
            for j in range(FFN_CONV - 1):
                r0 = tail - (FFN_CONV - 1) + j
                cv = cv + cw_ref[j:j + 1, cs] * ext_ref[part, r0:r0 + tm, :]
            carry_ref[ct] = u[tm - tail:tm, :]
            parts.append(cv)
        act = (parts[0] * _sigmoid(parts[0]) * parts[1]).astype(BF16)
        acc = acc + _dot(act, wdn_ref[f * FFN_TILE:(f + 1) * FFN_TILE, :])
    if final_norm:
        acc = _rms(acc, fnw_ref[...])
    o_ref[...] = acc


def _ffn(x2, nw, wup, cw, cb, wdn, fnw, bsz, seq, final_norm):
    tm = FFN_ROWS
    ns = seq // tm
    nf = FFN_DIM // FFN_TILE
    row = pl.BlockSpec((tm, D_MODEL), lambda b, s: (b * ns + s, 0))
    return pl.pallas_call(
        functools.partial(_ffn_kernel, final_norm=final_norm),
        grid=(bsz, ns),
        in_specs=[row, _resident((1, D_MODEL)), _resident((D_MODEL, 2 * FFN_DIM)),
                  _resident((FFN_CONV, 2 * FFN_DIM)), _resident((1, 2 * FFN_DIM)),
                  _resident((FFN_DIM, D_MODEL)), _resident((1, D_MODEL))],
        out_specs=row,
        out_shape=jax.ShapeDtypeStruct((bsz * seq, D_MODEL), F32),
        scratch_shapes=[pltpu.VMEM((2, tm + SUBLANES, FFN_TILE), F32),
                        pltpu.VMEM((2 * nf, SUBLANES, FFN_TILE), F32)],
        compiler_params=_params("parallel", "arbitrary"),
        name="conv_glu_ffn",
    )(x2, nw, wup, cw, cb, wdn, fnw)


def _split_w_in(w_in):
    o = np.cumsum([0, ATTN_WIDTH, KV_WIDTH, KV_WIDTH, IDX_HEADS * IDX_DIM, IDX_DIM, IDX_HEADS,
                   SSD_INNER, SSD_CONV_DIM, SSD_HEADS, D_MODEL, D_MODEL]).tolist()
    seg = lambda j: w_in[:, o[j]:o[j + 1]]
    d = w_in.shape[0]
    qi = seg(3).reshape(d, IDX_HEADS, IDX_DIM)
    qi = jnp.pad(qi, ((0, 0), (0, 0), (0, LANES - IDX_DIM))).reshape(d, IDX_HEADS * LANES)
    kiw = jnp.pad(jnp.concatenate([seg(4), seg(5)], axis=1), ((0, 0), (0, LANES - IDX_DIM - IDX_HEADS)))
    w_a = jnp.concatenate([seg(0), seg(1), seg(2), qi, kiw], axis=1).astype(BF16)
    dt = jnp.pad(seg(8), ((0, 0), (0, LANES - SSD_HEADS)))
    w_b = jnp.concatenate([seg(6), seg(7), dt], axis=1).astype(BF16)
    w_g = jnp.concatenate([seg(9), seg(10)], axis=1).astype(BF16)
    return w_a, w_b, w_g


def _lane_pad_row(v):
    return jnp.pad(v, (0, LANES - v.shape[0]))[None, :]


def kernel(x, norm_mix_w, w_in, ssd_conv_w, ssd_conv_b, ssd_dt_bias, ssd_a_log, ssd_d, ssd_norm_w, w_proj_attn, w_proj_ssd, w_out, norm_ffn_w, ffn_w_up, ffn_conv_w, ffn_conv_b, ffn_w_down, norm_final_w):
    bsz, seq, d = x.shape
    depth = w_in.shape[0]
    assert d == D_MODEL and seq % max(PROJ_ROWS_A, FFN_ROWS, MERGE_ROWS, IDX_TILE) == 0
    tabs_a = _rope_tables(seq, HEAD_DIM // ROPE_FRACTION)
    tabs_i = _rope_tables(seq, IDX_DIM // ROPE_FRACTION)
    emat = (jnp.arange(SSD_INNER)[None, :] // SSD_HEAD_DIM == jnp.arange(LANES)[:, None]).astype(F32)
    x2 = x.reshape(bsz * seq, d)
    for l in range(depth):
        w_a, w_b, w_g = _split_w_in(w_in[l])
        nw = norm_mix_w[l][None, :]
        q, k, v, qi, kiwf, kiwb = _proj_attn(x2, nw, w_a, tabs_a, tabs_i, seq)
        a = _dsa(q, k, v, qi, kiwf, kiwb, bsz, seq)
        z, xbc, dtp = _proj_ssd(x2, nw, w_b)
        b = _ssd(z, xbc, dtp, ssd_conv_w[l], ssd_conv_b[l][None, :], _lane_pad_row(ssd_dt_bias[l]),
                 _lane_pad_row(ssd_a_log[l]), jnp.repeat(ssd_d[l], SSD_HEAD_DIM)[None, :],
                 ssd_norm_w[l][None, :], emat, bsz, seq)
        x2 = _merge(x2, a, b, nw, w_g, w_proj_attn[l].astype(BF16), w_proj_ssd[l].astype(BF16),
                    w_out[l].astype(BF16))
        x2 = _ffn(x2, norm_ffn_w[l][None, :], ffn_w_up[l].astype(BF16), ffn_conv_w[l], ffn_conv_b[l][None, :],
                  ffn_w_down[l].astype(BF16), norm_final_w[None, :], bsz, seq, final_norm=(l == depth - 1))
    return x2.reshape(bsz, seq, d)
```

```python
import functools

import numpy as np
import jax
import jax.numpy as jnp
from jax import lax
from jax.experimental import pallas as pl
from jax.experimental.pallas import tpu as pltpu

D_MODEL = 1024
N_HEADS = 8
HEAD_DIM = 128
N_KV_HEADS = 2
Q_PER_KV = N_HEADS // N_KV_HEADS
ATTN_WIDTH = N_HEADS * HEAD_DIM
KV_WIDTH = N_KV_HEADS * HEAD_DIM
IDX_HEADS = 8
IDX_DIM = 64
TOPK_MAX = 256
ROPE_THETA = 500000.0
ROPE_FRACTION = 4
SSD_INNER = 2 * D_MODEL
SSD_HEAD_DIM = 64
SSD_HEADS = SSD_INNER // SSD_HEAD_DIM
SSD_GROUPS = 4
SSD_HEADS_PER_GROUP = SSD_HEADS // SSD_GROUPS
SSD_STATE = 128
SSD_CONV = 4
SSD_CHUNK = 128
SSD_GN = SSD_GROUPS * SSD_STATE
SSD_CONV_DIM = SSD_INNER + 2 * SSD_GN
SSD_GROUP_WIDTH = SSD_INNER // SSD_GROUPS
FFN_DIM = 2816
FFN_CONV = 3
NORM_EPS = 1e-6

LANES = 128
SUBLANES = 8
VMEM_LIMIT = 56 * 1024 * 1024
Q_BLOCK = 128
IDX_TILE = 512
ATT_TILE = 512
PROJ_ROWS_A = 512
PROJ_ROWS_B = 256
MERGE_ROWS = 512
FFN_ROWS = 512
FFN_TILE = 256
MASKED = -1e30
LOG2_E = 1.4426950408889634
INT_MIN = -(2 ** 31)

F32 = jnp.float32
BF16 = jnp.bfloat16
_NT = (((1,), (1,)), ((), ()))
_HI = lax.Precision.HIGHEST


def _params(*sem):
    return pltpu.CompilerParams(dimension_semantics=sem, vmem_limit_bytes=VMEM_LIMIT)


def _resident(shape):
    nd = len(shape)
    return pl.BlockSpec(shape, lambda *_: (0,) * nd, pipeline_mode=pl.Buffered(1))


def _rms(x, w):
    return x * lax.rsqrt(jnp.mean(x * x, axis=-1, keepdims=True) + NORM_EPS) * w


def _sigmoid(x):
    return 1.0 / (1.0 + jnp.exp(-x))


def _dot(a, b):
    return jnp.dot(a, b, preferred_element_type=F32)


def _fold_rows(x, rows):
    parts = [x[r:r + rows] for r in range(0, x.shape[0], rows)]
    while len(parts) > 1:
        parts = [parts[a] + parts[a + 1] for a in range(0, len(parts) - 1, 2)] + parts[len(parts) & ~1:]
    return parts[0]


def _rope_tables(seq, rot_dim):
    half = rot_dim // 2
    inv = ROPE_THETA ** (-jnp.arange(0, rot_dim, 2, dtype=F32) / rot_dim)
    ang = jnp.arange(seq, dtype=F32)[:, None] * inv[None, :]
    cos, sin = jnp.cos(ang), jnp.sin(ang)
    pad = LANES - rot_dim
    c = jnp.concatenate([cos, cos, jnp.ones((seq, pad), F32)], axis=1)
    s1 = jnp.concatenate([-sin, jnp.zeros((seq, half + pad), F32)], axis=1)
    s2 = jnp.concatenate([jnp.zeros((seq, half), F32), sin, jnp.zeros((seq, pad), F32)], axis=1)
    return c, s1, s2


def _rope(t, c, s1, s2, half):
    return t * c + pltpu.roll(t, LANES - half, 1) * s1 + pltpu.roll(t, half, 1) * s2


_A_Q = 0
_A_K = _A_Q + ATTN_WIDTH
_A_QI = _A_K + KV_WIDTH
_A_KIW = _A_QI + IDX_HEADS * LANES
_A_COLS = _A_KIW + LANES


def _proj_attn_kernel(x_ref, nw_ref, w_ref, wvt_ref, ca_ref, sa1_ref, sa2_ref, ci_ref, si1_ref, si2_ref,
                      q_ref, k_ref, vt_ref, qi_ref, kiwf_ref, kiwb_ref):
    h = _rms(x_ref[...], nw_ref[...]).astype(BF16)
    ca, sa1, sa2 = ca_ref[...], sa1_ref[...], sa2_ref[...]
    ci, si1, si2 = ci_ref[...], si1_ref[...], si2_ref[...]
    ha = HEAD_DIM // ROPE_FRACTION // 2
    hi = IDX_DIM // ROPE_FRACTION // 2
    scale = HEAD_DIM ** -0.5 * LOG2_E
    for j in range(N_HEADS):
        t = _dot(h, w_ref[:, _A_Q + j * LANES:_A_Q + (j + 1) * LANES])
        q_ref[:, j * LANES:(j + 1) * LANES] = (_rope(t, ca, sa1, sa2, ha) * scale).astype(BF16)
    for j in range(N_KV_HEADS):
        t = _dot(h, w_ref[:, _A_K + j * LANES:_A_K + (j + 1) * LANES])
        k_ref[:, j * LANES:(j + 1) * LANES] = _rope(t, ca, sa1, sa2, ha).astype(BF16)
    vt_ref[...] = lax.dot_general(wvt_ref[...], h, _NT, preferred_element_type=F32).astype(BF16)
    for j in range(IDX_HEADS):
        t = _dot(h, w_ref[:, _A_QI + j * LANES:_A_QI + (j + 1) * LANES])
        qi_ref[:, j * LANES:(j + 1) * LANES] = _rope(t, ci, si1, si2, hi).astype(BF16)
    t = _rope(_dot(h, w_ref[:, _A_KIW:_A_KIW + LANES]), ci, si1, si2, hi)
    kiwf_ref[...] = t
    kiwb_ref[...] = t.astype(BF16)


def _proj_attn(x2, nw, w_a, w_vt, tabs_a, tabs_i, seq):
    tokens = x2.shape[0]
    tm = PROJ_ROWS_A
    per_seq = seq // tm
    row = lambda w: pl.BlockSpec((tm, w), lambda i: (i, 0))
    tab = pl.BlockSpec((tm, LANES), lambda i: (i % per_seq, 0))
    sds = jax.ShapeDtypeStruct
    return pl.pallas_call(
        _proj_attn_kernel,
        grid=(tokens // tm,),
        in_specs=[row(D_MODEL), _resident((1, D_MODEL)), _resident((D_MODEL, _A_COLS)),
                  _resident((KV_WIDTH, D_MODEL))] + [tab] * 6,
        out_specs=[row(ATTN_WIDTH), row(KV_WIDTH), pl.BlockSpec((KV_WIDTH, tm), lambda i: (0, i)),
                   row(IDX_HEADS * LANES), row(LANES), row(LANES)],
        out_shape=[sds((tokens, ATTN_WIDTH), BF16), sds((tokens, KV_WIDTH), BF16), sds((KV_WIDTH, tokens), BF16),
                   sds((tokens, IDX_HEADS * LANES), BF16), sds((tokens, LANES), F32), sds((tokens, LANES), BF16)],
        compiler_params=_params("parallel"),
        name="proj_attn",
    )(x2, nw, w_a, w_vt, *tabs_a, *tabs_i)


_B_Z = 0
_B_XBC = _B_Z + SSD_INNER
_B_DT = _B_XBC + SSD_CONV_DIM
_B_COLS = _B_DT + LANES


def _proj_ssd_kernel(x_ref, nw_ref, w_ref, z_ref, xbc_ref, dt_ref):
    h = _rms(x_ref[...], nw_ref[...]).astype(BF16)
    z_ref[...] = _dot(h, w_ref[:, _B_Z:_B_Z + SSD_INNER])
    xbc_ref[...] = _dot(h, w_ref[:, _B_XBC:_B_XBC + SSD_CONV_DIM])
    dt_ref[...] = _dot(h, w_ref[:, _B_DT:_B_DT + LANES])


def _proj_ssd(x2, nw, w_b):
    tokens = x2.shape[0]
    tm = PROJ_ROWS_B
    row = lambda w: pl.BlockSpec((tm, w), lambda i: (i, 0))
    widths = (SSD_INNER, SSD_CONV_DIM, LANES)
    return pl.pallas_call(
        _proj_ssd_kernel,
        grid=(tokens // tm,),
        in_specs=[row(D_MODEL), _resident((1, D_MODEL)), _resident((D_MODEL, _B_COLS))],
        out_specs=[row(w) for w in widths],
        out_shape=[jax.ShapeDtypeStruct((tokens, w), F32) for w in widths],
        compiler_params=_params("parallel"),
        name="proj_ssd",
    )(x2, nw, w_b)


def _dsa_kernel(q_ref, qi_ref, wq_ref, k_ref, vt_ref, kib_ref, o_ref, key_ref, hi_ref, lo_ref, bias_ref, *, topk, seq):
    i = pl.program_id(1)
    n_idx = (i + IDX_TILE // Q_BLOCK) // (IDX_TILE // Q_BLOCK)
    n_att = (i + ATT_TILE // Q_BLOCK) // (ATT_TILE // Q_BLOCK)
    kpos = lax.broadcasted_iota(jnp.int32, (IDX_TILE, Q_BLOCK), 0)
    qpos = lax.broadcasted_iota(jnp.int32, (IDX_TILE, Q_BLOCK), 1) + i * Q_BLOCK
    w_t = wq_ref[...].T
    qi_all = jnp.concatenate([qi_ref[:, h * LANES:(h + 1) * LANES] for h in range(IDX_HEADS)], axis=0)
    kf = float(topk)
    i16_min = -(2 ** 15)

    def tile(j):
        return pl.ds(pl.multiple_of(j * IDX_TILE, IDX_TILE), IDX_TILE)

    def idx_body(j, carry):
        s = lax.dot_general(kib_ref[tile(j), :], qi_all, _NT, preferred_element_type=F32)
        acc = jnp.zeros((IDX_TILE, Q_BLOCK), F32)
        for h in range(IDX_HEADS):
            acc = acc + w_t[IDX_DIM + h:IDX_DIM + h + 1, :] * jnp.maximum(s[:, h * Q_BLOCK:(h + 1) * Q_BLOCK], 0.0)
        bits = pltpu.bitcast(acc, jnp.int32)
        key = bits ^ ((bits >> 31) & jnp.int32(0x7FFFFFFF))
        key = jnp.where(kpos + j * IDX_TILE <= qpos, key, jnp.int32(INT_MIN))
        key_ref[tile(j), :] = key
        hi_ref[tile(j), :] = (key >> 16).astype(jnp.int16)
        lo_ref[tile(j), :] = ((key & jnp.int32(0xFFFF)) + jnp.int32(i16_min)).astype(jnp.int16)
        return carry

    lax.fori_loop(0, n_idx, idx_body, 0)

    def count(pred):
        def body(j, cnt):
            m = jnp.where(pred(key_ref[tile(j), :], kpos + j * IDX_TILE), 1.0, 0.0)
            return cnt + _fold_rows(m, SUBLANES)
        cnt = lax.fori_loop(0, n_idx, body, jnp.zeros((SUBLANES, Q_BLOCK), F32))
        return jnp.sum(cnt, axis=0, keepdims=True)

    def count16(ref, pred):
        pk = 2 * SUBLANES
        def body(j, cnt):
            m = jnp.where(pred(ref[tile(j), :]), jnp.int16(1), jnp.int16(0))
            return cnt + _fold_rows(m, pk)
        cnt = lax.fori_loop(0, n_idx, body, jnp.zeros((pk, Q_BLOCK), jnp.int16))
        return jnp.sum(cnt.astype(jnp.int32).astype(F32), axis=0, keepdims=True)

    def bisect16(ref, need):
        def body(p, t):
            cand = t + lax.shift_left(jnp.int32(1), 15 - p)
            c = count16(ref, lambda x: x >= cand.astype(jnp.int16))
            return jnp.where(c >= need, cand, t)
        return lax.fori_loop(0, 16, body, jnp.full((1, Q_BLOCK), i16_min, jnp.int32))

    t_hi = bisect16(hi_ref, kf)
    t_hi16 = t_hi.astype(jnp.int16)
    need_lo = kf - count16(hi_ref, lambda x: x > t_hi16)

    def mask_lo(j, carry):
        lo_ref[tile(j), :] = jnp.where(hi_ref[tile(j), :] == t_hi16, lo_ref[tile(j), :], jnp.int16(i16_min))
        return carry

    lax.fori_loop(0, n_idx, mask_lo, 0)
    t_lo = bisect16(lo_ref, need_lo)
    thr = lax.shift_left(t_hi, 16) + (t_lo - i16_min)
    thr = jnp.maximum(thr, jnp.int32(INT_MIN + 1))
    c_ge = count(lambda kt, pos: kt >= thr)
    tied = jnp.max(c_ge) > kf

    def write_bias(pred):
        def body(j, carry):
            sel = pred(key_ref[tile(j), :], kpos + j * IDX_TILE)
            bias_ref[tile(j), :] = jnp.where(sel, 0.0, MASKED)
            return carry
        lax.fori_loop(0, n_idx, body, 0)

    @pl.when(jnp.logical_not(tied))
    def _():
        write_bias(lambda kt, pos: kt >= thr)

    @pl.when(tied)
    def _():
        need = kf - count(lambda kt, pos: kt > thr)
        nbits = max(1, int(np.ceil(np.log2(seq))))

        def pos_body(p, lim):
            cand = lim + lax.shift_left(jnp.int32(1), nbits - 1 - p)
            c = count(lambda kt, pos: jnp.where(kt == thr, pos, jnp.int32(seq)) < cand)
            return jnp.where(c < need, cand, lim)

        lim = lax.fori_loop(0, nbits, pos_body, jnp.zeros((1, Q_BLOCK), jnp.int32))
        lim = jnp.where(c_ge > kf, lim, jnp.int32(seq))
        write_bias(lambda kt, pos: (kt > thr) | ((kt == thr) & (pos <= lim)))

    rows = Q_PER_KV * Q_BLOCK
    for g in range(N_KV_HEADS):
        qg = jnp.concatenate(
            [q_ref[:, (Q_PER_KV * g + e) * LANES:(Q_PER_KV * g + e + 1) * LANES] for e in range(Q_PER_KV)], axis=0)

        def att_body(j, carry, g=g, qg=qg):
            m, l8, acc = carry
            ks = pl.ds(pl.multiple_of(j * ATT_TILE, ATT_TILE), ATT_TILE)
            b = bias_ref[ks, :]
            s = lax.dot_general(k_ref[ks, g * LANES:(g + 1) * LANES], qg, _NT, preferred_element_type=F32)
            s = s + jnp.concatenate([b] * Q_PER_KV, axis=1)
            m_new = jnp.maximum(m, jnp.max(s, axis=0, keepdims=True))
            alpha = jnp.exp2(m - m_new)
            p = jnp.exp2(s - m_new)
            l8 = alpha * l8 + _fold_rows(p, SUBLANES)
            acc = alpha * acc + _dot(vt_ref[g * LANES:(g + 1) * LANES, ks], p.astype(BF16))
            return m_new, l8, acc

        init = (jnp.full((1, rows), MASKED, F32), jnp.zeros((SUBLANES, rows), F32), jnp.zeros((LANES, rows), F32))
        _, l8, acc = lax.fori_loop(0, n_att, att_body, init)
        o_t = acc / jnp.sum(l8, axis=0, keepdims=True)
        for e in range(Q_PER_KV):
            col = (Q_PER_KV * g + e) * LANES
            o_ref[:, col:col + LANES] = o_t[:, e * Q_BLOCK:(e + 1) * Q_BLOCK].T.astype(BF16)


def _dsa(q, k, vt, qi, kiwf, kiwb, bsz, seq):
    assert seq % IDX_TILE == 0 and seq % ATT_TILE == 0 and seq // (2 * SUBLANES) < 2 ** 15
    nq = seq // Q_BLOCK
    topk = min(TOPK_MAX, seq // 4)
    qrow = lambda w: pl.BlockSpec((Q_BLOCK, w), lambda b, i: (b * nq + i, 0))
    full = lambda w: pl.BlockSpec((seq, w), lambda b, i: (b, 0))
    return pl.pallas_call(
        functools.partial(_dsa_kernel, topk=topk, seq=seq),
        grid=(bsz, nq),
        in_specs=[qrow(ATTN_WIDTH), qrow(IDX_HEADS * LANES), qrow(LANES), full(KV_WIDTH),
                  pl.BlockSpec((KV_WIDTH, seq), lambda b, i: (0, b)), full(LANES)],
        out_specs=qrow(ATTN_WIDTH),
        out_shape=jax.ShapeDtypeStruct((bsz * seq, ATTN_WIDTH), BF16),
        scratch_shapes=[pltpu.VMEM((seq, Q_BLOCK), jnp.int32), pltpu.VMEM((seq, Q_BLOCK), jnp.int16),
                        pltpu.VMEM((seq, Q_BLOCK), jnp.int16), pltpu.VMEM((seq, Q_BLOCK), F32)],
        compiler_params=_params("parallel", "arbitrary"),
        name="dsa_attention",
    )(q, qi, kiwf, k, vt, kiwb)


def _ssd_kernel(z_ref, xbc_ref, dt_ref, cw_ref, cb_ref, dtb_ref, alog_ref, dexp_ref, nw_ref, e_ref,
                o_ref, ext_ref, xa_ref, st_ref):
    L = SSD_CHUNK
    tail = SUBLANES

    @pl.when(pl.program_id(1) == 0)
    def _():
        ext_ref[0:tail, :] = jnp.zeros((tail, SSD_CONV_DIM), F32)
        st_ref[...] = jnp.zeros_like(st_ref)

    ext_ref[tail:tail + L, :] = xbc_ref[...]
    cstep = 512
    for c0 in range(0, SSD_CONV_DIM, cstep):
        cs = slice(c0, c0 + cstep)
        u = cb_ref[:, cs]
        for j in range(SSD_CONV):
            r0 = tail - (SSD_CONV - 1) + j
            u = u + cw_ref[j:j + 1, cs] * ext_ref[r0:r0 + L, cs]
        xa_ref[:, cs] = u * _sigmoid(u)
    ext_ref[0:tail, :] = ext_ref[L:L + tail, :]

    dtv = dt_ref[...] + dtb_ref[...]
    dt = jnp.maximum(dtv, 0.0) + jnp.log1p(jnp.exp(-jnp.abs(dtv)))
    adt = dt * (-jnp.exp(alog_ref[...]))
    ri = lax.broadcasted_iota(jnp.int32, (L, L), 0)
    ci = lax.broadcasted_iota(jnp.int32, (L, L), 1)
    tril = ci <= ri
    acs = jnp.dot(jnp.where(tril, 1.0, 0.0), adt, precision=_HI, preferred_element_type=F32)
    last = acs[L - 1:L, :]
    acs_t = acs.T
    dt_t = dt.T
    stack = jnp.concatenate([dt * jnp.exp(last - acs), jnp.exp(acs)], axis=0)
    first_half = ci < SSD_HEAD_DIM

    gw = SSD_GROUP_WIDTH
    for g in range(SSD_GROUPS):
        gs = slice(g * gw, (g + 1) * gw)
        b_g = xa_ref[:, SSD_INNER + g * SSD_STATE:SSD_INNER + (g + 1) * SSD_STATE]
        c_g = xa_ref[:, SSD_INNER + SSD_GN + g * SSD_STATE:SSD_INNER + SSD_GN + (g + 1) * SSD_STATE]
        b_b, c_b = b_g.astype(BF16), c_g.astype(BF16)
        cb = lax.dot_general(c_b, b_b, _NT, preferred_element_type=F32)
        b_t = b_g.T.astype(BF16)
        ex = jnp.dot(stack, e_ref[:, gs], precision=_HI, preferred_element_type=F32)
        dtdec, eacs = ex[0:L], ex[L:2 * L]
        xs = xa_ref[:, gs]
        s_in = st_ref[:, gs]
        y = _dot(c_b, s_in.astype(BF16)) * eacs + xs * dexp_ref[:, gs]
        st_ref[:, gs] = s_in * eacs[L - 1:L, :] + _dot(b_t, (xs * dtdec).astype(BF16))
        ys = []
        for c2 in range(gw // LANES):
            xp = xs[:, c2 * LANES:(c2 + 1) * LANES].astype(BF16)
            pair = []
            for hh in range(LANES // SSD_HEAD_DIM):
                h = g * SSD_HEADS_PER_GROUP + c2 * (LANES // SSD_HEAD_DIM) + hh
                seg = acs[:, h:h + 1] - acs_t[h:h + 1, :]
                lm = jnp.exp(jnp.where(tril, seg, -jnp.inf))
                pair.append(_dot((cb * lm * dt_t[h:h + 1, :]).astype(BF16), xp))
            ys.append(jnp.where(first_half, pair[0], pair[1]))
        y = y + jnp.concatenate(ys, axis=1)
        zg = z_ref[:, gs]
        yz = y * (zg * _sigmoid(zg))
        ms = jnp.mean(yz * yz, axis=1, keepdims=True)
        o_ref[:, gs] = (yz * lax.rsqrt(ms + NORM_EPS) * nw_ref[:, gs]).astype(BF16)


def _ssd(z, xbc, dtp, cw, cb, dtb, alog, dexp, nw, emat, bsz, seq):
    L = SSD_CHUNK
    nc = seq // L
    row = lambda w: pl.BlockSpec((L, w), lambda b, c: (b * nc + c, 0))
    return pl.pallas_call(
        _ssd_kernel,
        grid=(bsz, nc),
        in_specs=[row(SSD_INNER), row(SSD_CONV_DIM), row(LANES),
                  _resident((SSD_CONV, SSD_CONV_DIM)), _resident((1, SSD_CONV_DIM)),
                  _resident((1, LANES)), _resident((1, LANES)), _resident((1, SSD_INNER)),
                  _resident((1, SSD_INNER)), _resident((LANES, SSD_INNER))],
        out_specs=row(SSD_INNER),
        out_shape=jax.ShapeDtypeStruct((bsz * seq, SSD_INNER), BF16),
        scratch_shapes=[pltpu.VMEM((L + SUBLANES, SSD_CONV_DIM), F32),
                        pltpu.VMEM((L, SSD_CONV_DIM), F32),
                        pltpu.VMEM((SSD_STATE, SSD_INNER), F32)],
        compiler_params=_params("parallel", "arbitrary"),
        name="ssd_branch",
    )(z, xbc, dtp, cw, cb, dtb, alog, dexp, nw, emat)


def _merge_kernel(x_ref, a_ref, b_ref, nw_ref, wg_ref, wpa_ref, wps_ref, wo_ref, o_ref):
    x = x_ref[...]
    h = _rms(x, nw_ref[...]).astype(BF16)
    merged = _sigmoid(_dot(h, wg_ref[:, 0:D_MODEL])) * _dot(a_ref[...], wpa_ref[...])
    merged = merged + _sigmoid(_dot(h, wg_ref[:, D_MODEL:2 * D_MODEL])) * _dot(b_ref[...], wps_ref[...])
    o_ref[...] = x + _dot(merged.astype(BF16), wo_ref[...])


def _merge(x2, a, b, nw, wg, wpa, wps, wo):
    tokens = x2.shape[0]
    tm = MERGE_ROWS
    row = lambda w: pl.BlockSpec((tm, w), lambda i: (i, 0))
    return pl.pallas_call(
        _merge_kernel,
        grid=(tokens // tm,),
        in_specs=[row(D_MODEL), row(ATTN_WIDTH), row(SSD_INNER), _resident((1, D_MODEL)),
                  _resident((D_MODEL, 2 * D_MODEL)), _resident((ATTN_WIDTH, D_MODEL)),
                  _resident((SSD_INNER, D_MODEL)), _resident((D_MODEL, D_MODEL))],
        out_specs=row(D_MODEL),
        out_shape=jax.ShapeDtypeStruct((tokens, D_MODEL), F32),
        compiler_params=_params("parallel"),
        name="merge_out",
    )(x2, a, b, nw, wg, wpa, wps, wo)


def _ffn_kernel(x_ref, nw_ref, wup_ref, cw_ref, cb_ref, wdn_ref, fnw_ref, o_ref, ext_ref, carry_ref, *, final_norm):
    tm = FFN_ROWS
    tail = SUBLANES
    nf = FFN_DIM // FFN_TILE

    @pl.when(pl.program_id(1) == 0)
    def _():
        carry_ref[...] = jnp.zeros_like(carry_ref)

    x = x_ref[...]
    h = _rms(x, nw_ref[...]).astype(BF16)
    acc = x
    for f in range(nf):
        parts = []
        for part in range(2):
            col = part * FFN_DIM + f * FFN_TILE
            cs = slice(col, col + FFN_TILE)
            ct = part * nf + f
            u = _dot(h, wup_ref[:, cs])
            ext_ref[part, 0:tail, :] = carry_ref[ct]
            ext_ref[part, tail:tail + tm, :] = u
            cv = cb_ref[:, cs] + cw_ref[2:3, cs] * u
            for j in range(FFN_CONV - 1):
                r0 = tail - (FFN_CONV - 1) + j
                cv = cv + cw_ref[j:j + 1, cs] * ext_ref[part, r0:r0 + tm, :]
            carry_ref[ct] = u[tm - tail:tm, :]
            parts.append(cv)
        act = (parts[0] * _sigmoid(parts[0]) * parts[1]).astype(BF16)
        acc = acc + _dot(act, wdn_ref[f * FFN_TILE:(f + 1) * FFN_TILE, :])
    if final_norm:
        acc = _rms(acc, fnw_ref[...])
    o_ref[...] = acc


def _ffn(x2, nw, wup, cw, cb, wdn, fnw, bsz, seq, final_norm):
    tm = FFN_ROWS
    ns = seq // tm
    nf = FFN_DIM // FFN_TILE
    row = pl.BlockSpec((tm, D_MODEL), lambda b, s: (b * ns + s, 0))
    return pl.pallas_call(
        functools.partial(_ffn_kernel, final_norm=final_norm),
        grid=(bsz, ns),
        in_specs=[row, _resident((1, D_MODEL)), _resident((D_MODEL, 2 * FFN_DIM)),
                  _resident((FFN_CONV, 2 * FFN_DIM)), _resident((1, 2 * FFN_DIM)),
                  _resident((FFN_DIM, D_MODEL)), _resident((1, D_MODEL))],
        out_specs=row,
        out_shape=jax.ShapeDtypeStruct((bsz * seq, D_MODEL), F32),
        scratch_shapes=[pltpu.VMEM((2, tm + SUBLANES, FFN_TILE), F32),
                        pltpu.VMEM((2 * nf, SUBLANES, FFN_TILE), F32)],
        compiler_params=_params("parallel", "arbitrary"),
        name="conv_glu_ffn",
    )(x2, nw, wup, cw, cb, wdn, fnw)


def _split_w_in(w_in):
    o = np.cumsum([0, ATTN_WIDTH, KV_WIDTH, KV_WIDTH, IDX_HEADS * IDX_DIM, IDX_DIM, IDX_HEADS,
                   SSD_INNER, SSD_CONV_DIM, SSD_HEADS, D_MODEL, D_MODEL]).tolist()
    seg = lambda j: w_in[:, o[j]:o[j + 1]]
    d = w_in.shape[0]
    qi = seg(3).reshape(d, IDX_HEADS, IDX_DIM)
    qi = jnp.pad(qi, ((0, 0), (0, 0), (0, LANES - IDX_DIM))).reshape(d, IDX_HEADS * LANES)
    kiw = jnp.pad(jnp.concatenate([seg(4), seg(5)], axis=1), ((0, 0), (0, LANES - IDX_DIM - IDX_HEADS)))
    w_a = jnp.concatenate([seg(0), seg(1), qi, kiw], axis=1).astype(BF16)
    w_vt = seg(2).T.astype(BF16)
    dt = jnp.pad(seg(8), ((0, 0), (0, LANES - SSD_HEADS)))
    w_b = jnp.concatenate([seg(6), seg(7), dt], axis=1).astype(BF16)
    w_g = jnp.concatenate([seg(9), seg(10)], axis=1).astype(BF16)
    return w_a, w_vt, w_b, w_g


def _lane_pad_row(v):
    return jnp.pad(v, (0, LANES - v.shape[0]))[None, :]


def kernel(x, norm_mix_w, w_in, ssd_conv_w, ssd_conv_b, ssd_dt_bias, ssd_a_log, ssd_d, ssd_norm_w, w_proj_attn, w_proj_ssd, w_out, norm_ffn_w, ffn_w_up, ffn_conv_w, ffn_conv_b, ffn_w_down, norm_final_w):
    bsz, seq, d = x.shape
    depth = w_in.shape[0]
    assert d == D_MODEL and seq % max(PROJ_ROWS_A, FFN_ROWS, MERGE_ROWS, IDX_TILE) == 0
    tabs_a = _rope_tables(seq, HEAD_DIM // ROPE_FRACTION)
    tabs_i = _rope_tables(seq, IDX_DIM // ROPE_FRACTION)
    emat = (jnp.arange(SSD_INNER)[None, :] // SSD_HEAD_DIM == jnp.arange(LANES)[:, None]).astype(F32)
    x2 = x.reshape(bsz * seq, d)
    for l in range(depth):
        w_a, w_vt, w_b, w_g = _split_w_in(w_in[l])
        nw = norm_mix_w[l][None, :]
        q, k, vt, qi, kiwf, kiwb = _proj_attn(x2, nw, w_a, w_vt, tabs_a, tabs_i, seq)
        a = _dsa(q, k, vt, qi, kiwf, kiwb, bsz, seq)
        z, xbc, dtp = _proj_ssd(x2, nw, w_b)
        b = _ssd(z, xbc, dtp, ssd_conv_w[l], ssd_conv_b[l][None, :], _lane_pad_row(ssd_dt_bias[l]),
                 _lane_pad_row(ssd_a_log[l]), jnp.repeat(ssd_d[l], SSD_HEAD_DIM)[None, :],
                 ssd_norm_w[l][None, :], emat, bsz, seq)
        x2 = _merge(x2, a, b, nw, w_g, w_proj_attn[l].astype(BF16), w_proj_ssd[l].astype(BF16),
                    w_out[l].astype(BF16))
        x2 = _ffn(x2, norm_ffn_w[l][None, :], ffn_w_up[l].astype(BF16), ffn_conv_w[l], ffn_conv_b[l][None, :],
                  ffn_w_down[l].astype(BF16), norm_final_w[None, :], bsz, seq, final_norm=(l == depth - 1))
    return x2.reshape(bsz, seq, d)
```

```python
import functools

import numpy as np
import jax
import jax.numpy as jnp
from jax import lax
from jax.experimental import pallas as pl
from jax.experimental.pallas import tpu as pltpu

D_MODEL = 1024
N_HEADS = 8
HEAD_DIM = 128
N_KV_HEADS = 2
Q_PER_KV = N_HEADS // N_KV_HEADS
ATTN_WIDTH = N_HEADS * HEAD_DIM
KV_WIDTH = N_KV_HEADS * HEAD_DIM
IDX_HEADS = 8
IDX_DIM = 64
TOPK_MAX = 256
ROPE_THETA = 500000.0
ROPE_FRACTION = 4
SSD_INNER = 2 * D_MODEL
SSD_HEAD_DIM = 64
SSD_HEADS = SSD_INNER // SSD_HEAD_DIM
SSD_GROUPS = 4
SSD_HEADS_PER_GROUP = SSD_HEADS // SSD_GROUPS
SSD_STATE = 128
SSD_CONV = 4
SSD_CHUNK = 128
SSD_GN = SSD_GROUPS * SSD_STATE
SSD_CONV_DIM = SSD_INNER + 2 * SSD_GN
SSD_GROUP_WIDTH = SSD_INNER // SSD_GROUPS
FFN_DIM = 2816
FFN_CONV = 3
NORM_EPS = 1e-6

LANES = 128
SUBLANES = 8
VMEM_LIMIT = 56 * 1024 * 1024
Q_BLOCK = 128
IDX_TILE = 512
ATT_TILE = 512
PROJ_ROWS_A = 512
PROJ_ROWS_B = 256
MERGE_ROWS = 512
FFN_ROWS = 512
FFN_TILE = 256
MASKED = -1e30
LOG2_E = 1.4426950408889634
INT_MIN = -(2 ** 31)

F32 = jnp.float32
BF16 = jnp.bfloat16
_NT = (((1,), (1,)), ((), ()))
_HI = lax.Precision.HIGHEST


def _params(*sem):
    return pltpu.CompilerParams(dimension_semantics=sem, vmem_limit_bytes=VMEM_LIMIT)


def _resident(shape):
    nd = len(shape)
    return pl.BlockSpec(shape, lambda *_: (0,) * nd, pipeline_mode=pl.Buffered(1))


def _rms(x, w):
    return x * lax.rsqrt(jnp.mean(x * x, axis=-1, keepdims=True) + NORM_EPS) * w


def _sigmoid(x):
    return 1.0 / (1.0 + jnp.exp(-x))


def _dot(a, b):
    return jnp.dot(a, b, preferred_element_type=F32)


def _fold_rows(x, rows, op=jnp.add):
    parts = [x[r:r + rows] for r in range(0, x.shape[0], rows)]
    while len(parts) > 1:
        parts = [op(parts[a], parts[a + 1]) for a in range(0, len(parts) - 1, 2)] + parts[len(parts) & ~1:]
    return parts[0]


def _rope_tables(seq, rot_dim):
    half = rot_dim // 2
    inv = ROPE_THETA ** (-jnp.arange(0, rot_dim, 2, dtype=F32) / rot_dim)
    ang = jnp.arange(seq, dtype=F32)[:, None] * inv[None, :]
    cos, sin = jnp.cos(ang), jnp.sin(ang)
    pad = LANES - rot_dim
    c = jnp.concatenate([cos, cos, jnp.ones((seq, pad), F32)], axis=1)
    s1 = jnp.concatenate([-sin, jnp.zeros((seq, half + pad), F32)], axis=1)
    s2 = jnp.concatenate([jnp.zeros((seq, half), F32), sin, jnp.zeros((seq, pad), F32)], axis=1)
    return c, s1, s2


def _rope(t, c, s1, s2, half):
    return t * c + pltpu.roll(t, LANES - half, 1) * s1 + pltpu.roll(t, half, 1) * s2


_A_Q = 0
_A_K = _A_Q + ATTN_WIDTH
_A_QI = _A_K + KV_WIDTH
_A_KIW = _A_QI + IDX_HEADS * LANES
_A_COLS = _A_KIW + LANES


def _proj_attn_kernel(x_ref, nw_ref, w_ref, wvt_ref, ca_ref, sa1_ref, sa2_ref, ci_ref, si1_ref, si2_ref,
                      q_ref, k_ref, vt_ref, qi_ref, kiwf_ref, kiwb_ref):
    h = _rms(x_ref[...], nw_ref[...]).astype(BF16)
    ca, sa1, sa2 = ca_ref[...], sa1_ref[...], sa2_ref[...]
    ci, si1, si2 = ci_ref[...], si1_ref[...], si2_ref[...]
    ha = HEAD_DIM // ROPE_FRACTION // 2
    hi = IDX_DIM // ROPE_FRACTION // 2
    scale = HEAD_DIM ** -0.5 * LOG2_E
    for j in range(N_HEADS):
        t = _dot(h, w_ref[:, _A_Q + j * LANES:_A_Q + (j + 1) * LANES])
        q_ref[:, j * LANES:(j + 1) * LANES] = (_rope(t, ca, sa1, sa2, ha) * scale).astype(BF16)
    for j in range(N_KV_HEADS):
        t = _dot(h, w_ref[:, _A_K + j * LANES:_A_K + (j + 1) * LANES])
        k_ref[:, j * LANES:(j + 1) * LANES] = _rope(t, ca, sa1, sa2, ha).astype(BF16)
    vt_ref[...] = lax.dot_general(wvt_ref[...], h, _NT, preferred_element_type=F32).astype(BF16)
    for j in range(IDX_HEADS):
        t = _dot(h, w_ref[:, _A_QI + j * LANES:_A_QI + (j + 1) * LANES])
        qi_ref[:, j * LANES:(j + 1) * LANES] = _rope(t, ci, si1, si2, hi).astype(BF16)
    t = _rope(_dot(h, w_ref[:, _A_KIW:_A_KIW + LANES]), ci, si1, si2, hi)
    kiwf_ref[...] = t
    kiwb_ref[...] = t.astype(BF16)


def _proj_attn(x2, nw, w_a, w_vt, tabs_a, tabs_i, seq):
    tokens = x2.shape[0]
    tm = PROJ_ROWS_A
    per_seq = seq // tm
    row = lambda w: pl.BlockSpec((tm, w), lambda i: (i, 0))
    tab = pl.BlockSpec((tm, LANES), lambda i: (i % per_seq, 0))
    sds = jax.ShapeDtypeStruct
    return pl.pallas_call(
        _proj_attn_kernel,
        grid=(tokens // tm,),
        in_specs=[row(D_MODEL), _resident((1, D_MODEL)), _resident((D_MODEL, _A_COLS)),
                  _resident((KV_WIDTH, D_MODEL))] + [tab] * 6,
        out_specs=[row(ATTN_WIDTH), row(KV_WIDTH), pl.BlockSpec((KV_WIDTH, tm), lambda i: (0, i)),
                   row(IDX_HEADS * LANES), row(LANES), row(LANES)],
        out_shape=[sds((tokens, ATTN_WIDTH), BF16), sds((tokens, KV_WIDTH), BF16), sds((KV_WIDTH, tokens), BF16),
                   sds((tokens, IDX_HEADS * LANES), BF16), sds((tokens, LANES), F32), sds((tokens, LANES), BF16)],
        compiler_params=_params("parallel"),
        name="proj_attn",
    )(x2, nw, w_a, w_vt, *tabs_a, *tabs_i)


_B_Z = 0
_B_XBC = _B_Z + SSD_INNER
_B_DT = _B_XBC + SSD_CONV_DIM
_B_COLS = _B_DT + LANES


def _proj_ssd_kernel(x_ref, nw_ref, w_ref, z_ref, xbc_ref, dt_ref):
    h = _rms(x_ref[...], nw_ref[...]).astype(BF16)
    z_ref[...] = _dot(h, w_ref[:, _B_Z:_B_Z + SSD_INNER])
    xbc_ref[...] = _dot(h, w_ref[:, _B_XBC:_B_XBC + SSD_CONV_DIM])
    dt_ref[...] = _dot(h, w_ref[:, _B_DT:_B_DT + LANES])


def _proj_ssd(x2, nw, w_b):
    tokens = x2.shape[0]
    tm = PROJ_ROWS_B
    row = lambda w: pl.BlockSpec((tm, w), lambda i: (i, 0))
    widths = (SSD_INNER, SSD_CONV_DIM, LANES)
    return pl.pallas_call(
        _proj_ssd_kernel,
        grid=(tokens // tm,),
        in_specs=[row(D_MODEL), _resident((1, D_MODEL)), _resident((D_MODEL, _B_COLS))],
        out_specs=[row(w) for w in widths],
        out_shape=[jax.ShapeDtypeStruct((tokens, w), F32) for w in widths],
        compiler_params=_params("parallel"),
        name="proj_ssd",
    )(x2, nw, w_b)


def _sortable(x):
    return x ^ ((x >> 31) & jnp.int32(0x7FFFFFFF))


def _reduce_rows(x, op):
    for sh in (4, 2, 1):
        x = op(x, pltpu.roll(x, sh, 0))
    return x[0:1, :]


def _dsa_kernel(q_ref, qi_ref, wq_ref, k_ref, vt_ref, kib_ref, o_ref, key_ref, sel_ref, *, topk, seq):
    i = pl.program_id(1)
    n_idx = (i + IDX_TILE // Q_BLOCK) // (IDX_TILE // Q_BLOCK)
    n_att = (i + ATT_TILE // Q_BLOCK) // (ATT_TILE // Q_BLOCK)
    kpos = lax.broadcasted_iota(jnp.int32, (IDX_TILE, Q_BLOCK), 0)
    qpos = lax.broadcasted_iota(jnp.int32, (IDX_TILE, Q_BLOCK), 1) + i * Q_BLOCK
    w_t = wq_ref[...].T
    qi_all = jnp.concatenate([qi_ref[:, h * LANES:(h + 1) * LANES] for h in range(IDX_HEADS)], axis=0)
    kf = float(topk)
    int_max = jnp.int32(2 ** 31 - 1)

    def tile(j):
        return pl.ds(pl.multiple_of(j * IDX_TILE, IDX_TILE), IDX_TILE)

    def idx_body(j, carry):
        kmax, kmin = carry
        s = lax.dot_general(kib_ref[tile(j), :], qi_all, _NT, preferred_element_type=F32)
        acc = jnp.zeros((IDX_TILE, Q_BLOCK), F32)
        for h in range(IDX_HEADS):
            acc = acc + w_t[IDX_DIM + h:IDX_DIM + h + 1, :] * jnp.maximum(s[:, h * Q_BLOCK:(h + 1) * Q_BLOCK], 0.0)
        key = _sortable(pltpu.bitcast(acc, jnp.int32))
        causal = kpos + j * IDX_TILE <= qpos
        low = jnp.where(causal, key, jnp.int32(INT_MIN))
        key_ref[tile(j), :] = low
        kmax = jnp.maximum(kmax, _fold_rows(low, SUBLANES, jnp.maximum))
        kmin = jnp.minimum(kmin, _fold_rows(jnp.where(causal, key, int_max), SUBLANES, jnp.minimum))
        return kmax, kmin

    kmax, kmin = lax.fori_loop(0, n_idx, idx_body, (jnp.full((SUBLANES, Q_BLOCK), INT_MIN, jnp.int32),
                                                    jnp.full((SUBLANES, Q_BLOCK), 2 ** 31 - 1, jnp.int32)))
    kmax = _reduce_rows(kmax, jnp.maximum)
    kmin = _reduce_rows(kmin, jnp.minimum)

    def count(pred):
        def body(j, cnt):
            m = jnp.where(pred(key_ref[tile(j), :], kpos + j * IDX_TILE), 1.0, 0.0)
            return cnt + _fold_rows(m, SUBLANES)
        cnt = lax.fori_loop(0, n_idx, body, jnp.zeros((SUBLANES, Q_BLOCK), F32))
        return jnp.sum(cnt, axis=0, keepdims=True)

    n_valid = (qpos[0:1, :] + 1).astype(F32)
    few = n_valid <= kf
    state = dict(
        lo=kmin, hi=kmax + 1, c_lo=n_valid, c_hi=jnp.zeros_like(n_valid),
        thr=jnp.full((1, Q_BLOCK), INT_MIN + 1, jnp.int32), c_thr=jnp.where(few, n_valid, 0.0),
        done=jnp.where(few | (kmin == kmax), 1.0, 0.0))
    state["thr"] = jnp.where(few, state["thr"], kmin)
    state["c_thr"] = jnp.where(few, state["c_thr"], n_valid)
    names = sorted(state)

    def search_cond(carry):
        it, pending = carry[0], carry[1]
        return jnp.logical_and(pending > 0.0, it < 512)

    def search_body(carry):
        it = carry[0]
        st = dict(zip(names, carry[2:]))
        lo, hi = st["lo"], st["hi"]
        f_lo = pltpu.bitcast(_sortable(lo), F32)
        f_hi = pltpu.bitcast(_sortable(hi - 1), F32)
        piv_f = _sortable(pltpu.bitcast(0.5 * f_lo + 0.5 * f_hi, jnp.int32))
        piv_i = lo + lax.shift_right_logical(hi - lo, 1)
        piv = jnp.where(it % 4 == 3, piv_i, piv_f)
        piv = jnp.minimum(jnp.maximum(piv, lo + 1), hi - 1)
        c = count(lambda kt, pos: kt >= piv)
        live = st["done"] == 0.0
        ge = c >= kf
        lo_n = jnp.where(live & ge, piv, lo)
        c_lo_n = jnp.where(live & ge, c, st["c_lo"])
        hi_n = jnp.where(live & jnp.logical_not(ge), piv, hi)
        c_hi_n = jnp.where(live & jnp.logical_not(ge), c, st["c_hi"])
        hit = live & (c == kf)
        closed = live & jnp.logical_not(hit) & (hi_n - lo_n == 1)
        new = dict(lo=lo_n, hi=hi_n, c_lo=c_lo_n, c_hi=c_hi_n,
                   thr=jnp.where(hit, piv, jnp.where(closed, lo_n, st["thr"])),
                   c_thr=jnp.where(hit, c, jnp.where(closed, c_lo_n, st["c_thr"])),
                   done=jnp.where(hit | closed, 1.0, st["done"]))
        pending = jnp.max(1.0 - new["done"])
        return (it + 1, pending) + tuple(new[n] for n in names)

    init = (jnp.int32(0), jnp.max(1.0 - state["done"])) + tuple(state[n] for n in names)
    final = dict(zip(names, lax.while_loop(search_cond, search_body, init)[2:]))
    thr, c_ge = final["thr"], final["c_thr"]
    tied = jnp.max(c_ge) > kf

    def write_sel(pred):
        def body(j, carry):
            sel = pred(key_ref[tile(j), :], kpos + j * IDX_TILE)
            sel_ref[tile(j), :] = jnp.where(sel, 1.0, 0.0).astype(BF16)
            return carry
        lax.fori_loop(0, n_idx, body, 0)

    @pl.when(jnp.logical_not(tied))
    def _():
        write_sel(lambda kt, pos: kt >= thr)

    @pl.when(tied)
    def _():
        need = kf - count(lambda kt, pos: kt > thr)
        nbits = max(1, int(np.ceil(np.log2(seq))))

        def pos_body(p, lim):
            cand = lim + lax.shift_left(jnp.int32(1), nbits - 1 - p)
            c = count(lambda kt, pos: jnp.where(kt == thr, pos, jnp.int32(seq)) < cand)
            return jnp.where(c < need, cand, lim)

        lim = lax.fori_loop(0, nbits, pos_body, jnp.zeros((1, Q_BLOCK), jnp.int32))
        lim = jnp.where(c_ge > kf, lim, jnp.int32(seq))
        write_sel(lambda kt, pos: (kt > thr) | ((kt == thr) & (pos <= lim)))

    rows = Q_PER_KV * Q_BLOCK
    pk = 2 * SUBLANES
    qgs = [jnp.concatenate([q_ref[:, (Q_PER_KV * g + e) * LANES:(Q_PER_KV * g + e + 1) * LANES]
                            for e in range(Q_PER_KV)], axis=0) for g in range(N_KV_HEADS)]
    ones_rows = jnp.ones((pk, ATT_TILE), BF16)

    def logits(j, g):
        ks = pl.ds(pl.multiple_of(j * ATT_TILE, ATT_TILE), ATT_TILE)
        return lax.dot_general(k_ref[ks, g * LANES:(g + 1) * LANES], qgs[g], _NT, preferred_element_type=F32)

    def weighted_v(j, g, p):
        ks = pl.ds(pl.multiple_of(j * ATT_TILE, ATT_TILE), ATT_TILE)
        sel = jnp.concatenate([sel_ref[ks, :]] * Q_PER_KV, axis=1)
        v_aug = jnp.concatenate([vt_ref[g * LANES:(g + 1) * LANES, ks], ones_rows], axis=0)
        return _dot(v_aug, p.astype(BF16) * sel)

    shifts = [jnp.max(logits(0, g), axis=0, keepdims=True) for g in range(N_KV_HEADS)]

    def fast_body(j, accs):
        return tuple(accs[g] + weighted_v(j, g, jnp.exp2(logits(j, g) - shifts[g])) for g in range(N_KV_HEADS))

    accs = lax.fori_loop(0, n_att, fast_body, (jnp.zeros((LANES + pk, rows), F32),) * N_KV_HEADS)
    lo_ok, hi_ok = 2.0 ** -60, 2.0 ** 60
    worst = [jnp.max(jnp.where((a[LANES:LANES + 1] >= lo_ok) & (a[LANES:LANES + 1] <= hi_ok), 0.0, 1.0))
             + jnp.max(jnp.where(jnp.abs(a[0:LANES]) < jnp.inf, 0.0, 1.0)) for a in accs]
    redo = (worst[0] + worst[1]) > 0.0

    def store(g, acc):
        o_t = acc[0:LANES] / acc[LANES:LANES + 1]
        for e in range(Q_PER_KV):
            col = (Q_PER_KV * g + e) * LANES
            o_ref[:, col:col + LANES] = o_t[:, e * Q_BLOCK:(e + 1) * Q_BLOCK].T.astype(BF16)

    @pl.when(jnp.logical_not(redo))
    def _():
        for g in range(N_KV_HEADS):
            store(g, accs[g])

    @pl.when(redo)
    def _():
        def safe_body(j, carry):
            ks = pl.ds(pl.multiple_of(j * ATT_TILE, ATT_TILE), ATT_TILE)
            sel = jnp.concatenate([sel_ref[ks, :]] * Q_PER_KV, axis=1).astype(F32)
            out = []
            for g in range(N_KV_HEADS):
                m, acc = carry[g]
                s = logits(j, g) + (sel - 1.0) * (-MASKED)
                m_new = jnp.maximum(m, jnp.max(s, axis=0, keepdims=True))
                out.append((m_new, jnp.exp2(m - m_new) * acc + weighted_v(j, g, jnp.exp2(s - m_new))))
            return tuple(out)

        init = (jnp.full((1, rows), MASKED, F32), jnp.zeros((LANES + pk, rows), F32))
        res = lax.fori_loop(0, n_att, safe_body, (init,) * N_KV_HEADS)
        for g in range(N_KV_HEADS):
            store(g, res[g][1])


def _dsa(q, k, vt, qi, kiwf, kiwb, bsz, seq):
    assert seq % IDX_TILE == 0 and seq % ATT_TILE == 0
    nq = seq // Q_BLOCK
    topk = min(TOPK_MAX, seq // 4)
    qrow = lambda w: pl.BlockSpec((Q_BLOCK, w), lambda b, i: (b * nq + i, 0))
    full = lambda w: pl.BlockSpec((seq, w), lambda b, i: (b, 0))
    return pl.pallas_call(
        functools.partial(_dsa_kernel, topk=topk, seq=seq),
        grid=(bsz, nq),
        in_specs=[qrow(ATTN_WIDTH), qrow(IDX_HEADS * LANES), qrow(LANES), full(KV_WIDTH),
                  pl.BlockSpec((KV_WIDTH, seq), lambda b, i: (0, b)), full(LANES)],
        out_specs=qrow(ATTN_WIDTH),
        out_shape=jax.ShapeDtypeStruct((bsz * seq, ATTN_WIDTH), BF16),
        scratch_shapes=[pltpu.VMEM((seq, Q_BLOCK), jnp.int32), pltpu.VMEM((seq, Q_BLOCK), BF16)],
        compiler_params=_params("parallel", "arbitrary"),
        name="dsa_attention",
    )(q, qi, kiwf, k, vt, kiwb)


def _ssd_kernel(z_ref, xbc_ref, dt_ref, cw_ref, cb_ref, dtb_ref, alog_ref, dexp_ref, nw_ref, e_ref,
                o_ref, ext_ref, xa_ref, st_ref):
    L = SSD_CHUNK
    tail = SUBLANES

    @pl.when(pl.program_id(1) == 0)
    def _():
        ext_ref[0:tail, :] = jnp.zeros((tail, SSD_CONV_DIM), F32)
        st_ref[...] = jnp.zeros_like(st_ref)

    ext_ref[tail:tail + L, :] = xbc_ref[...]
    cstep = 512
    for c0 in range(0, SSD_CONV_DIM, cstep):
        cs = slice(c0, c0 + cstep)
        u = cb_ref[:, cs]
        for j in range(SSD_CONV):
            r0 = tail - (SSD_CONV - 1) + j
            u = u + cw_ref[j:j + 1, cs] * ext_ref[r0:r0 + L, cs]
        xa_ref[:, cs] = u * _sigmoid(u)
    ext_ref[0:tail, :] = ext_ref[L:L + tail, :]

    dtv = dt_ref[...] + dtb_ref[...]
    dt = jnp.maximum(dtv, 0.0) + jnp.log1p(jnp.exp(-jnp.abs(dtv)))
    adt = dt * (-jnp.exp(alog_ref[...]))
    ri = lax.broadcasted_iota(jnp.int32, (L, L), 0)
    ci = lax.broadcasted_iota(jnp.int32, (L, L), 1)
    tril = ci <= ri
    acs = jnp.dot(jnp.where(tril, 1.0, 0.0), adt, precision=_HI, preferred_element_type=F32)
    last = acs[L - 1:L, :]
    acs_t = acs.T
    dt_t = dt.T
    stack = jnp.concatenate([dt * jnp.exp(last - acs), jnp.exp(acs)], axis=0)
    first_half = ci < SSD_HEAD_DIM

    gw = SSD_GROUP_WIDTH
    for g in range(SSD_GROUPS):
        gs = slice(g * gw, (g + 1) * gw)
        b_g = xa_ref[:, SSD_INNER + g * SSD_STATE:SSD_INNER + (g + 1) * SSD_STATE]
        c_g = xa_ref[:, SSD_INNER + SSD_GN + g * SSD_STATE:SSD_INNER + SSD_GN + (g + 1) * SSD_STATE]
        b_b, c_b = b_g.astype(BF16), c_g.astype(BF16)
        cb = lax.dot_general(c_b, b_b, _NT, preferred_element_type=F32)
        b_t = b_g.T.astype(BF16)
        ex = jnp.dot(stack, e_ref[:, gs], precision=_HI, preferred_element_type=F32)
        dtdec, eacs = ex[0:L], ex[L:2 * L]
        xs = xa_ref[:, gs]
        s_in = st_ref[:, gs]
        y = _dot(c_b, s_in.astype(BF16)) * eacs + xs * dexp_ref[:, gs]
        st_ref[:, gs] = s_in * eacs[L - 1:L, :] + _dot(b_t, (xs * dtdec).astype(BF16))
        ys = []
        for c2 in range(gw // LANES):
            xp = xs[:, c2 * LANES:(c2 + 1) * LANES].astype(BF16)
            pair = []
            for hh in range(LANES // SSD_HEAD_DIM):
                h = g * SSD_HEADS_PER_GROUP + c2 * (LANES // SSD_HEAD_DIM) + hh
                seg = acs[:, h:h + 1] - acs_t[h:h + 1, :]
                lm = jnp.exp(jnp.where(tril, seg, -jnp.inf))
                pair.append(_dot((cb * lm * dt_t[h:h + 1, :]).astype(BF16), xp))
            ys.append(jnp.where(first_half, pair[0], pair[1]))
        y = y + jnp.concatenate(ys, axis=1)
        zg = z_ref[:, gs]
        yz = y * (zg * _sigmoid(zg))
        ms = jnp.mean(yz * yz, axis=1, keepdims=True)
        o_ref[:, gs] = (yz * lax.rsqrt(ms + NORM_EPS) * nw_ref[:, gs]).astype(BF16)


def _ssd(z, xbc, dtp, cw, cb, dtb, alog, dexp, nw, emat, bsz, seq):
    L = SSD_CHUNK
    nc = seq // L
    row = lambda w: pl.BlockSpec((L, w), lambda b, c: (b * nc + c, 0))
    return pl.pallas_call(
        _ssd_kernel,
        grid=(bsz, nc),
        in_specs=[row(SSD_INNER), row(SSD_CONV_DIM), row(LANES),
                  _resident((SSD_CONV, SSD_CONV_DIM)), _resident((1, SSD_CONV_DIM)),
                  _resident((1, LANES)), _resident((1, LANES)), _resident((1, SSD_INNER)),
                  _resident((1, SSD_INNER)), _resident((LANES, SSD_INNER))],
        out_specs=row(SSD_INNER),
        out_shape=jax.ShapeDtypeStruct((bsz * seq, SSD_INNER), BF16),
        scratch_shapes=[pltpu.VMEM((L + SUBLANES, SSD_CONV_DIM), F32),
                        pltpu.VMEM((L, SSD_CONV_DIM), F32),
                        pltpu.VMEM((SSD_STATE, SSD_INNER), F32)],
        compiler_params=_params("parallel", "arbitrary"),
        name="ssd_branch",
    )(z, xbc, dtp, cw, cb, dtb, alog, dexp, nw, emat)


def _merge_kernel(x_ref, a_ref, b_ref, nw_ref, wg_ref, wpa_ref, wps_ref, wo_ref, o_ref):
    x = x_ref[...]
    h = _rms(x, nw_ref[...]).astype(BF16)
    merged = _sigmoid(_dot(h, wg_ref[:, 0:D_MODEL])) * _dot(a_ref[...], wpa_ref[...])
    merged = merged + _sigmoid(_dot(h, wg_ref[:, D_MODEL:2 * D_MODEL])) * _dot(b_ref[...], wps_ref[...])
    o_ref[...] = x + _dot(merged.astype(BF16), wo_ref[...])


def _merge(x2, a, b, nw, wg, wpa, wps, wo):
    tokens = x2.shape[0]
    tm = MERGE_ROWS
    row = lambda w: pl.BlockSpec((tm, w), lambda i: (i, 0))
    return pl.pallas_call(
        _merge_kernel,
        grid=(tokens // tm,),
        in_specs=[row(D_MODEL), row(ATTN_WIDTH), row(SSD_INNER), _resident((1, D_MODEL)),
                  _resident((D_MODEL, 2 * D_MODEL)), _resident((ATTN_WIDTH, D_MODEL)),
                  _resident((SSD_INNER, D_MODEL)), _resident((D_MODEL, D_MODEL))],
        out_specs=row(D_MODEL),
        out_shape=jax.ShapeDtypeStruct((tokens, D_MODEL), F32),
        compiler_params=_params("parallel"),
        name="merge_out",
    )(x2, a, b, nw, wg, wpa, wps, wo)


def _ffn_kernel(x_ref, nw_ref, wup_ref, cw_ref, cb_ref, wdn_ref, fnw_ref, o_ref, ext_ref, carry_ref, *, final_norm):
    tm = FFN_ROWS
    tail = SUBLANES
    nf = FFN_DIM // FFN_TILE

    @pl.when(pl.program_id(1) == 0)
    def _():
        carry_ref[...] = jnp.zeros_like(carry_ref)

    x = x_ref[...]
    h = _rms(x, nw_ref[...]).astype(BF16)
    acc = x
    for f in range(nf):
        parts = []
        for part in range(2):
            col = part * FFN_DIM + f * FFN_TILE
            cs = slice(col, col + FFN_TILE)
            ct = part * nf + f
            u = _dot(h, wup_ref[:, cs])
            ext_ref[part, 0:tail, :] = carry_ref[ct]
            ext_ref[part, tail:tail + tm, :] = u
            cv = cb_ref[:, cs] + cw_ref[2:3, cs] * u
            for j in range(FFN_CONV - 1):
                r0 = tail - (FFN_CONV - 1) + j
                cv = cv + cw_ref[j:j + 1, cs] * ext_ref[part, r0:r0 + tm, :]
            carry_ref[ct] = u[tm - tail:tm, :]
            parts.append(cv)
        act = (parts[0] * _sigmoid(parts[0]) * parts[1]).astype(BF16)
        acc = acc + _dot(act, wdn_ref[f * FFN_TILE:(f + 1) * FFN_TILE, :])
    if final_norm:
        acc = _rms(acc, fnw_ref[...])
    o_ref[...] = acc


def _ffn(x2, nw, wup, cw, cb, wdn, fnw, bsz, seq, final_norm):
    tm = FFN_ROWS
    ns = seq // tm
    nf = FFN_DIM // FFN_TILE
    row = pl.BlockSpec((tm, D_MODEL), lambda b, s: (b * ns + s, 0))
    return pl.pallas_call(
        functools.partial(_ffn_kernel, final_norm=final_norm),
        grid=(bsz, ns),
        in_specs=[row, _resident((1, D_MODEL)), _resident((D_MODEL, 2 * FFN_DIM)),
                  _resident((FFN_CONV, 2 * FFN_DIM)), _resident((1, 2 * FFN_DIM)),
                  _resident((FFN_DIM, D_MODEL)), _resident((1, D_MODEL))],
        out_specs=row,
        out_shape=jax.ShapeDtypeStruct((bsz * seq, D_MODEL), F32),
        scratch_shapes=[pltpu.VMEM((2, tm + SUBLANES, FFN_TILE), F32),
                        pltpu.VMEM((2 * nf, SUBLANES, FFN_TILE), F32)],
        compiler_params=_params("parallel", "arbitrary"),
        name="conv_glu_ffn",
    )(x2, nw, wup, cw, cb, wdn, fnw)


def _split_w_in(w_in):
    o = np.cumsum([0, ATTN_WIDTH, KV_WIDTH, KV_WIDTH, IDX_HEADS * IDX_DIM, IDX_DIM, IDX_HEADS,
                   SSD_INNER, SSD_CONV_DIM, SSD_HEADS, D_MODEL, D_MODEL]).tolist()
    seg = lambda j: w_in[:, o[j]:o[j + 1]]
    d = w_in.shape[0]
    qi = seg(3).reshape(d, IDX_HEADS, IDX_DIM)
    qi = jnp.pad(qi, ((0, 0), (0, 0), (0, LANES - IDX_DIM))).reshape(d, IDX_HEADS * LANES)
    kiw = jnp.pad(jnp.concatenate([seg(4), seg(5)], axis=1), ((0, 0), (0, LANES - IDX_DIM - IDX_HEADS)))
    w_a = jnp.concatenate([seg(0), seg(1), qi, kiw], axis=1).astype(BF16)
    w_vt = seg(2).T.astype(BF16)
    dt = jnp.pad(seg(8), ((0, 0), (0, LANES - SSD_HEADS)))
    w_b = jnp.concatenate([seg(6), seg(7), dt], axis=1).astype(BF16)
    w_g = jnp.concatenate([seg(9), seg(10)], axis=1).astype(BF16)
    return w_a, w_vt, w_b, w_g


def _lane_pad_row(v):
    return jnp.pad(v, (0, LANES - v.shape[0]))[None, :]


def kernel(x, norm_mix_w, w_in, ssd_conv_w, ssd_conv_b, ssd_dt_bias, ssd_a_log, ssd_d, ssd_norm_w, w_proj_attn, w_proj_ssd, w_out, norm_ffn_w, ffn_w_up, ffn_conv_w, ffn_conv_b, ffn_w_down, norm_final_w):
    bsz, seq, d = x.shape
    depth = w_in.shape[0]
    assert d == D_MODEL and seq % max(PROJ_ROWS_A, FFN_ROWS, MERGE_ROWS, IDX_TILE) == 0
    tabs_a = _rope_tables(seq, HEAD_DIM // ROPE_FRACTION)
    tabs_i = _rope_tables(seq, IDX_DIM // ROPE_FRACTION)
    emat = (jnp.arange(SSD_INNER)[None, :] // SSD_HEAD_DIM == jnp.arange(LANES)[:, None]).astype(F32)
    x2 = x.reshape(bsz * seq, d)
    for l in range(depth):
        w_a, w_vt, w_b, w_g = _split_w_in(w_in[l])
        nw = norm_mix_w[l][None, :]
        q, k, vt, qi, kiwf, kiwb = _proj_attn(x2, nw, w_a, w_vt, tabs_a, tabs_i, seq)
        a = _dsa(q, k, vt, qi, kiwf, kiwb, bsz, seq)
        z, xbc, dtp = _proj_ssd(x2, nw, w_b)
        b = _ssd(z, xbc, dtp, ssd_conv_w[l], ssd_conv_b[l][None, :], _lane_pad_row(ssd_dt_bias[l]),
                 _lane_pad_row(ssd_a_log[l]), jnp.repeat(ssd_d[l], SSD_HEAD_DIM)[None, :],
                 ssd_norm_w[l][None, :], emat, bsz, seq)
        x2 = _merge(x2, a, b, nw, w_g, w_proj_attn[l].astype(BF16), w_proj_ssd[l].astype(BF16),
                    w_out[l].astype(BF16))
        x2 = _ffn(x2, norm_ffn_w[l][None, :], ffn_w_up[l].astype(BF16), ffn_conv_w[l], ffn_conv_b[l][None, :],
                  ffn_w_down[l].astype(BF16), norm_final_w[None, :], bsz, seq, final_norm=(l == depth - 1))
    return x2.reshape(bsz, seq, d)
```

```python
import functools

import numpy as np
import jax
import jax.numpy as jnp
from jax import lax
from jax.experimental import pallas as pl
from jax.experimental.pallas import tpu as pltpu

D_MODEL = 1024
N_HEADS = 8
HEAD_DIM = 128
N_KV_HEADS = 2
Q_PER_KV = N_HEADS // N_KV_HEADS
ATTN_WIDTH = N_HEADS * HEAD_DIM
KV_WIDTH = N_KV_HEADS * HEAD_DIM
IDX_HEADS = 8
IDX_DIM = 64
TOPK_MAX = 256
ROPE_THETA = 500000.0
ROPE_FRACTION = 4
SSD_INNER = 2 * D_MODEL
SSD_HEAD_DIM = 64
SSD_HEADS = SSD_INNER // SSD_HEAD_DIM
SSD_GROUPS = 4
SSD_HEADS_PER_GROUP = SSD_HEADS // SSD_GROUPS
SSD_STATE = 128
SSD_CONV = 4
SSD_CHUNK = 128
SSD_GN = SSD_GROUPS * SSD_STATE
SSD_CONV_DIM = SSD_INNER + 2 * SSD_GN
SSD_GROUP_WIDTH = SSD_INNER // SSD_GROUPS
FFN_DIM = 2816
FFN_CONV = 3
NORM_EPS = 1e-6

LANES = 128
SUBLANES = 8
VMEM_LIMIT = 56 * 1024 * 1024
Q_BLOCK = 128
IDX_TILE = 512
ATT_TILE = 512
SEARCH_PASSES = 18
PROJ_ROWS_A = 512
PROJ_ROWS_B = 256
MERGE_ROWS = 512
FFN_ROWS = 512
FFN_TILE = 256
MASKED = -1e30
LOG2_E = 1.4426950408889634
INT_MIN = -(2 ** 31)

F32 = jnp.float32
BF16 = jnp.bfloat16
_NT = (((1,), (1,)), ((), ()))
_HI = lax.Precision.HIGHEST


def _params(*sem):
    return pltpu.CompilerParams(dimension_semantics=sem, vmem_limit_bytes=VMEM_LIMIT)


def _resident(shape):
    nd = len(shape)
    return pl.BlockSpec(shape, lambda *_: (0,) * nd, pipeline_mode=pl.Buffered(1))


def _rms(x, w):
    return x * lax.rsqrt(jnp.mean(x * x, axis=-1, keepdims=True) + NORM_EPS) * w


def _sigmoid(x):
    return 1.0 / (1.0 + jnp.exp(-x))


def _dot(a, b):
    return jnp.dot(a, b, preferred_element_type=F32)


def _fold_rows(x, rows, op=jnp.add):
    parts = [x[r:r + rows] for r in range(0, x.shape[0], rows)]
    while len(parts) > 1:
        parts = [op(parts[a], parts[a + 1]) for a in range(0, len(parts) - 1, 2)] + parts[len(parts) & ~1:]
    return parts[0]


def _rope_tables(seq, rot_dim):
    half = rot_dim // 2
    inv = ROPE_THETA ** (-jnp.arange(0, rot_dim, 2, dtype=F32) / rot_dim)
    ang = jnp.arange(seq, dtype=F32)[:, None] * inv[None, :]
    cos, sin = jnp.cos(ang), jnp.sin(ang)
    pad = LANES - rot_dim
    c = jnp.concatenate([cos, cos, jnp.ones((seq, pad), F32)], axis=1)
    s1 = jnp.concatenate([-sin, jnp.zeros((seq, half + pad), F32)], axis=1)
    s2 = jnp.concatenate([jnp.zeros((seq, half), F32), sin, jnp.zeros((seq, pad), F32)], axis=1)
    return c, s1, s2


def _rope(t, c, s1, s2, half):
    return t * c + pltpu.roll(t, LANES - half, 1) * s1 + pltpu.roll(t, half, 1) * s2


_A_Q = 0
_A_K = _A_Q + ATTN_WIDTH
_A_QI = _A_K + KV_WIDTH
_A_KIW = _A_QI + IDX_HEADS * LANES
_A_COLS = _A_KIW + LANES


def _proj_attn_kernel(x_ref, nw_ref, w_ref, wvt_ref, ca_ref, sa1_ref, sa2_ref, ci_ref, si1_ref, si2_ref,
                      q_ref, k_ref, vt_ref, qi_ref, kiwf_ref, kiwb_ref):
    h = _rms(x_ref[...], nw_ref[...]).astype(BF16)
    ca, sa1, sa2 = ca_ref[...], sa1_ref[...], sa2_ref[...]
    ci, si1, si2 = ci_ref[...], si1_ref[...], si2_ref[...]
    ha = HEAD_DIM // ROPE_FRACTION // 2
    hi = IDX_DIM // ROPE_FRACTION // 2
    scale = HEAD_DIM ** -0.5 * LOG2_E
    for j in range(N_HEADS):
        t = _dot(h, w_ref[:, _A_Q + j * LANES:_A_Q + (j + 1) * LANES])
        q_ref[:, j * LANES:(j + 1) * LANES] = (_rope(t, ca, sa1, sa2, ha) * scale).astype(BF16)
    for j in range(N_KV_HEADS):
        t = _dot(h, w_ref[:, _A_K + j * LANES:_A_K + (j + 1) * LANES])
        k_ref[:, j * LANES:(j + 1) * LANES] = _rope(t, ca, sa1, sa2, ha).astype(BF16)
    vt_ref[...] = lax.dot_general(wvt_ref[...], h, _NT, preferred_element_type=F32).astype(BF16)
    for j in range(IDX_HEADS):
        t = _dot(h, w_ref[:, _A_QI + j * LANES:_A_QI + (j + 1) * LANES])
        qi_ref[:, j * LANES:(j + 1) * LANES] = _rope(t, ci, si1, si2, hi).astype(BF16)
    t = _rope(_dot(h, w_ref[:, _A_KIW:_A_KIW + LANES]), ci, si1, si2, hi)
    kiwf_ref[...] = t
    kiwb_ref[...] = t.astype(BF16)


def _proj_attn(x2, nw, w_a, w_vt, tabs_a, tabs_i, seq):
    tokens = x2.shape[0]
    tm = PROJ_ROWS_A
    per_seq = seq // tm
    row = lambda w: pl.BlockSpec((tm, w), lambda i: (i, 0))
    tab = pl.BlockSpec((tm, LANES), lambda i: (i % per_seq, 0))
    sds = jax.ShapeDtypeStruct
    return pl.pallas_call(
        _proj_attn_kernel,
        grid=(tokens // tm,),
        in_specs=[row(D_MODEL), _resident((1, D_MODEL)), _resident((D_MODEL, _A_COLS)),
                  _resident((KV_WIDTH, D_MODEL))] + [tab] * 6,
        out_specs=[row(ATTN_WIDTH), row(KV_WIDTH), pl.BlockSpec((KV_WIDTH, tm), lambda i: (0, i)),
                   row(IDX_HEADS * LANES), row(LANES), row(LANES)],
        out_shape=[sds((tokens, ATTN_WIDTH), BF16), sds((tokens, KV_WIDTH), BF16), sds((KV_WIDTH, tokens), BF16),
                   sds((tokens, IDX_HEADS * LANES), BF16), sds((tokens, LANES), F32), sds((tokens, LANES), BF16)],
        compiler_params=_params("parallel"),
        name="proj_attn",
    )(x2, nw, w_a, w_vt, *tabs_a, *tabs_i)


_B_Z = 0
_B_XBC = _B_Z + SSD_INNER
_B_DT = _B_XBC + SSD_CONV_DIM
_B_COLS = _B_DT + LANES


def _proj_ssd_kernel(x_ref, nw_ref, w_ref, z_ref, xbc_ref, dt_ref):
    h = _rms(x_ref[...], nw_ref[...]).astype(BF16)
    z_ref[...] = _dot(h, w_ref[:, _B_Z:_B_Z + SSD_INNER])
    xbc_ref[...] = _dot(h, w_ref[:, _B_XBC:_B_XBC + SSD_CONV_DIM])
    dt_ref[...] = _dot(h, w_ref[:, _B_DT:_B_DT + LANES])


def _proj_ssd(x2, nw, w_b):
    tokens = x2.shape[0]
    tm = PROJ_ROWS_B
    row = lambda w: pl.BlockSpec((tm, w), lambda i: (i, 0))
    widths = (SSD_INNER, SSD_CONV_DIM, LANES)
    return pl.pallas_call(
        _proj_ssd_kernel,
        grid=(tokens // tm,),
        in_specs=[row(D_MODEL), _resident((1, D_MODEL)), _resident((D_MODEL, _B_COLS))],
        out_specs=[row(w) for w in widths],
        out_shape=[jax.ShapeDtypeStruct((tokens, w), F32) for w in widths],
        compiler_params=_params("parallel"),
        name="proj_ssd",
    )(x2, nw, w_b)


def _sortable(x):
    return x ^ ((x >> 31) & jnp.int32(0x7FFFFFFF))


def _reduce_rows(x, op):
    for sh in (4, 2, 1):
        x = op(x, pltpu.roll(x, sh, 0))
    return x[0:1, :]


def _dsa_kernel(q_ref, qi_ref, wq_ref, k_ref, vt_ref, kib_ref, o_ref, key_ref, sel_ref, *, topk, seq):
    i = pl.program_id(1)
    n_idx = (i + IDX_TILE // Q_BLOCK) // (IDX_TILE // Q_BLOCK)
    n_att = (i + ATT_TILE // Q_BLOCK) // (ATT_TILE // Q_BLOCK)
    kpos = lax.broadcasted_iota(jnp.int32, (IDX_TILE, Q_BLOCK), 0)
    qpos = lax.broadcasted_iota(jnp.int32, (IDX_TILE, Q_BLOCK), 1) + i * Q_BLOCK
    w_t = wq_ref[...].T
    qi_all = jnp.concatenate([qi_ref[:, h * LANES:(h + 1) * LANES] for h in range(IDX_HEADS)], axis=0)
    kf = float(topk)
    int_max = jnp.int32(2 ** 31 - 1)

    def tile(j):
        return pl.ds(pl.multiple_of(j * IDX_TILE, IDX_TILE), IDX_TILE)

    def idx_body(j, carry):
        kmax, kmin = carry
        s = lax.dot_general(kib_ref[tile(j), :], qi_all, _NT, preferred_element_type=F32)
        acc = jnp.zeros((IDX_TILE, Q_BLOCK), F32)
        for h in range(IDX_HEADS):
            acc = acc + w_t[IDX_DIM + h:IDX_DIM + h + 1, :] * jnp.maximum(s[:, h * Q_BLOCK:(h + 1) * Q_BLOCK], 0.0)
        key = _sortable(pltpu.bitcast(acc, jnp.int32))
        causal = kpos + j * IDX_TILE <= qpos
        low = jnp.where(causal, key, jnp.int32(INT_MIN))
        key_ref[tile(j), :] = low
        kmax = jnp.maximum(kmax, _fold_rows(low, SUBLANES, jnp.maximum))
        kmin = jnp.minimum(kmin, _fold_rows(jnp.where(causal, key, int_max), SUBLANES, jnp.minimum))
        return kmax, kmin

    kmax, kmin = lax.fori_loop(0, n_idx, idx_body, (jnp.full((SUBLANES, Q_BLOCK), INT_MIN, jnp.int32),
                                                    jnp.full((SUBLANES, Q_BLOCK), 2 ** 31 - 1, jnp.int32)))
    kmax = _reduce_rows(kmax, jnp.maximum)
    kmin = _reduce_rows(kmin, jnp.minimum)

    def count(pred):
        def body(j, cnt):
            m = jnp.where(pred(key_ref[tile(j), :], kpos + j * IDX_TILE), 1.0, 0.0)
            return cnt + _fold_rows(m, SUBLANES)
        cnt = lax.fori_loop(0, n_idx, body, jnp.zeros((SUBLANES, Q_BLOCK), F32))
        return jnp.sum(cnt, axis=0, keepdims=True)

    n_valid = (qpos[0:1, :] + 1).astype(F32)
    few = n_valid <= kf
    flat = kmin == kmax
    start = few | flat
    thr0 = jnp.where(few, jnp.int32(INT_MIN + 1), kmin)
    state = (jnp.where(start, thr0, kmin), jnp.where(start, thr0, kmax + 1), n_valid)

    def search_pass(it, st):
        lo, hi, c_lo = st
        f_lo = pltpu.bitcast(_sortable(lo), F32)
        f_hi = pltpu.bitcast(_sortable(hi - 1), F32)
        piv = _sortable(pltpu.bitcast(0.5 * f_lo + 0.5 * f_hi, jnp.int32))
        piv = jnp.where(it % 4 == 3, lo + lax.shift_right_logical(hi - lo, 1), piv)
        piv = jnp.where(it == 0, jnp.int32(1), piv)
        piv = jnp.where(it == 1, jnp.where(hi == 1, jnp.int32(-1), piv), piv)
        piv = jnp.minimum(jnp.maximum(piv, lo + 1), hi - 1)
        c = count(lambda kt, pos: kt >= piv)
        live = lo != hi
        ge = c >= kf
        hit = c == kf
        lo_n = jnp.where(live & ge, piv, lo)
        c_lo_n = jnp.where(live & ge, c, c_lo)
        hi_n = jnp.where(live & jnp.logical_not(ge), piv, hi)
        hi_n = jnp.where(live & (hit | (hi_n - lo_n == 1)), lo_n, hi_n)
        return lo_n, hi_n, c_lo_n

    state = lax.fori_loop(0, SEARCH_PASSES, search_pass, state)

    def pending(st):
        return jnp.max(jnp.where(st[0] != st[1], 1.0, 0.0))

    def more_cond(carry):
        return jnp.logical_and(carry[1] > 0.0, carry[0] < 1024)

    def more_body(carry):
        st = search_pass(carry[0] + 1, search_pass(carry[0], carry[2:]))
        return (carry[0] + 2, pending(st)) + st

    thr, _, c_ge = lax.while_loop(more_cond, more_body, (jnp.int32(SEARCH_PASSES), pending(state)) + state)[2:]
    tied = jnp.max(c_ge) > kf

    def write_sel(pred):
        def body(j, carry):
            sel = pred(key_ref[tile(j), :], kpos + j * IDX_TILE)
            sel_ref[tile(j), :] = jnp.where(sel, 1.0, 0.0).astype(BF16)
            return carry
        lax.fori_loop(0, n_idx, body, 0)

    @pl.when(jnp.logical_not(tied))
    def _():
        write_sel(lambda kt, pos: kt >= thr)

    @pl.when(tied)
    def _():
        need = kf - count(lambda kt, pos: kt > thr)
        nbits = max(1, int(np.ceil(np.log2(seq))))

        def pos_body(p, lim):
            cand = lim + lax.shift_left(jnp.int32(1), nbits - 1 - p)
            c = count(lambda kt, pos: jnp.where(kt == thr, pos, jnp.int32(seq)) < cand)
            return jnp.where(c < need, cand, lim)

        lim = lax.fori_loop(0, nbits, pos_body, jnp.zeros((1, Q_BLOCK), jnp.int32))
        lim = jnp.where(c_ge > kf, lim, jnp.int32(seq))
        write_sel(lambda kt, pos: (kt > thr) | ((kt == thr) & (pos <= lim)))

    rows = Q_PER_KV * Q_BLOCK
    pk = 2 * SUBLANES
    qgs = [jnp.concatenate([q_ref[:, (Q_PER_KV * g + e) * LANES:(Q_PER_KV * g + e + 1) * LANES]
                            for e in range(Q_PER_KV)], axis=0) for g in range(N_KV_HEADS)]
    ones_rows = jnp.ones((pk, ATT_TILE), BF16)

    def logits(j, g):
        ks = pl.ds(pl.multiple_of(j * ATT_TILE, ATT_TILE), ATT_TILE)
        return lax.dot_general(k_ref[ks, g * LANES:(g + 1) * LANES], qgs[g], _NT, preferred_element_type=F32)

    def weighted_v(j, g, p):
        ks = pl.ds(pl.multiple_of(j * ATT_TILE, ATT_TILE), ATT_TILE)
        sel = jnp.concatenate([sel_ref[ks, :]] * Q_PER_KV, axis=1)
        v_aug = jnp.concatenate([vt_ref[g * LANES:(g + 1) * LANES, ks], ones_rows], axis=0)
        return _dot(v_aug, p.astype(BF16) * sel)

    shifts = [jnp.max(logits(0, g), axis=0, keepdims=True) for g in range(N_KV_HEADS)]

    def fast_body(j, accs):
        return tuple(accs[g] + weighted_v(j, g, jnp.exp2(logits(j, g) - shifts[g])) for g in range(N_KV_HEADS))

    accs = lax.fori_loop(0, n_att, fast_body, (jnp.zeros((LANES + pk, rows), F32),) * N_KV_HEADS)
    lo_ok, hi_ok = 2.0 ** -60, 2.0 ** 60
    worst = [jnp.max(jnp.where((a[LANES:LANES + 1] >= lo_ok) & (a[LANES:LANES + 1] <= hi_ok), 0.0, 1.0))
             + jnp.max(jnp.where(jnp.abs(a[0:LANES]) < jnp.inf, 0.0, 1.0)) for a in accs]
    redo = (worst[0] + worst[1]) > 0.0

    def store(g, acc):
        o_t = acc[0:LANES] / acc[LANES:LANES + 1]
        for e in range(Q_PER_KV):
            col = (Q_PER_KV * g + e) * LANES
            o_ref[:, col:col + LANES] = o_t[:, e * Q_BLOCK:(e + 1) * Q_BLOCK].T.astype(BF16)

    @pl.when(jnp.logical_not(redo))
    def _():
        for g in range(N_KV_HEADS):
            store(g, accs[g])

    @pl.when(redo)
    def _():
        def safe_body(j, carry):
            ks = pl.ds(pl.multiple_of(j * ATT_TILE, ATT_TILE), ATT_TILE)
            sel = jnp.concatenate([sel_ref[ks, :]] * Q_PER_KV, axis=1).astype(F32)
            out = []
            for g in range(N_KV_HEADS):
                m, acc = carry[g]
                s = logits(j, g) + (sel - 1.0) * (-MASKED)
                m_new = jnp.maximum(m, jnp.max(s, axis=0, keepdims=True))
                out.append((m_new, jnp.exp2(m - m_new) * acc + weighted_v(j, g, jnp.exp2(s - m_new))))
            return tuple(out)

        init = (jnp.full((1, rows), MASKED, F32), jnp.zeros((LANES + pk, rows), F32))
        res = lax.fori_loop(0, n_att, safe_body, (init,) * N_KV_HEADS)
        for g in range(N_KV_HEADS):
            store(g, res[g][1])


def _dsa(q, k, vt, qi, kiwf, kiwb, bsz, seq):
    assert seq % IDX_TILE == 0 and seq % ATT_TILE == 0
    nq = seq // Q_BLOCK
    topk = min(TOPK_MAX, seq // 4)
    qrow = lambda w: pl.BlockSpec((Q_BLOCK, w), lambda b, i: (b * nq + i, 0))
    full = lambda w: pl.BlockSpec((seq, w), lambda b, i: (b, 0))
    return pl.pallas_call(
        functools.partial(_dsa_kernel, topk=topk, seq=seq),
        grid=(bsz, nq),
        in_specs=[qrow(ATTN_WIDTH), qrow(IDX_HEADS * LANES), qrow(LANES), full(KV_WIDTH),
                  pl.BlockSpec((KV_WIDTH, seq), lambda b, i: (0, b)), full(LANES)],
        out_specs=qrow(ATTN_WIDTH),
        out_shape=jax.ShapeDtypeStruct((bsz * seq, ATTN_WIDTH), BF16),
        scratch_shapes=[pltpu.VMEM((seq, Q_BLOCK), jnp.int32), pltpu.VMEM((seq, Q_BLOCK), BF16)],
        compiler_params=_params("parallel", "arbitrary"),
        name="dsa_attention",
    )(q, qi, kiwf, k, vt, kiwb)


def _ssd_kernel(z_ref, xbc_ref, dt_ref, cw_ref, cb_ref, dtb_ref, alog_ref, dexp_ref, nw_ref, e_ref,
                o_ref, ext_ref, xa_ref, st_ref):
    L = SSD_CHUNK
    tail = SUBLANES

    @pl.when(pl.program_id(1) == 0)
    def _():
        ext_ref[0:tail, :] = jnp.zeros((tail, SSD_CONV_DIM), F32)
        st_ref[...] = jnp.zeros_like(st_ref)

    ext_ref[tail:tail + L, :] = xbc_ref[...]
    cstep = 512
    for c0 in range(0, SSD_CONV_DIM, cstep):
        cs = slice(c0, c0 + cstep)
        u = cb_ref[:, cs]
        for j in range(SSD_CONV):
            r0 = tail - (SSD_CONV - 1) + j
            u = u + cw_ref[j:j + 1, cs] * ext_ref[r0:r0 + L, cs]
        xa_ref[:, cs] = u * _sigmoid(u)
    ext_ref[0:tail, :] = ext_ref[L:L + tail, :]

    dtv = dt_ref[...] + dtb_ref[...]
    dt = jnp.maximum(dtv, 0.0) + jnp.log1p(jnp.exp(-jnp.abs(dtv)))
    adt = dt * (-jnp.exp(alog_ref[...]))
    ri = lax.broadcasted_iota(jnp.int32, (L, L), 0)
    ci = lax.broadcasted_iota(jnp.int32, (L, L), 1)
    tril = ci <= ri
    acs = jnp.dot(jnp.where(tril, 1.0, 0.0), adt, precision=_HI, preferred_element_type=F32)
    last = acs[L - 1:L, :]
    acs_t = acs.T
    dt_t = dt.T
    stack = jnp.concatenate([dt * jnp.exp(last - acs), jnp.exp(acs)], axis=0)
    first_half = ci < SSD_HEAD_DIM

    gw = SSD_GROUP_WIDTH
    for g in range(SSD_GROUPS):
        gs = slice(g * gw, (g + 1) * gw)
        b_g = xa_ref[:, SSD_INNER + g * SSD_STATE:SSD_INNER + (g + 1) * SSD_STATE]
        c_g = xa_ref[:, SSD_INNER + SSD_GN + g * SSD_STATE:SSD_INNER + SSD_GN + (g + 1) * SSD_STATE]
        b_b, c_b = b_g.astype(BF16), c_g.astype(BF16)
        cb = lax.dot_general(c_b, b_b, _NT, preferred_element_type=F32)
        b_t = b_g.T.astype(BF16)
        ex = jnp.dot(stack, e_ref[:, gs], precision=_HI, preferred_element_type=F32)
        dtdec, eacs = ex[0:L], ex[L:2 * L]
        xs = xa_ref[:, gs]
        s_in = st_ref[:, gs]
        y = _dot(c_b, s_in.astype(BF16)) * eacs + xs * dexp_ref[:, gs]
        st_ref[:, gs] = s_in * eacs[L - 1:L, :] + _dot(b_t, (xs * dtdec).astype(BF16))
        ys = []
        for c2 in range(gw // LANES):
            xp = xs[:, c2 * LANES:(c2 + 1) * LANES].astype(BF16)
            pair = []
            for hh in range(LANES // SSD_HEAD_DIM):
                h = g * SSD_HEADS_PER_GROUP + c2 * (LANES // SSD_HEAD_DIM) + hh
                seg = acs[:, h:h + 1] - acs_t[h:h + 1, :]
                lm = jnp.exp(jnp.where(tril, seg, -jnp.inf))
                pair.append(_dot((cb * lm * dt_t[h:h + 1, :]).astype(BF16), xp))
            ys.append(jnp.where(first_half, pair[0], pair[1]))
        y = y + jnp.concatenate(ys, axis=1)
        zg = z_ref[:, gs]
        yz = y * (zg * _sigmoid(zg))
        ms = jnp.mean(yz * yz, axis=1, keepdims=True)
        o_ref[:, gs] = (yz * lax.rsqrt(ms + NORM_EPS) * nw_ref[:, gs]).astype(BF16)


def _ssd(z, xbc, dtp, cw, cb, dtb, alog, dexp, nw, emat, bsz, seq):
    L = SSD_CHUNK
    nc = seq // L
    row = lambda w: pl.BlockSpec((L, w), lambda b, c: (b * nc + c, 0))
    return pl.pallas_call(
        _ssd_kernel,
        grid=(bsz, nc),
        in_specs=[row(SSD_INNER), row(SSD_CONV_DIM), row(LANES),
                  _resident((SSD_CONV, SSD_CONV_DIM)), _resident((1, SSD_CONV_DIM)),
                  _resident((1, LANES)), _resident((1, LANES)), _resident((1, SSD_INNER)),
                  _resident((1, SSD_INNER)), _resident((LANES, SSD_INNER))],
        out_specs=row(SSD_INNER),
        out_shape=jax.ShapeDtypeStruct((bsz * seq, SSD_INNER), BF16),
        scratch_shapes=[pltpu.VMEM((L + SUBLANES, SSD_CONV_DIM), F32),
                        pltpu.VMEM((L, SSD_CONV_DIM), F32),
                        pltpu.VMEM((SSD_STATE, SSD_INNER), F32)],
        compiler_params=_params("parallel", "arbitrary"),
        name="ssd_branch",
    )(z, xbc, dtp, cw, cb, dtb, alog, dexp, nw, emat)


def _merge_kernel(x_ref, a_ref, b_ref, nw_ref, wg_ref, wpa_ref, wps_ref, wo_ref, o_ref):
    x = x_ref[...]
    h = _rms(x, nw_ref[...]).astype(BF16)
    merged = _sigmoid(_dot(h, wg_ref[:, 0:D_MODEL])) * _dot(a_ref[...], wpa_ref[...])
    merged = merged + _sigmoid(_dot(h, wg_ref[:, D_MODEL:2 * D_MODEL])) * _dot(b_ref[...], wps_ref[...])
    o_ref[...] = x + _dot(merged.astype(BF16), wo_ref[...])


def _merge(x2, a, b, nw, wg, wpa, wps, wo):
    tokens = x2.shape[0]
    tm = MERGE_ROWS
    row = lambda w: pl.BlockSpec((tm, w), lambda i: (i, 0))
    return pl.pallas_call(
        _merge_kernel,
        grid=(tokens // tm,),
        in_specs=[row(D_MODEL), row(ATTN_WIDTH), row(SSD_INNER), _resident((1, D_MODEL)),
                  _resident((D_MODEL, 2 * D_MODEL)), _resident((ATTN_WIDTH, D_MODEL)),
                  _resident((SSD_INNER, D_MODEL)), _resident((D_MODEL, D_MODEL))],
        out_specs=row(D_MODEL),
        out_shape=jax.ShapeDtypeStruct((tokens, D_MODEL), F32),
        compiler_params=_params("parallel"),
        name="merge_out",
    )(x2, a, b, nw, wg, wpa, wps, wo)


def _ffn_kernel(x_ref, nw_ref, wup_ref, cw_ref, cb_ref, wdn_ref, fnw_ref, o_ref, ext_ref, carry_ref, *, final_norm):
    tm = FFN_ROWS
    tail = SUBLANES
    nf = FFN_DIM // FFN_TILE

    @pl.when(pl.program_id(1) == 0)
    def _():
        carry_ref[...] = jnp.zeros_like(carry_ref)

    x = x_ref[...]
    h = _rms(x, nw_ref[...]).astype(BF16)
    acc = x
    for f in range(nf):
        parts = []
        for part in range(2):
            col = part * FFN_DIM + f * FFN_TILE
            cs = slice(col, col + FFN_TILE)
            ct = part * nf + f
            u = _dot(h, wup_ref[:, cs])
            ext_ref[part, 0:tail, :] = carry_ref[ct]
            ext_ref[part, tail:tail + tm, :] = u
            cv = cb_ref[:, cs] + cw_ref[2:3, cs] * u
            for j in range(FFN_CONV - 1):
                r0 = tail - (FFN_CONV - 1) + j
                cv = cv + cw_ref[j:j + 1, cs] * ext_ref[part, r0:r0 + tm, :]
            carry_ref[ct] = u[tm - tail:tm, :]
            parts.append(cv)
        act = (parts[0] * _sigmoid(parts[0]) * parts[1]).astype(BF16)
        acc = acc + _dot(act, wdn_ref[f * FFN_TILE:(f + 1) * FFN_TILE, :])
    if final_norm:
        acc = _rms(acc, fnw_ref[...])
    o_ref[...] = acc


def _ffn(x2, nw, wup, cw, cb, wdn, fnw, bsz, seq, final_norm):
    tm = FFN_ROWS
    ns = seq // tm
    nf = FFN_DIM // FFN_TILE
    row = pl.BlockSpec((tm, D_MODEL), lambda b, s: (b * ns + s, 0))
    return pl.pallas_call(
        functools.partial(_ffn_kernel, final_norm=final_norm),
        grid=(bsz, ns),
        in_specs=[row, _resident((1, D_MODEL)), _resident((D_MODEL, 2 * FFN_DIM)),
                  _resident((FFN_CONV, 2 * FFN_DIM)), _resident((1, 2 * FFN_DIM)),
                  _resident((FFN_DIM, D_MODEL)), _resident((1, D_MODEL))],
        out_specs=row,
        out_shape=jax.ShapeDtypeStruct((bsz * seq, D_MODEL), F32),
        scratch_shapes=[pltpu.VMEM((2, tm + SUBLANES, FFN_TILE), F32),
                        pltpu.VMEM((2 * nf, SUBLANES, FFN_TILE), F32)],
        compiler_params=_params("parallel", "arbitrary"),
        name="conv_glu_ffn",
    )(x2, nw, wup, cw, cb, wdn, fnw)


def _split_w_in(w_in):
    o = np.cumsum([0, ATTN_WIDTH, KV_WIDTH, KV_WIDTH, IDX_HEADS * IDX_DIM, IDX_DIM, IDX_HEADS,
                   SSD_INNER, SSD_CONV_DIM, SSD_HEADS, D_MODEL, D_MODEL]).tolist()
    seg = lambda j: w_in[:, o[j]:o[j + 1]]
    d = w_in.shape[0]
    qi = seg(3).reshape(d, IDX_HEADS, IDX_DIM)
    qi = jnp.pad(qi, ((0, 0), (0, 0), (0, LANES - IDX_DIM))).reshape(d, IDX_HEADS * LANES)
    kiw = jnp.pad(jnp.concatenate([seg(4), seg(5)], axis=1), ((0, 0), (0, LANES - IDX_DIM - IDX_HEADS)))
    w_a = jnp.concatenate([seg(0), seg(1), qi, kiw], axis=1).astype(BF16)
    w_vt = seg(2).T.astype(BF16)
    dt = jnp.pad(seg(8), ((0, 0), (0, LANES - SSD_HEADS)))
    w_b = jnp.concatenate([seg(6), seg(7), dt], axis=1).astype(BF16)
    w_g = jnp.concatenate([seg(9), seg(10)], axis=1).astype(BF16)
    return w_a, w_vt, w_b, w_g


def _lane_pad_row(v):
    return jnp.pad(v, (0, LANES - v.shape[0]))[None, :]


def kernel(x, norm_mix_w, w_in, ssd_conv_w, ssd_conv_b, ssd_dt_bias, ssd_a_log, ssd_d, ssd_norm_w, w_proj_attn, w_proj_ssd, w_out, norm_ffn_w, ffn_w_up, ffn_conv_w, ffn_conv_b, ffn_w_down, norm_final_w):
    bsz, seq, d = x.shape
    depth = w_in.shape[0]
    assert d == D_MODEL and seq % max(PROJ_ROWS_A, FFN_ROWS, MERGE_ROWS, IDX_TILE) == 0
    tabs_a = _rope_tables(seq, HEAD_DIM // ROPE_FRACTION)
    tabs_i = _rope_tables(seq, IDX_DIM // ROPE_FRACTION)
    emat = (jnp.arange(SSD_INNER)[None, :] // SSD_HEAD_DIM == jnp.arange(LANES)[:, None]).astype(F32)
    x2 = x.reshape(bsz * seq, d)
    for l in range(depth):
        w_a, w_vt, w_b, w_g = _split_w_in(w_in[l])
        nw = norm_mix_w[l][None, :]
        q, k, vt, qi, kiwf, kiwb = _proj_attn(x2, nw, w_a, w_vt, tabs_a, tabs_i, seq)
        a = _dsa(q, k, vt, qi, kiwf, kiwb, bsz, seq)
        z, xbc, dtp = _proj_ssd(x2, nw, w_b)
        b = _ssd(z, xbc, dtp, ssd_conv_w[l], ssd_conv_b[l][None, :], _lane_pad_row(ssd_dt_bias[l]),
                 _lane_pad_row(ssd_a_log[l]), jnp.repeat(ssd_d[l], SSD_HEAD_DIM)[None, :],
                 ssd_norm_w[l][None, :], emat, bsz, seq)
        x2 = _merge(x2, a, b, nw, w_g, w_proj_attn[l].astype(BF16), w_proj_ssd[l].astype(BF16),
                    w_out[l].astype(BF16))
        x2 = _ffn(x2, norm_ffn_w[l][None, :], ffn_w_up[l].astype(BF16), ffn_conv_w[l], ffn_conv_b[l][None, :],
                  ffn_w_down[l].astype(BF16), norm_final_w[None, :], bsz, seq, final_norm=(l == depth - 1))
    return x2.reshape(bsz, seq, d)
```

```python
import functools

import numpy as np
import jax
import jax.numpy as jnp
from jax import lax
from jax.experimental import pallas as pl
from jax.experimental.pallas import tpu as pltpu

D_MODEL = 1024
N_HEADS = 8
HEAD_DIM = 128
N_KV_HEADS = 2
Q_PER_KV = N_HEADS // N_KV_HEADS
ATTN_WIDTH = N_HEADS * HEAD_DIM
KV_WIDTH = N_KV_HEADS * HEAD_DIM
IDX_HEADS = 8
IDX_DIM = 64
TOPK_MAX = 256
ROPE_THETA = 500000.0
ROPE_FRACTION = 4
SSD_INNER = 2 * D_MODEL
SSD_HEAD_DIM = 64
SSD_HEADS = SSD_INNER // SSD_HEAD_DIM
SSD_GROUPS = 4
SSD_HEADS_PER_GROUP = SSD_HEADS // SSD_GROUPS
SSD_STATE = 128
SSD_CONV = 4
SSD_CHUNK = 128
SSD_GN = SSD_GROUPS * SSD_STATE
SSD_CONV_DIM = SSD_INNER + 2 * SSD_GN
SSD_GROUP_WIDTH = SSD_INNER // SSD_GROUPS
FFN_DIM = 2816
FFN_CONV = 3
NORM_EPS = 1e-6

LANES = 128
SUBLANES = 8
VMEM_LIMIT = 56 * 1024 * 1024
Q_BLOCK = 128
IDX_TILE = 512
ATT_TILE = 512
SEARCH_PASSES = 18
PROJ_ROWS_A = 512
PROJ_ROWS_B = 256
MERGE_ROWS = 512
FFN_ROWS = 512
FFN_TILE = 256
MASKED = -1e30
LOG2_E = 1.4426950408889634
INT_MIN = -(2 ** 31)

F32 = jnp.float32
BF16 = jnp.bfloat16
_NT = (((1,), (1,)), ((), ()))
_HI = lax.Precision.HIGHEST


def _params(*sem):
    return pltpu.CompilerParams(dimension_semantics=sem, vmem_limit_bytes=VMEM_LIMIT)


def _resident(shape):
    nd = len(shape)
    return pl.BlockSpec(shape, lambda *_: (0,) * nd, pipeline_mode=pl.Buffered(1))


def _rms(x, w):
    return x * lax.rsqrt(jnp.mean(x * x, axis=-1, keepdims=True) + NORM_EPS) * w


def _sigmoid(x):
    return 1.0 / (1.0 + jnp.exp(-x))


def _dot(a, b):
    return jnp.dot(a, b, preferred_element_type=F32)


def _fold_rows(x, rows, op=jnp.add):
    parts = [x[r:r + rows] for r in range(0, x.shape[0], rows)]
    while len(parts) > 1:
        parts = [op(parts[a], parts[a + 1]) for a in range(0, len(parts) - 1, 2)] + parts[len(parts) & ~1:]
    return parts[0]


def _rope_tables(seq, rot_dim):
    half = rot_dim // 2
    inv = ROPE_THETA ** (-jnp.arange(0, rot_dim, 2, dtype=F32) / rot_dim)
    ang = jnp.arange(seq, dtype=F32)[:, None] * inv[None, :]
    cos, sin = jnp.cos(ang), jnp.sin(ang)
    pad = LANES - rot_dim
    c = jnp.concatenate([cos, cos, jnp.ones((seq, pad), F32)], axis=1)
    s1 = jnp.concatenate([-sin, jnp.zeros((seq, half + pad), F32)], axis=1)
    s2 = jnp.concatenate([jnp.zeros((seq, half), F32), sin, jnp.zeros((seq, pad), F32)], axis=1)
    return c, s1, s2


def _rope(t, c, s1, s2, half):
    return t * c + pltpu.roll(t, LANES - half, 1) * s1 + pltpu.roll(t, half, 1) * s2


_A_Q = 0
_A_K = _A_Q + ATTN_WIDTH
_A_QI = _A_K + KV_WIDTH
_A_KIW = _A_QI + IDX_HEADS * LANES
_A_COLS = _A_KIW + LANES


def _proj_attn_kernel(x_ref, nw_ref, w_ref, wvt_ref, ca_ref, sa1_ref, sa2_ref, ci_ref, si1_ref, si2_ref,
                      q_ref, k_ref, vt_ref, qi_ref, kiwf_ref, kiwb_ref):
    h = _rms(x_ref[...], nw_ref[...]).astype(BF16)
    ca, sa1, sa2 = ca_ref[...], sa1_ref[...], sa2_ref[...]
    ci, si1, si2 = ci_ref[...], si1_ref[...], si2_ref[...]
    ha = HEAD_DIM // ROPE_FRACTION // 2
    hi = IDX_DIM // ROPE_FRACTION // 2
    scale = HEAD_DIM ** -0.5 * LOG2_E
    for j in range(N_HEADS):
        t = _dot(h, w_ref[:, _A_Q + j * LANES:_A_Q + (j + 1) * LANES])
        q_ref[:, j * LANES:(j + 1) * LANES] = (_rope(t, ca, sa1, sa2, ha) * scale).astype(BF16)
    for j in range(N_KV_HEADS):
        t = _dot(h, w_ref[:, _A_K + j * LANES:_A_K + (j + 1) * LANES])
        k_ref[:, j * LANES:(j + 1) * LANES] = _rope(t, ca, sa1, sa2, ha).astype(BF16)
    vt_ref[...] = lax.dot_general(wvt_ref[...], h, _NT, preferred_element_type=F32).astype(BF16)
    for j in range(IDX_HEADS):
        t = _dot(h, w_ref[:, _A_QI + j * LANES:_A_QI + (j + 1) * LANES])
        qi_ref[:, j * LANES:(j + 1) * LANES] = _rope(t, ci, si1, si2, hi).astype(BF16)
    t = _rope(_dot(h, w_ref[:, _A_KIW:_A_KIW + LANES]), ci, si1, si2, hi)
    kiwf_ref[...] = t
    kiwb_ref[...] = t.astype(BF16)


def _proj_attn(x2, nw, w_a, w_vt, tabs_a, tabs_i, seq):
    tokens = x2.shape[0]
    tm = PROJ_ROWS_A
    per_seq = seq // tm
    row = lambda w: pl.BlockSpec((tm, w), lambda i: (i, 0))
    tab = pl.BlockSpec((tm, LANES), lambda i: (i % per_seq, 0))
    sds = jax.ShapeDtypeStruct
    return pl.pallas_call(
        _proj_attn_kernel,
        grid=(tokens // tm,),
        in_specs=[row(D_MODEL), _resident((1, D_MODEL)), _resident((D_MODEL, _A_COLS)),
                  _resident((KV_WIDTH, D_MODEL))] + [tab] * 6,
        out_specs=[row(ATTN_WIDTH), row(KV_WIDTH), pl.BlockSpec((KV_WIDTH, tm), lambda i: (0, i)),
                   row(IDX_HEADS * LANES), row(LANES), row(LANES)],
        out_shape=[sds((tokens, ATTN_WIDTH), BF16), sds((tokens, KV_WIDTH), BF16), sds((KV_WIDTH, tokens), BF16),
                   sds((tokens, IDX_HEADS * LANES), BF16), sds((tokens, LANES), F32), sds((tokens, LANES), BF16)],
        compiler_params=_params("parallel"),
        name="proj_attn",
    )(x2, nw, w_a, w_vt, *tabs_a, *tabs_i)


_B_Z = 0
_B_XBC = _B_Z + SSD_INNER
_B_DT = _B_XBC + SSD_CONV_DIM
_B_COLS = _B_DT + LANES


def _proj_ssd_kernel(x_ref, nw_ref, w_ref, cw_ref, cb_ref, zs_ref, xa_ref, dt_ref, ext_ref):
    tm = PROJ_ROWS_B
    tail = SUBLANES

    @pl.when(pl.program_id(1) == 0)
    def _():
        ext_ref[0:tail, :] = jnp.zeros((tail, SSD_CONV_DIM), F32)

    h = _rms(x_ref[...], nw_ref[...]).astype(BF16)
    z = _dot(h, w_ref[:, _B_Z:_B_Z + SSD_INNER])
    zs_ref[...] = z * _sigmoid(z)
    dt_ref[...] = _dot(h, w_ref[:, _B_DT:_B_DT + LANES])
    cstep = 512
    for c0 in range(0, SSD_CONV_DIM, cstep):
        cs = slice(c0, c0 + cstep)
        ext_ref[tail:tail + tm, cs] = _dot(h, w_ref[:, _B_XBC + c0:_B_XBC + c0 + cstep])
        ue = ext_ref[:, cs]
        u = cb_ref[:, cs] + cw_ref[SSD_CONV - 1:SSD_CONV, cs] * ue[tail:tail + tm]
        for j in range(SSD_CONV - 1):
            back = SSD_CONV - 1 - j
            u = u + cw_ref[j:j + 1, cs] * pltpu.roll(ue, back, 0)[tail:tail + tm]
        xa_ref[:, cs] = u * _sigmoid(u)
        ext_ref[0:tail, cs] = ue[tm:tm + tail]


def _proj_ssd(x2, nw, w_b, cw, cb, bsz, seq):
    tm = PROJ_ROWS_B
    ns = seq // tm
    row = lambda w: pl.BlockSpec((tm, w), lambda b, s: (b * ns + s, 0))
    widths = (SSD_INNER, SSD_CONV_DIM, LANES)
    return pl.pallas_call(
        _proj_ssd_kernel,
        grid=(bsz, ns),
        in_specs=[row(D_MODEL), _resident((1, D_MODEL)), _resident((D_MODEL, _B_COLS)),
                  _resident((SSD_CONV, SSD_CONV_DIM)), _resident((1, SSD_CONV_DIM))],
        out_specs=[row(w) for w in widths],
        out_shape=[jax.ShapeDtypeStruct((bsz * seq, w), F32) for w in widths],
        scratch_shapes=[pltpu.VMEM((tm + SUBLANES, SSD_CONV_DIM), F32)],
        compiler_params=_params("parallel", "arbitrary"),
        name="proj_ssd",
    )(x2, nw, w_b, cw, cb)


def _sortable(x):
    return x ^ ((x >> 31) & jnp.int32(0x7FFFFFFF))


def _reduce_rows(x, op):
    for sh in (4, 2, 1):
        x = op(x, pltpu.roll(x, sh, 0))
    return x[0:1, :]


def _dsa_kernel(q_ref, qi_ref, wq_ref, k_ref, vt_ref, kib_ref, o_ref, key_ref, sel_ref, *, topk, seq):
    i = pl.program_id(1)
    n_idx = (i + IDX_TILE // Q_BLOCK) // (IDX_TILE // Q_BLOCK)
    n_att = (i + ATT_TILE // Q_BLOCK) // (ATT_TILE // Q_BLOCK)
    kpos = lax.broadcasted_iota(jnp.int32, (IDX_TILE, Q_BLOCK), 0)
    qpos = lax.broadcasted_iota(jnp.int32, (IDX_TILE, Q_BLOCK), 1) + i * Q_BLOCK
    w_t = wq_ref[...].T
    qi_all = jnp.concatenate([qi_ref[:, h * LANES:(h + 1) * LANES] for h in range(IDX_HEADS)], axis=0)
    kf = float(topk)
    int_max = jnp.int32(2 ** 31 - 1)

    def tile(j):
        return pl.ds(pl.multiple_of(j * IDX_TILE, IDX_TILE), IDX_TILE)

    def idx_body(j, carry):
        kmax, kmin = carry
        s = lax.dot_general(kib_ref[tile(j), :], qi_all, _NT, preferred_element_type=F32)
        acc = jnp.zeros((IDX_TILE, Q_BLOCK), F32)
        for h in range(IDX_HEADS):
            acc = acc + w_t[IDX_DIM + h:IDX_DIM + h + 1, :] * jnp.maximum(s[:, h * Q_BLOCK:(h + 1) * Q_BLOCK], 0.0)
        key = _sortable(pltpu.bitcast(acc, jnp.int32))
        causal = kpos + j * IDX_TILE <= qpos
        low = jnp.where(causal, key, jnp.int32(INT_MIN))
        key_ref[tile(j), :] = low
        kmax = jnp.maximum(kmax, _fold_rows(low, SUBLANES, jnp.maximum))
        kmin = jnp.minimum(kmin, _fold_rows(jnp.where(causal, key, int_max), SUBLANES, jnp.minimum))
        return kmax, kmin

    kmax, kmin = lax.fori_loop(0, n_idx, idx_body, (jnp.full((SUBLANES, Q_BLOCK), INT_MIN, jnp.int32),
                                                    jnp.full((SUBLANES, Q_BLOCK), 2 ** 31 - 1, jnp.int32)))
    kmax = _reduce_rows(kmax, jnp.maximum)
    kmin = _reduce_rows(kmin, jnp.minimum)

    def count(pred):
        def body(j, cnt):
            m = jnp.where(pred(key_ref[tile(j), :], kpos + j * IDX_TILE), 1.0, 0.0)
            return cnt + _fold_rows(m, SUBLANES)
        cnt = lax.fori_loop(0, n_idx, body, jnp.zeros((SUBLANES, Q_BLOCK), F32))
        return jnp.sum(cnt, axis=0, keepdims=True)

    n_valid = (qpos[0:1, :] + 1).astype(F32)
    few = n_valid <= kf
    flat = kmin == kmax
    start = few | flat
    thr0 = jnp.where(few, jnp.int32(INT_MIN + 1), kmin)
    state = (jnp.where(start, thr0, kmin), jnp.where(start, thr0, kmax + 1), n_valid)

    def search_pass(it, st):
        lo, hi, c_lo = st
        f_lo = pltpu.bitcast(_sortable(lo), F32)
        f_hi = pltpu.bitcast(_sortable(hi - 1), F32)
        piv = _sortable(pltpu.bitcast(0.5 * f_lo + 0.5 * f_hi, jnp.int32))
        piv = jnp.where(it % 4 == 3, lo + lax.shift_right_logical(hi - lo, 1), piv)
        piv = jnp.where(it == 0, jnp.int32(1), piv)
        piv = jnp.where(it == 1, jnp.where(hi == 1, jnp.int32(-1), piv), piv)
        piv = jnp.minimum(jnp.maximum(piv, lo + 1), hi - 1)
        c = count(lambda kt, pos: kt >= piv)
        live = lo != hi
        ge = c >= kf
        hit = c == kf
        lo_n = jnp.where(live & ge, piv, lo)
        c_lo_n = jnp.where(live & ge, c, c_lo)
        hi_n = jnp.where(live & jnp.logical_not(ge), piv, hi)
        hi_n = jnp.where(live & (hit | (hi_n - lo_n == 1)), lo_n, hi_n)
        return lo_n, hi_n, c_lo_n

    state = lax.fori_loop(0, SEARCH_PASSES, search_pass, state)

    def pending(st):
        return jnp.max(jnp.where(st[0] != st[1], 1.0, 0.0))

    def more_cond(carry):
        return jnp.logical_and(carry[1] > 0.0, carry[0] < 1024)

    def more_body(carry):
        st = search_pass(carry[0] + 1, search_pass(carry[0], carry[2:]))
        return (carry[0] + 2, pending(st)) + st

    thr, _, c_ge = lax.while_loop(more_cond, more_body, (jnp.int32(SEARCH_PASSES), pending(state)) + state)[2:]
    tied = jnp.max(c_ge) > kf

    def write_sel(pred):
        def body(j, carry):
            sel = pred(key_ref[tile(j), :], kpos + j * IDX_TILE)
            sel_ref[tile(j), :] = jnp.where(sel, 1.0, 0.0).astype(BF16)
            return carry
        lax.fori_loop(0, n_idx, body, 0)

    @pl.when(jnp.logical_not(tied))
    def _():
        write_sel(lambda kt, pos: kt >= thr)

    @pl.when(tied)
    def _():
        need = kf - count(lambda kt, pos: kt > thr)
        nbits = max(1, int(np.ceil(np.log2(seq))))

        def pos_body(p, lim):
            cand = lim + lax.shift_left(jnp.int32(1), nbits - 1 - p)
            c = count(lambda kt, pos: jnp.where(kt == thr, pos, jnp.int32(seq)) < cand)
            return jnp.where(c < need, cand, lim)

        lim = lax.fori_loop(0, nbits, pos_body, jnp.zeros((1, Q_BLOCK), jnp.int32))
        lim = jnp.where(c_ge > kf, lim, jnp.int32(seq))
        write_sel(lambda kt, pos: (kt > thr) | ((kt == thr) & (pos <= lim)))

    rows = Q_PER_KV * Q_BLOCK
    pk = 2 * SUBLANES
    qgs = [jnp.concatenate([q_ref[:, (Q_PER_KV * g + e) * LANES:(Q_PER_KV * g + e + 1) * LANES]
                            for e in range(Q_PER_KV)], axis=0) for g in range(N_KV_HEADS)]
    ones_rows = jnp.ones((pk, ATT_TILE), BF16)

    def logits(j, g):
        ks = pl.ds(pl.multiple_of(j * ATT_TILE, ATT_TILE), ATT_TILE)
        return lax.dot_general(k_ref[ks, g * LANES:(g + 1) * LANES], qgs[g], _NT, preferred_element_type=F32)

    def weighted_v(j, g, p):
        ks = pl.ds(pl.multiple_of(j * ATT_TILE, ATT_TILE), ATT_TILE)
        sel = jnp.concatenate([sel_ref[ks, :]] * Q_PER_KV, axis=1)
        v_aug = jnp.concatenate([vt_ref[g * LANES:(g + 1) * LANES, ks], ones_rows], axis=0)
        return _dot(v_aug, p.astype(BF16) * sel)

    shifts = [jnp.max(logits(0, g), axis=0, keepdims=True) for g in range(N_KV_HEADS)]

    def fast_body(j, accs):
        return tuple(accs[g] + weighted_v(j, g, jnp.exp2(logits(j, g) - shifts[g])) for g in range(N_KV_HEADS))

    accs = lax.fori_loop(0, n_att, fast_body, (jnp.zeros((LANES + pk, rows), F32),) * N_KV_HEADS)
    lo_ok, hi_ok = 2.0 ** -60, 2.0 ** 60
    worst = [jnp.max(jnp.where((a[LANES:LANES + 1] >= lo_ok) & (a[LANES:LANES + 1] <= hi_ok), 0.0, 1.0))
             + jnp.max(jnp.where(jnp.abs(a[0:LANES]) < jnp.inf, 0.0, 1.0)) for a in accs]
    redo = (worst[0] + worst[1]) > 0.0

    def store(g, acc):
        o_t = acc[0:LANES] / acc[LANES:LANES + 1]
        for e in range(Q_PER_KV):
            col = (Q_PER_KV * g + e) * LANES
            o_ref[:, col:col + LANES] = o_t[:, e * Q_BLOCK:(e + 1) * Q_BLOCK].T.astype(BF16)

    @pl.when(jnp.logical_not(redo))
    def _():
        for g in range(N_KV_HEADS):
            store(g, accs[g])

    @pl.when(redo)
    def _():
        def safe_body(j, carry):
            ks = pl.ds(pl.multiple_of(j * ATT_TILE, ATT_TILE), ATT_TILE)
            sel = jnp.concatenate([sel_ref[ks, :]] * Q_PER_KV, axis=1).astype(F32)
            out = []
            for g in range(N_KV_HEADS):
                m, acc = carry[g]
                s = logits(j, g) + (sel - 1.0) * (-MASKED)
                m_new = jnp.maximum(m, jnp.max(s, axis=0, keepdims=True))
                out.append((m_new, jnp.exp2(m - m_new) * acc + weighted_v(j, g, jnp.exp2(s - m_new))))
            return tuple(out)

        init = (jnp.full((1, rows), MASKED, F32), jnp.zeros((LANES + pk, rows), F32))
        res = lax.fori_loop(0, n_att, safe_body, (init,) * N_KV_HEADS)
        for g in range(N_KV_HEADS):
            store(g, res[g][1])


def _dsa(q, k, vt, qi, kiwf, kiwb, bsz, seq):
    assert seq % IDX_TILE == 0 and seq % ATT_TILE == 0
    nq = seq // Q_BLOCK
    topk = min(TOPK_MAX, seq // 4)
    qrow = lambda w: pl.BlockSpec((Q_BLOCK, w), lambda b, i: (b * nq + i, 0))
    full = lambda w: pl.BlockSpec((seq, w), lambda b, i: (b, 0))
    return pl.pallas_call(
        functools.partial(_dsa_kernel, topk=topk, seq=seq),
        grid=(bsz, nq),
        in_specs=[qrow(ATTN_WIDTH), qrow(IDX_HEADS * LANES), qrow(LANES), full(KV_WIDTH),
                  pl.BlockSpec((KV_WIDTH, seq), lambda b, i: (0, b)), full(LANES)],
        out_specs=qrow(ATTN_WIDTH),
        out_shape=jax.ShapeDtypeStruct((bsz * seq, ATTN_WIDTH), BF16),
        scratch_shapes=[pltpu.VMEM((seq, Q_BLOCK), jnp.int32), pltpu.VMEM((seq, Q_BLOCK), BF16)],
        compiler_params=_params("parallel", "arbitrary"),
        name="dsa_attention",
    )(q, qi, kiwf, k, vt, kiwb)


def _split3(x):
    hi = x.astype(BF16)
    r = x - hi.astype(F32)
    mid = r.astype(BF16)
    return hi, mid, (r - mid.astype(F32)).astype(BF16)


def _ssd_kernel(zs_ref, xa_ref, dt_ref, dtb_ref, alog_ref, dexp_ref, nw_ref, e_ref, o_ref, st_ref):
    L = SSD_CHUNK

    @pl.when(pl.program_id(1) == 0)
    def _():
        st_ref[...] = jnp.zeros_like(st_ref)

    dtv = dt_ref[...] + dtb_ref[...]
    dt = jnp.maximum(dtv, 0.0) + jnp.log1p(jnp.exp(-jnp.abs(dtv)))
    adt = dt * (-jnp.exp(alog_ref[...]))
    ri = lax.broadcasted_iota(jnp.int32, (L, L), 0)
    ci = lax.broadcasted_iota(jnp.int32, (L, L), 1)
    tril = ci <= ri
    tril_b = jnp.where(tril, 1.0, 0.0).astype(BF16)
    acs = sum(_dot(tril_b, t) for t in _split3(adt))
    last = acs[L - 1:L, :]
    acs_t = acs.T
    dt_t = dt.T
    stack = _split3(jnp.concatenate([dt * jnp.exp(last - acs), jnp.exp(acs)], axis=0))
    first_half = ci < SSD_HEAD_DIM

    gw = SSD_GROUP_WIDTH
    for g in range(SSD_GROUPS):
        gs = slice(g * gw, (g + 1) * gw)
        b_g = xa_ref[:, SSD_INNER + g * SSD_STATE:SSD_INNER + (g + 1) * SSD_STATE]
        c_g = xa_ref[:, SSD_INNER + SSD_GN + g * SSD_STATE:SSD_INNER + SSD_GN + (g + 1) * SSD_STATE]
        b_b, c_b = b_g.astype(BF16), c_g.astype(BF16)
        cb = lax.dot_general(c_b, b_b, _NT, preferred_element_type=F32)
        b_t = b_g.T.astype(BF16)
        ex = sum(_dot(t, e_ref[:, gs]) for t in stack)
        dtdec, eacs = ex[0:L], ex[L:2 * L]
        xs = xa_ref[:, gs]
        s_in = st_ref[:, gs]
        y = _dot(c_b, s_in.astype(BF16)) * eacs + xs * dexp_ref[:, gs]
        st_ref[:, gs] = s_in * eacs[L - 1:L, :] + _dot(b_t, (xs * dtdec).astype(BF16))
        ys = []
        for c2 in range(gw // LANES):
            xp = xs[:, c2 * LANES:(c2 + 1) * LANES].astype(BF16)
            pair = []
            for hh in range(LANES // SSD_HEAD_DIM):
                h = g * SSD_HEADS_PER_GROUP + c2 * (LANES // SSD_HEAD_DIM) + hh
                seg = acs[:, h:h + 1] - acs_t[h:h + 1, :]
                lm = jnp.exp(jnp.where(tril, seg, -jnp.inf))
                pair.append(_dot((cb * lm * dt_t[h:h + 1, :]).astype(BF16), xp))
            ys.append(jnp.where(first_half, pair[0], pair[1]))
        y = y + jnp.concatenate(ys, axis=1)
        yz = y * zs_ref[:, gs]
        ms = jnp.mean(yz * yz, axis=1, keepdims=True)
        o_ref[:, gs] = (yz * lax.rsqrt(ms + NORM_EPS) * nw_ref[:, gs]).astype(BF16)


def _ssd(zs, xa, dtp, dtb, alog, dexp, nw, emat, bsz, seq):
    L = SSD_CHUNK
    nc = seq // L
    row = lambda w: pl.BlockSpec((L, w), lambda b, c: (b * nc + c, 0))
    return pl.pallas_call(
        _ssd_kernel,
        grid=(bsz, nc),
        in_specs=[row(SSD_INNER), row(SSD_CONV_DIM), row(LANES),
                  _resident((1, LANES)), _resident((1, LANES)), _resident((1, SSD_INNER)),
                  _resident((1, SSD_INNER)), _resident((LANES, SSD_INNER))],
        out_specs=row(SSD_INNER),
        out_shape=jax.ShapeDtypeStruct((bsz * seq, SSD_INNER), BF16),
        scratch_shapes=[pltpu.VMEM((SSD_STATE, SSD_INNER), F32)],
        compiler_params=_params("parallel", "arbitrary"),
        name="ssd_branch",
    )(zs, xa, dtp, dtb, alog, dexp, nw, emat)


def _merge_kernel(x_ref, a_ref, b_ref, nw_ref, wg_ref, wpa_ref, wps_ref, wo_ref, o_ref):
    x = x_ref[...]
    h = _rms(x, nw_ref[...]).astype(BF16)
    merged = _sigmoid(_dot(h, wg_ref[:, 0:D_MODEL])) * _dot(a_ref[...], wpa_ref[...])
    merged = merged + _sigmoid(_dot(h, wg_ref[:, D_MODEL:2 * D_MODEL])) * _dot(b_ref[...], wps_ref[...])
    o_ref[...] = x + _dot(merged.astype(BF16), wo_ref[...])


def _merge(x2, a, b, nw, wg, wpa, wps, wo):
    tokens = x2.shape[0]
    tm = MERGE_ROWS
    row = lambda w: pl.BlockSpec((tm, w), lambda i: (i, 0))
    return pl.pallas_call(
        _merge_kernel,
        grid=(tokens // tm,),
        in_specs=[row(D_MODEL), row(ATTN_WIDTH), row(SSD_INNER), _resident((1, D_MODEL)),
                  _resident((D_MODEL, 2 * D_MODEL)), _resident((ATTN_WIDTH, D_MODEL)),
                  _resident((SSD_INNER, D_MODEL)), _resident((D_MODEL, D_MODEL))],
        out_specs=row(D_MODEL),
        out_shape=jax.ShapeDtypeStruct((tokens, D_MODEL), F32),
        compiler_params=_params("parallel"),
        name="merge_out",
    )(x2, a, b, nw, wg, wpa, wps, wo)


def _ffn_kernel(x_ref, nw_ref, wup_ref, cw_ref, cb_ref, wdn_ref, fnw_ref, o_ref, ext_ref, carry_ref, *, final_norm):
    tm = FFN_ROWS
    tail = SUBLANES
    nf = FFN_DIM // FFN_TILE

    @pl.when(pl.program_id(1) == 0)
    def _():
        carry_ref[...] = jnp.zeros_like(carry_ref)

    x = x_ref[...]
    h = _rms(x, nw_ref[...]).astype(BF16)
    acc = x
    for f in range(nf):
        parts = []
        for part in range(2):
            col = part * FFN_DIM + f * FFN_TILE
            cs = slice(col, col + FFN_TILE)
            ct = part * nf + f
            u = _dot(h, wup_ref[:, cs])
            ext_ref[part, 0:tail, :] = carry_ref[ct]
            ext_ref[part, tail:tail + tm, :] = u
            cv = cb_ref[:, cs] + cw_ref[2:3, cs] * u
            for j in range(FFN_CONV - 1):
                r0 = tail - (FFN_CONV - 1) + j
                cv = cv + cw_ref[j:j + 1, cs] * ext_ref[part, r0:r0 + tm, :]
            carry_ref[ct] = u[tm - tail:tm, :]
            parts.append(cv)
        act = (parts[0] * _sigmoid(parts[0]) * parts[1]).astype(BF16)
        acc = acc + _dot(act, wdn_ref[f * FFN_TILE:(f + 1) * FFN_TILE, :])
    if final_norm:
        acc = _rms(acc, fnw_ref[...])
    o_ref[...] = acc


def _ffn(x2, nw, wup, cw, cb, wdn, fnw, bsz, seq, final_norm):
    tm = FFN_ROWS
    ns = seq // tm
    nf = FFN_DIM // FFN_TILE
    row = pl.BlockSpec((tm, D_MODEL), lambda b, s: (b * ns + s, 0))
    return pl.pallas_call(
        functools.partial(_ffn_kernel, final_norm=final_norm),
        grid=(bsz, ns),
        in_specs=[row, _resident((1, D_MODEL)), _resident((D_MODEL, 2 * FFN_DIM)),
                  _resident((FFN_CONV, 2 * FFN_DIM)), _resident((1, 2 * FFN_DIM)),
                  _resident((FFN_DIM, D_MODEL)), _resident((1, D_MODEL))],
        out_specs=row,
        out_shape=jax.ShapeDtypeStruct((bsz * seq, D_MODEL), F32),
        scratch_shapes=[pltpu.VMEM((2, tm + SUBLANES, FFN_TILE), F32),
                        pltpu.VMEM((2 * nf, SUBLANES, FFN_TILE), F32)],
        compiler_params=_params("parallel", "arbitrary"),
        name="conv_glu_ffn",
    )(x2, nw, wup, cw, cb, wdn, fnw)


def _split_w_in(w_in):
    o = np.cumsum([0, ATTN_WIDTH, KV_WIDTH, KV_WIDTH, IDX_HEADS * IDX_DIM, IDX_DIM, IDX_HEADS,
                   SSD_INNER, SSD_CONV_DIM, SSD_HEADS, D_MODEL, D_MODEL]).tolist()
    seg = lambda j: w_in[:, o[j]:o[j + 1]]
    d = w_in.shape[0]
    qi = seg(3).reshape(d, IDX_HEADS, IDX_DIM)
    qi = jnp.pad(qi, ((0, 0), (0, 0), (0, LANES - IDX_DIM))).reshape(d, IDX_HEADS * LANES)
    kiw = jnp.pad(jnp.concatenate([seg(4), seg(5)], axis=1), ((0, 0), (0, LANES - IDX_DIM - IDX_HEADS)))
    w_a = jnp.concatenate([seg(0), seg(1), qi, kiw], axis=1).astype(BF16)
    w_vt = seg(2).T.astype(BF16)
    dt = jnp.pad(seg(8), ((0, 0), (0, LANES - SSD_HEADS)))
    w_b = jnp.concatenate([seg(6), seg(7), dt], axis=1).astype(BF16)
    w_g = jnp.concatenate([seg(9), seg(10)], axis=1).astype(BF16)
    return w_a, w_vt, w_b, w_g


def _lane_pad_row(v):
    return jnp.pad(v, (0, LANES - v.shape[0]))[None, :]


def kernel(x, norm_mix_w, w_in, ssd_conv_w, ssd_conv_b, ssd_dt_bias, ssd_a_log, ssd_d, ssd_norm_w, w_proj_attn, w_proj_ssd, w_out, norm_ffn_w, ffn_w_up, ffn_conv_w, ffn_conv_b, ffn_w_down, norm_final_w):
    bsz, seq, d = x.shape
    depth = w_in.shape[0]
    assert d == D_MODEL and seq % max(PROJ_ROWS_A, FFN_ROWS, MERGE_ROWS, IDX_TILE) == 0
    tabs_a = _rope_tables(seq, HEAD_DIM // ROPE_FRACTION)
    tabs_i = _rope_tables(seq, IDX_DIM // ROPE_FRACTION)
    emat = (jnp.arange(SSD_INNER)[None, :] // SSD_HEAD_DIM == jnp.arange(LANES)[:, None]).astype(BF16)
    x2 = x.reshape(bsz * seq, d)
    for l in range(depth):
        w_a, w_vt, w_b, w_g = _split_w_in(w_in[l])
        nw = norm_mix_w[l][None, :]
        q, k, vt, qi, kiwf, kiwb = _proj_attn(x2, nw, w_a, w_vt, tabs_a, tabs_i, seq)
        a = _dsa(q, k, vt, qi, kiwf, kiwb, bsz, seq)
        zs, xa, dtp = _proj_ssd(x2, nw, w_b, ssd_conv_w[l], ssd_conv_b[l][None, :], bsz, seq)
        b = _ssd(zs, xa, dtp, _lane_pad_row(ssd_dt_bias[l]), _lane_pad_row(ssd_a_log[l]),
                 jnp.repeat(ssd_d[l], SSD_HEAD_DIM)[None, :], ssd_norm_w[l][None, :], emat, bsz, seq)
        x2 = _merge(x2, a, b, nw, w_g, w_proj_attn[l].astype(BF16), w_proj_ssd[l].astype(BF16),
                    w_out[l].astype(BF16))
        x2 = _ffn(x2, norm_ffn_w[l][None, :], ffn_w_up[l].astype(BF16), ffn_conv_w[l], ffn_conv_b[l][None, :],
                  ffn_w_down[l].astype(BF16), norm_final_w[None, :], bsz, seq, final_norm=(l == depth - 1))
    return x2.reshape(bsz, seq, d)
```

```python
import functools

import numpy as np
import jax
import jax.numpy as jnp
from jax import lax
from jax.experimental import pallas as pl
from jax.experimental.pallas import tpu as pltpu

D_MODEL = 1024
N_HEADS = 8
HEAD_DIM = 128
N_KV_HEADS = 2
Q_PER_KV = N_HEADS // N_KV_HEADS
ATTN_WIDTH = N_HEADS * HEAD_DIM
KV_WIDTH = N_KV_HEADS * HEAD_DIM
IDX_HEADS = 8
IDX_DIM = 64
TOPK_MAX = 256
ROPE_THETA = 500000.0
ROPE_FRACTION = 4
SSD_INNER = 2 * D_MODEL
SSD_HEAD_DIM = 64
SSD_HEADS = SSD_INNER // SSD_HEAD_DIM
SSD_GROUPS = 4
SSD_HEADS_PER_GROUP = SSD_HEADS // SSD_GROUPS
SSD_STATE = 128
SSD_CONV = 4
SSD_CHUNK = 128
SSD_GN = SSD_GROUPS * SSD_STATE
SSD_CONV_DIM = SSD_INNER + 2 * SSD_GN
SSD_GROUP_WIDTH = SSD_INNER // SSD_GROUPS
FFN_DIM = 2816
FFN_CONV = 3
NORM_EPS = 1e-6

LANES = 128
SUBLANES = 8
VMEM_LIMIT = 56 * 1024 * 1024
Q_BLOCK = 256
IDX_TILE = 512
ATT_TILE = 512
SEARCH_PASSES = 18
PROJ_ROWS_A = 512
PROJ_ROWS_B = 256
MERGE_ROWS = 512
FFN_ROWS = 512
FFN_TILE = 256
MASKED = -1e30
LOG2_E = 1.4426950408889634
INT_MIN = -(2 ** 31)

F32 = jnp.float32
BF16 = jnp.bfloat16
_NT = (((1,), (1,)), ((), ()))
_HI = lax.Precision.HIGHEST


def _params(*sem):
    return pltpu.CompilerParams(dimension_semantics=sem, vmem_limit_bytes=VMEM_LIMIT)


def _resident(shape):
    nd = len(shape)
    return pl.BlockSpec(shape, lambda *_: (0,) * nd, pipeline_mode=pl.Buffered(1))


def _rms(x, w):
    return x * lax.rsqrt(jnp.mean(x * x, axis=-1, keepdims=True) + NORM_EPS) * w


def _sigmoid(x):
    return 1.0 / (1.0 + jnp.exp(-x))


def _dot(a, b):
    return jnp.dot(a, b, preferred_element_type=F32)


def _fold_rows(x, rows, op=jnp.add):
    parts = [x[r:r + rows] for r in range(0, x.shape[0], rows)]
    while len(parts) > 1:
        parts = [op(parts[a], parts[a + 1]) for a in range(0, len(parts) - 1, 2)] + parts[len(parts) & ~1:]
    return parts[0]


def _rope_tables(seq, rot_dim):
    half = rot_dim // 2
    inv = ROPE_THETA ** (-jnp.arange(0, rot_dim, 2, dtype=F32) / rot_dim)
    ang = jnp.arange(seq, dtype=F32)[:, None] * inv[None, :]
    cos, sin = jnp.cos(ang), jnp.sin(ang)
    pad = LANES - rot_dim
    c = jnp.concatenate([cos, cos, jnp.ones((seq, pad), F32)], axis=1)
    s1 = jnp.concatenate([-sin, jnp.zeros((seq, half + pad), F32)], axis=1)
    s2 = jnp.concatenate([jnp.zeros((seq, half), F32), sin, jnp.zeros((seq, pad), F32)], axis=1)
    return c, s1, s2


def _rope(t, c, s1, s2, half):
    return t * c + pltpu.roll(t, LANES - half, 1) * s1 + pltpu.roll(t, half, 1) * s2


_A_Q = 0
_A_K = _A_Q + ATTN_WIDTH
_A_QI = _A_K + KV_WIDTH
_A_KIW = _A_QI + IDX_HEADS * LANES
_A_COLS = _A_KIW + LANES


def _proj_attn_kernel(x_ref, nw_ref, w_ref, wvt_ref, ca_ref, sa1_ref, sa2_ref, ci_ref, si1_ref, si2_ref,
                      q_ref, k_ref, vt_ref, qi_ref, kiwf_ref, kiwb_ref):
    h = _rms(x_ref[...], nw_ref[...]).astype(BF16)
    ca, sa1, sa2 = ca_ref[...], sa1_ref[...], sa2_ref[...]
    ci, si1, si2 = ci_ref[...], si1_ref[...], si2_ref[...]
    ha = HEAD_DIM // ROPE_FRACTION // 2
    hi = IDX_DIM // ROPE_FRACTION // 2
    scale = HEAD_DIM ** -0.5 * LOG2_E
    for j in range(N_HEADS):
        t = _dot(h, w_ref[:, _A_Q + j * LANES:_A_Q + (j + 1) * LANES])
        q_ref[:, j * LANES:(j + 1) * LANES] = (_rope(t, ca, sa1, sa2, ha) * scale).astype(BF16)
    for j in range(N_KV_HEADS):
        t = _dot(h, w_ref[:, _A_K + j * LANES:_A_K + (j + 1) * LANES])
        k_ref[:, j * LANES:(j + 1) * LANES] = _rope(t, ca, sa1, sa2, ha).astype(BF16)
    vt_ref[...] = lax.dot_general(wvt_ref[...], h, _NT, preferred_element_type=F32).astype(BF16)
    for j in range(IDX_HEADS):
        t = _dot(h, w_ref[:, _A_QI + j * LANES:_A_QI + (j + 1) * LANES])
        qi_ref[:, j * LANES:(j + 1) * LANES] = _rope(t, ci, si1, si2, hi).astype(BF16)
    t = _rope(_dot(h, w_ref[:, _A_KIW:_A_KIW + LANES]), ci, si1, si2, hi)
    kiwf_ref[...] = t
    kiwb_ref[...] = t.astype(BF16)


def _proj_attn(x2, nw, w_a, w_vt, tabs_a, tabs_i, seq):
    tokens = x2.shape[0]
    tm = PROJ_ROWS_A
    per_seq = seq // tm
    row = lambda w: pl.BlockSpec((tm, w), lambda i: (i, 0))
    tab = pl.BlockSpec((tm, LANES), lambda i: (i % per_seq, 0))
    sds = jax.ShapeDtypeStruct
    return pl.pallas_call(
        _proj_attn_kernel,
        grid=(tokens // tm,),
        in_specs=[row(D_MODEL), _resident((1, D_MODEL)), _resident((D_MODEL, _A_COLS)),
                  _resident((KV_WIDTH, D_MODEL))] + [tab] * 6,
        out_specs=[row(ATTN_WIDTH), row(KV_WIDTH), pl.BlockSpec((KV_WIDTH, tm), lambda i: (0, i)),
                   row(IDX_HEADS * LANES), row(LANES), row(LANES)],
        out_shape=[sds((tokens, ATTN_WIDTH), BF16), sds((tokens, KV_WIDTH), BF16), sds((KV_WIDTH, tokens), BF16),
                   sds((tokens, IDX_HEADS * LANES), BF16), sds((tokens, LANES), F32), sds((tokens, LANES), BF16)],
        compiler_params=_params("parallel"),
        name="proj_attn",
    )(x2, nw, w_a, w_vt, *tabs_a, *tabs_i)


_B_Z = 0
_B_XBC = _B_Z + SSD_INNER
_B_DT = _B_XBC + SSD_CONV_DIM
_B_COLS = _B_DT + LANES


def _proj_ssd_kernel(x_ref, nw_ref, w_ref, cw_ref, cb_ref, zs_ref, xa_ref, dt_ref, ext_ref):
    tm = PROJ_ROWS_B
    tail = SUBLANES

    @pl.when(pl.program_id(1) == 0)
    def _():
        ext_ref[0:tail, :] = jnp.zeros((tail, SSD_CONV_DIM), F32)

    h = _rms(x_ref[...], nw_ref[...]).astype(BF16)
    z = _dot(h, w_ref[:, _B_Z:_B_Z + SSD_INNER])
    zs_ref[...] = z * _sigmoid(z)
    dt_ref[...] = _dot(h, w_ref[:, _B_DT:_B_DT + LANES])
    cstep = 512
    for c0 in range(0, SSD_CONV_DIM, cstep):
        cs = slice(c0, c0 + cstep)
        ext_ref[tail:tail + tm, cs] = _dot(h, w_ref[:, _B_XBC + c0:_B_XBC + c0 + cstep])
        ue = ext_ref[:, cs]
        u = cb_ref[:, cs] + cw_ref[SSD_CONV - 1:SSD_CONV, cs] * ue[tail:tail + tm]
        for j in range(SSD_CONV - 1):
            back = SSD_CONV - 1 - j
            u = u + cw_ref[j:j + 1, cs] * pltpu.roll(ue, back, 0)[tail:tail + tm]
        xa_ref[:, cs] = u * _sigmoid(u)
        ext_ref[0:tail, cs] = ue[tm:tm + tail]


def _proj_ssd(x2, nw, w_b, cw, cb, bsz, seq):
    tm = PROJ_ROWS_B
    ns = seq // tm
    row = lambda w: pl.BlockSpec((tm, w), lambda b, s: (b * ns + s, 0))
    widths = (SSD_INNER, SSD_CONV_DIM, LANES)
    return pl.pallas_call(
        _proj_ssd_kernel,
        grid=(bsz, ns),
        in_specs=[row(D_MODEL), _resident((1, D_MODEL)), _resident((D_MODEL, _B_COLS)),
                  _resident((SSD_CONV, SSD_CONV_DIM)), _resident((1, SSD_CONV_DIM))],
        out_specs=[row(w) for w in widths],
        out_shape=[jax.ShapeDtypeStruct((bsz * seq, w), F32) for w in widths],
        scratch_shapes=[pltpu.VMEM((tm + SUBLANES, SSD_CONV_DIM), F32)],
        compiler_params=_params("parallel", "arbitrary"),
        name="proj_ssd",
    )(x2, nw, w_b, cw, cb)


def _sortable(x):
    return x ^ ((x >> 31) & jnp.int32(0x7FFFFFFF))


def _reduce_rows(x, op):
    for sh in (4, 2, 1):
        x = op(x, pltpu.roll(x, sh, 0))
    return x[0:1, :]


def _dsa_kernel(q_ref, qi_ref, wq_ref, k_ref, vt_ref, kib_ref, o_ref, key_ref, sel_ref, *, topk, seq):
    i = pl.program_id(1)
    n_idx = (i + IDX_TILE // Q_BLOCK) // (IDX_TILE // Q_BLOCK)
    n_att = (i + ATT_TILE // Q_BLOCK) // (ATT_TILE // Q_BLOCK)
    kpos = lax.broadcasted_iota(jnp.int32, (IDX_TILE, Q_BLOCK), 0)
    qpos = lax.broadcasted_iota(jnp.int32, (IDX_TILE, Q_BLOCK), 1) + i * Q_BLOCK
    w_t = wq_ref[...].T
    qi_all = jnp.concatenate([qi_ref[:, h * LANES:(h + 1) * LANES] for h in range(IDX_HEADS)], axis=0)
    kf = float(topk)
    int_max = jnp.int32(2 ** 31 - 1)

    def tile(j):
        return pl.ds(pl.multiple_of(j * IDX_TILE, IDX_TILE), IDX_TILE)

    def idx_body(j, carry):
        kmax, kmin = carry
        s = lax.dot_general(kib_ref[tile(j), :], qi_all, _NT, preferred_element_type=F32)
        acc = jnp.zeros((IDX_TILE, Q_BLOCK), F32)
        for h in range(IDX_HEADS):
            acc = acc + w_t[IDX_DIM + h:IDX_DIM + h + 1, :] * jnp.maximum(s[:, h * Q_BLOCK:(h + 1) * Q_BLOCK], 0.0)
        key = _sortable(pltpu.bitcast(acc, jnp.int32))
        causal = kpos + j * IDX_TILE <= qpos
        low = jnp.where(causal, key, jnp.int32(INT_MIN))
        key_ref[tile(j), :] = low
        kmax = jnp.maximum(kmax, _fold_rows(low, SUBLANES, jnp.maximum))
        kmin = jnp.minimum(kmin, _fold_rows(jnp.where(causal, key, int_max), SUBLANES, jnp.minimum))
        return kmax, kmin

    kmax, kmin = lax.fori_loop(0, n_idx, idx_body, (jnp.full((SUBLANES, Q_BLOCK), INT_MIN, jnp.int32),
                                                    jnp.full((SUBLANES, Q_BLOCK), 2 ** 31 - 1, jnp.int32)))
    kmax = _reduce_rows(kmax, jnp.maximum)
    kmin = _reduce_rows(kmin, jnp.minimum)

    def count(pred):
        def body(j, cnt):
            m = jnp.where(pred(key_ref[tile(j), :], kpos + j * IDX_TILE), 1.0, 0.0)
            return cnt + _fold_rows(m, SUBLANES)
        cnt = lax.fori_loop(0, n_idx, body, jnp.zeros((SUBLANES, Q_BLOCK), F32))
        return jnp.sum(cnt, axis=0, keepdims=True)

    n_valid = (qpos[0:1, :] + 1).astype(F32)
    few = n_valid <= kf
    flat = kmin == kmax
    start = few | flat
    thr0 = jnp.where(few, jnp.int32(INT_MIN + 1), kmin)
    state = (jnp.where(start, thr0, kmin), jnp.where(start, thr0, kmax + 1), n_valid)

    def search_pass(it, st):
        lo, hi, c_lo = st
        f_lo = pltpu.bitcast(_sortable(lo), F32)
        f_hi = pltpu.bitcast(_sortable(hi - 1), F32)
        piv = _sortable(pltpu.bitcast(0.5 * f_lo + 0.5 * f_hi, jnp.int32))
        piv = jnp.where(it % 4 == 3, lo + lax.shift_right_logical(hi - lo, 1), piv)
        piv = jnp.where(it == 0, jnp.int32(1), piv)
        piv = jnp.where(it == 1, jnp.where(hi == 1, jnp.int32(-1), piv), piv)
        piv = jnp.minimum(jnp.maximum(piv, lo + 1), hi - 1)
        c = count(lambda kt, pos: kt >= piv)
        live = lo != hi
        ge = c >= kf
        hit = c == kf
        lo_n = jnp.where(live & ge, piv, lo)
        c_lo_n = jnp.where(live & ge, c, c_lo)
        hi_n = jnp.where(live & jnp.logical_not(ge), piv, hi)
        hi_n = jnp.where(live & (hit | (hi_n - lo_n == 1)), lo_n, hi_n)
        return lo_n, hi_n, c_lo_n

    state = lax.fori_loop(0, SEARCH_PASSES, search_pass, state)

    def pending(st):
        return jnp.max(jnp.where(st[0] != st[1], 1.0, 0.0))

    def more_cond(carry):
        return jnp.logical_and(carry[1] > 0.0, carry[0] < 1024)

    def more_body(carry):
        st = search_pass(carry[0] + 1, search_pass(carry[0], carry[2:]))
        return (carry[0] + 2, pending(st)) + st

    thr, _, c_ge = lax.while_loop(more_cond, more_body, (jnp.int32(SEARCH_PASSES), pending(state)) + state)[2:]
    tied = jnp.max(c_ge) > kf

    def write_sel(pred):
        def body(j, carry):
            sel = pred(key_ref[tile(j), :], kpos + j * IDX_TILE)
            sel_ref[tile(j), :] = jnp.where(sel, 1.0, 0.0).astype(BF16)
            return carry
        lax.fori_loop(0, n_idx, body, 0)

    @pl.when(jnp.logical_not(tied))
    def _():
        write_sel(lambda kt, pos: kt >= thr)

    @pl.when(tied)
    def _():
        need = kf - count(lambda kt, pos: kt > thr)
        nbits = max(1, int(np.ceil(np.log2(seq))))

        def pos_body(p, lim):
            cand = lim + lax.shift_left(jnp.int32(1), nbits - 1 - p)
            c = count(lambda kt, pos: jnp.where(kt == thr, pos, jnp.int32(seq)) < cand)
            return jnp.where(c < need, cand, lim)

        lim = lax.fori_loop(0, nbits, pos_body, jnp.zeros((1, Q_BLOCK), jnp.int32))
        lim = jnp.where(c_ge > kf, lim, jnp.int32(seq))
        write_sel(lambda kt, pos: (kt > thr) | ((kt == thr) & (pos <= lim)))

    rows = Q_PER_KV * Q_BLOCK
    pk = 2 * SUBLANES
    qgs = [jnp.concatenate([q_ref[:, (Q_PER_KV * g + e) * LANES:(Q_PER_KV * g + e + 1) * LANES]
                            for e in range(Q_PER_KV)], axis=0) for g in range(N_KV_HEADS)]
    ones_rows = jnp.ones((pk, ATT_TILE), BF16)

    def logits(j, g):
        ks = pl.ds(pl.multiple_of(j * ATT_TILE, ATT_TILE), ATT_TILE)
        return lax.dot_general(k_ref[ks, g * LANES:(g + 1) * LANES], qgs[g], _NT, preferred_element_type=F32)

    def weighted_v(j, g, p):
        ks = pl.ds(pl.multiple_of(j * ATT_TILE, ATT_TILE), ATT_TILE)
        sel = jnp.concatenate([sel_ref[ks, :]] * Q_PER_KV, axis=1)
        v_aug = jnp.concatenate([vt_ref[g * LANES:(g + 1) * LANES, ks], ones_rows], axis=0)
        return _dot(v_aug, p.astype(BF16) * sel)

    shifts = [jnp.max(logits(0, g), axis=0, keepdims=True) for g in range(N_KV_HEADS)]

    def fast_body(j, accs):
        return tuple(accs[g] + weighted_v(j, g, jnp.exp2(logits(j, g) - shifts[g])) for g in range(N_KV_HEADS))

    accs = lax.fori_loop(0, n_att, fast_body, (jnp.zeros((LANES + pk, rows), F32),) * N_KV_HEADS)
    lo_ok, hi_ok = 2.0 ** -60, 2.0 ** 60
    worst = [jnp.max(jnp.where((a[LANES:LANES + 1] >= lo_ok) & (a[LANES:LANES + 1] <= hi_ok), 0.0, 1.0))
             + jnp.max(jnp.where(jnp.abs(a[0:LANES]) < jnp.inf, 0.0, 1.0)) for a in accs]
    redo = (worst[0] + worst[1]) > 0.0

    def store(g, acc):
        o_t = acc[0:LANES] / acc[LANES:LANES + 1]
        for e in range(Q_PER_KV):
            col = (Q_PER_KV * g + e) * LANES
            o_ref[:, col:col + LANES] = o_t[:, e * Q_BLOCK:(e + 1) * Q_BLOCK].T.astype(BF16)

    @pl.when(jnp.logical_not(redo))
    def _():
        for g in range(N_KV_HEADS):
            store(g, accs[g])

    @pl.when(redo)
    def _():
        def safe_body(j, carry):
            ks = pl.ds(pl.multiple_of(j * ATT_TILE, ATT_TILE), ATT_TILE)
            sel = jnp.concatenate([sel_ref[ks, :]] * Q_PER_KV, axis=1).astype(F32)
            out = []
            for g in range(N_KV_HEADS):
                m, acc = carry[g]
                s = logits(j, g) + (sel - 1.0) * (-MASKED)
                m_new = jnp.maximum(m, jnp.max(s, axis=0, keepdims=True))
                out.append((m_new, jnp.exp2(m - m_new) * acc + weighted_v(j, g, jnp.exp2(s - m_new))))
            return tuple(out)

        init = (jnp.full((1, rows), MASKED, F32), jnp.zeros((LANES + pk, rows), F32))
        res = lax.fori_loop(0, n_att, safe_body, (init,) * N_KV_HEADS)
        for g in range(N_KV_HEADS):
            store(g, res[g][1])


def _dsa(q, k, vt, qi, kiwf, kiwb, bsz, seq):
    assert seq % IDX_TILE == 0 and seq % ATT_TILE == 0
    nq = seq // Q_BLOCK
    topk = min(TOPK_MAX, seq // 4)
    qrow = lambda w: pl.BlockSpec((Q_BLOCK, w), lambda b, i: (b * nq + i, 0))
    full = lambda w: pl.BlockSpec((seq, w), lambda b, i: (b, 0))
    return pl.pallas_call(
        functools.partial(_dsa_kernel, topk=topk, seq=seq),
        grid=(bsz, nq),
        in_specs=[qrow(ATTN_WIDTH), qrow(IDX_HEADS * LANES), qrow(LANES), full(KV_WIDTH),
                  pl.BlockSpec((KV_WIDTH, seq), lambda b, i: (0, b)), full(LANES)],
        out_specs=qrow(ATTN_WIDTH),
        out_shape=jax.ShapeDtypeStruct((bsz * seq, ATTN_WIDTH), BF16),
        scratch_shapes=[pltpu.VMEM((seq, Q_BLOCK), jnp.int32), pltpu.VMEM((seq, Q_BLOCK), BF16)],
        compiler_params=_params("parallel", "arbitrary"),
        name="dsa_attention",
    )(q, qi, kiwf, k, vt, kiwb)


def _split3(x):
    hi = x.astype(BF16)
    r = x - hi.astype(F32)
    mid = r.astype(BF16)
    return hi, mid, (r - mid.astype(F32)).astype(BF16)


def _ssd_kernel(zs_ref, xa_ref, dt_ref, dtb_ref, alog_ref, dexp_ref, nw_ref, e_ref, o_ref, st_ref):
    L = SSD_CHUNK

    @pl.when(pl.program_id(1) == 0)
    def _():
        st_ref[...] = jnp.zeros_like(st_ref)

    dtv = dt_ref[...] + dtb_ref[...]
    dt = jnp.maximum(dtv, 0.0) + jnp.log1p(jnp.exp(-jnp.abs(dtv)))
    adt = dt * (-jnp.exp(alog_ref[...]))
    ri = lax.broadcasted_iota(jnp.int32, (L, L), 0)
    ci = lax.broadcasted_iota(jnp.int32, (L, L), 1)
    tril = ci <= ri
    tril_b = jnp.where(tril, 1.0, 0.0).astype(BF16)
    acs = sum(_dot(tril_b, t) for t in _split3(adt))
    last = acs[L - 1:L, :]
    acs_t = acs.T
    dt_t = dt.T
    stack = _split3(jnp.concatenate([dt * jnp.exp(last - acs), jnp.exp(acs)], axis=0))
    first_half = ci < SSD_HEAD_DIM

    gw = SSD_GROUP_WIDTH
    for g in range(SSD_GROUPS):
        gs = slice(g * gw, (g + 1) * gw)
        b_g = xa_ref[:, SSD_INNER + g * SSD_STATE:SSD_INNER + (g + 1) * SSD_STATE]
        c_g = xa_ref[:, SSD_INNER + SSD_GN + g * SSD_STATE:SSD_INNER + SSD_GN + (g + 1) * SSD_STATE]
        b_b, c_b = b_g.astype(BF16), c_g.astype(BF16)
        cb = lax.dot_general(c_b, b_b, _NT, preferred_element_type=F32)
        b_t = b_g.T.astype(BF16)
        ex = sum(_dot(t, e_ref[:, gs]) for t in stack)
        dtdec, eacs = ex[0:L], ex[L:2 * L]
        xs = xa_ref[:, gs]
        s_in = st_ref[:, gs]
        y = _dot(c_b, s_in.astype(BF16)) * eacs + xs * dexp_ref[:, gs]
        st_ref[:, gs] = s_in * eacs[L - 1:L, :] + _dot(b_t, (xs * dtdec).astype(BF16))
        ys = []
        for c2 in range(gw // LANES):
            xp = xs[:, c2 * LANES:(c2 + 1) * LANES].astype(BF16)
            pair = []
            for hh in range(LANES // SSD_HEAD_DIM):
                h = g * SSD_HEADS_PER_GROUP + c2 * (LANES // SSD_HEAD_DIM) + hh
                seg = acs[:, h:h + 1] - acs_t[h:h + 1, :]
                lm = jnp.exp(jnp.where(tril, seg, -jnp.inf))
                pair.append(_dot((cb * lm * dt_t[h:h + 1, :]).astype(BF16), xp))
            ys.append(jnp.where(first_half, pair[0], pair[1]))
        y = y + jnp.concatenate(ys, axis=1)
        yz = y * zs_ref[:, gs]
        ms = jnp.mean(yz * yz, axis=1, keepdims=True)
        o_ref[:, gs] = (yz * lax.rsqrt(ms + NORM_EPS) * nw_ref[:, gs]).astype(BF16)


def _ssd(zs, xa, dtp, dtb, alog, dexp, nw, emat, bsz, seq):
    L = SSD_CHUNK
    nc = seq // L
    row = lambda w: pl.BlockSpec((L, w), lambda b, c: (b * nc + c, 0))
    return pl.pallas_call(
        _ssd_kernel,
        grid=(bsz, nc),
        in_specs=[row(SSD_INNER), row(SSD_CONV_DIM), row(LANES),
                  _resident((1, LANES)), _resident((1, LANES)), _resident((1, SSD_INNER)),
                  _resident((1, SSD_INNER)), _resident((LANES, SSD_INNER))],
        out_specs=row(SSD_INNER),
        out_shape=jax.ShapeDtypeStruct((bsz * seq, SSD_INNER), BF16),
        scratch_shapes=[pltpu.VMEM((SSD_STATE, SSD_INNER), F32)],
        compiler_params=_params("parallel", "arbitrary"),
        name="ssd_branch",
    )(zs, xa, dtp, dtb, alog, dexp, nw, emat)


def _merge_kernel(x_ref, a_ref, b_ref, nw_ref, wg_ref, wpa_ref, wps_ref, wo_ref, o_ref):
    x = x_ref[...]
    h = _rms(x, nw_ref[...]).astype(BF16)
    merged = _sigmoid(_dot(h, wg_ref[:, 0:D_MODEL])) * _dot(a_ref[...], wpa_ref[...])
    merged = merged + _sigmoid(_dot(h, wg_ref[:, D_MODEL:2 * D_MODEL])) * _dot(b_ref[...], wps_ref[...])
    o_ref[...] = x + _dot(merged.astype(BF16), wo_ref[...])


def _merge(x2, a, b, nw, wg, wpa, wps, wo):
    tokens = x2.shape[0]
    tm = MERGE_ROWS
    row = lambda w: pl.BlockSpec((tm, w), lambda i: (i, 0))
    return pl.pallas_call(
        _merge_kernel,
        grid=(tokens // tm,),
        in_specs=[row(D_MODEL), row(ATTN_WIDTH), row(SSD_INNER), _resident((1, D_MODEL)),
                  _resident((D_MODEL, 2 * D_MODEL)), _resident((ATTN_WIDTH, D_MODEL)),
                  _resident((SSD_INNER, D_MODEL)), _resident((D_MODEL, D_MODEL))],
        out_specs=row(D_MODEL),
        out_shape=jax.ShapeDtypeStruct((tokens, D_MODEL), F32),
        compiler_params=_params("parallel"),
        name="merge_out",
    )(x2, a, b, nw, wg, wpa, wps, wo)


def _ffn_kernel(x_ref, nw_ref, wup_ref, cw_ref, cb_ref, wdn_ref, fnw_ref, o_ref, ext_ref, carry_ref, *, final_norm):
    tm = FFN_ROWS
    tail = SUBLANES
    nf = FFN_DIM // FFN_TILE

    @pl.when(pl.program_id(1) == 0)
    def _():
        carry_ref[...] = jnp.zeros_like(carry_ref)

    x = x_ref[...]
    h = _rms(x, nw_ref[...]).astype(BF16)
    acc = x
    for f in range(nf):
        parts = []
        for part in range(2):
            col = part * FFN_DIM + f * FFN_TILE
            cs = slice(col, col + FFN_TILE)
            ct = part * nf + f
            u = _dot(h, wup_ref[:, cs])
            ext_ref[part, 0:tail, :] = carry_ref[ct]
            ext_ref[part, tail:tail + tm, :] = u
            cv = cb_ref[:, cs] + cw_ref[2:3, cs] * u
            for j in range(FFN_CONV - 1):
                r0 = tail - (FFN_CONV - 1) + j
                cv = cv + cw_ref[j:j + 1, cs] * ext_ref[part, r0:r0 + tm, :]
            carry_ref[ct] = u[tm - tail:tm, :]
            parts.append(cv)
        act = (parts[0] * _sigmoid(parts[0]) * parts[1]).astype(BF16)
        acc = acc + _dot(act, wdn_ref[f * FFN_TILE:(f + 1) * FFN_TILE, :])
    if final_norm:
        acc = _rms(acc, fnw_ref[...])
    o_ref[...] = acc


def _ffn(x2, nw, wup, cw, cb, wdn, fnw, bsz, seq, final_norm):
    tm = FFN_ROWS
    ns = seq // tm
    nf = FFN_DIM // FFN_TILE
    row = pl.BlockSpec((tm, D_MODEL), lambda b, s: (b * ns + s, 0))
    return pl.pallas_call(
        functools.partial(_ffn_kernel, final_norm=final_norm),
        grid=(bsz, ns),
        in_specs=[row, _resident((1, D_MODEL)), _resident((D_MODEL, 2 * FFN_DIM)),
                  _resident((FFN_CONV, 2 * FFN_DIM)), _resident((1, 2 * FFN_DIM)),
                  _resident((FFN_DIM, D_MODEL)), _resident((1, D_MODEL))],
        out_specs=row,
        out_shape=jax.ShapeDtypeStruct((bsz * seq, D_MODEL), F32),
        scratch_shapes=[pltpu.VMEM((2, tm + SUBLANES, FFN_TILE), F32),
                        pltpu.VMEM((2 * nf, SUBLANES, FFN_TILE), F32)],
        compiler_params=_params("parallel", "arbitrary"),
        name="conv_glu_ffn",
    )(x2, nw, wup, cw, cb, wdn, fnw)


def _split_w_in(w_in):
    o = np.cumsum([0, ATTN_WIDTH, KV_WIDTH, KV_WIDTH, IDX_HEADS * IDX_DIM, IDX_DIM, IDX_HEADS,
                   SSD_INNER, SSD_CONV_DIM, SSD_HEADS, D_MODEL, D_MODEL]).tolist()
    seg = lambda j: w_in[:, o[j]:o[j + 1]]
    d = w_in.shape[0]
    qi = seg(3).reshape(d, IDX_HEADS, IDX_DIM)
    qi = jnp.pad(qi, ((0, 0), (0, 0), (0, LANES - IDX_DIM))).reshape(d, IDX_HEADS * LANES)
    kiw = jnp.pad(jnp.concatenate([seg(4), seg(5)], axis=1), ((0, 0), (0, LANES - IDX_DIM - IDX_HEADS)))
    w_a = jnp.concatenate([seg(0), seg(1), qi, kiw], axis=1).astype(BF16)
    w_vt = seg(2).T.astype(BF16)
    dt = jnp.pad(seg(8), ((0, 0), (0, LANES - SSD_HEADS)))
    w_b = jnp.concatenate([seg(6), seg(7), dt], axis=1).astype(BF16)
    w_g = jnp.concatenate([seg(9), seg(10)], axis=1).astype(BF16)
    return w_a, w_vt, w_b, w_g


def _lane_pad_row(v):
    return jnp.pad(v, (0, LANES - v.shape[0]))[None, :]


def kernel(x, norm_mix_w, w_in, ssd_conv_w, ssd_conv_b, ssd_dt_bias, ssd_a_log, ssd_d, ssd_norm_w, w_proj_attn, w_proj_ssd, w_out, norm_ffn_w, ffn_w_up, ffn_conv_w, ffn_conv_b, ffn_w_down, norm_final_w):
    bsz, seq, d = x.shape
    depth = w_in.shape[0]
    assert d == D_MODEL and seq % max(PROJ_ROWS_A, FFN_ROWS, MERGE_ROWS, IDX_TILE) == 0
    tabs_a = _rope_tables(seq, HEAD_DIM // ROPE_FRACTION)
    tabs_i = _rope_tables(seq, IDX_DIM // ROPE_FRACTION)
    emat = (jnp.arange(SSD_INNER)[None, :] // SSD_HEAD_DIM == jnp.arange(LANES)[:, None]).astype(BF16)
    x2 = x.reshape(bsz * seq, d)
    for l in range(depth):
        w_a, w_vt, w_b, w_g = _split_w_in(w_in[l])
        nw = norm_mix_w[l][None, :]
        q, k, vt, qi, kiwf, kiwb = _proj_attn(x2, nw, w_a, w_vt, tabs_a, tabs_i, seq)
        a = _dsa(q, k, vt, qi, kiwf, kiwb, bsz, seq)
        zs, xa, dtp = _proj_ssd(x2, nw, w_b, ssd_conv_w[l], ssd_conv_b[l][None, :], bsz, seq)
        b = _ssd(zs, xa, dtp, _lane_pad_row(ssd_dt_bias[l]), _lane_pad_row(ssd_a_log[l]),
                 jnp.repeat(ssd_d[l], SSD_HEAD_DIM)[None, :], ssd_norm_w[l][None, :], emat, bsz, seq)
        x2 = _merge(x2, a, b, nw, w_g, w_proj_attn[l].astype(BF16), w_proj_ssd[l].astype(BF16),
                    w_out[l].astype(BF16))
        x2 = _ffn(x2, norm_ffn_w[l][None, :], ffn_w_up[l].astype(BF16), ffn_conv_w[l], ffn_conv_b[l][None, :],
                  ffn_w_down[l].astype(BF16), norm_final_w[None, :], bsz, seq, final_norm=(l == depth - 1))
    return x2.reshape(bsz, seq, d)
```

```python
import functools

import numpy as np
import jax
import jax.numpy as jnp
from jax import lax
from jax.experimental import pallas as pl
from jax.experimental.pallas import tpu as pltpu

D_MODEL = 1024
N_HEADS = 8
HEAD_DIM = 128
N_KV_HEADS = 2
Q_PER_KV = N_HEADS // N_KV_HEADS
ATTN_WIDTH = N_HEADS * HEAD_DIM
KV_WIDTH = N_KV_HEADS * HEAD_DIM
IDX_HEADS = 8
IDX_DIM = 64
TOPK_MAX = 256
ROPE_THETA = 500000.0
ROPE_FRACTION = 4
SSD_INNER = 2 * D_MODEL
SSD_HEAD_DIM = 64
SSD_HEADS = SSD_INNER // SSD_HEAD_DIM
SSD_GROUPS = 4
SSD_HEADS_PER_GROUP = SSD_HEADS // SSD_GROUPS
SSD_STATE = 128
SSD_CONV = 4
SSD_CHUNK = 128
SSD_GN = SSD_GROUPS * SSD_STATE
SSD_CONV_DIM = SSD_INNER + 2 * SSD_GN
SSD_GROUP_WIDTH = SSD_INNER // SSD_GROUPS
FFN_DIM = 2816
FFN_CONV = 3
NORM_EPS = 1e-6

LANES = 128
SUBLANES = 8
VMEM_LIMIT = 56 * 1024 * 1024
Q_BLOCK = 256
IDX_TILE = 512
ATT_TILE = 512
SEARCH_PASSES = 18
PROJ_ROWS_A = 512
PROJ_ROWS_B = 256
MERGE_ROWS = 512
FFN_ROWS = 512
FFN_TILE = 256
MASKED = -1e30
LOG2_E = 1.4426950408889634
INT_MIN = -(2 ** 31)

F32 = jnp.float32
BF16 = jnp.bfloat16
_NT = (((1,), (1,)), ((), ()))
_HI = lax.Precision.HIGHEST


def _params(*sem):
    return pltpu.CompilerParams(dimension_semantics=sem, vmem_limit_bytes=VMEM_LIMIT)


def _resident(shape):
    nd = len(shape)
    return pl.BlockSpec(shape, lambda *_: (0,) * nd, pipeline_mode=pl.Buffered(1))


def _rms(x, w):
    return x * lax.rsqrt(jnp.mean(x * x, axis=-1, keepdims=True) + NORM_EPS) * w


def _sigmoid(x):
    return 1.0 / (1.0 + jnp.exp(-x))


def _dot(a, b):
    return jnp.dot(a, b, preferred_element_type=F32)


def _fold_rows(x, rows, op=jnp.add):
    parts = [x[r:r + rows] for r in range(0, x.shape[0], rows)]
    while len(parts) > 1:
        parts = [op(parts[a], parts[a + 1]) for a in range(0, len(parts) - 1, 2)] + parts[len(parts) & ~1:]
    return parts[0]


def _rope_tables(seq, rot_dim):
    half = rot_dim // 2
    inv = ROPE_THETA ** (-jnp.arange(0, rot_dim, 2, dtype=F32) / rot_dim)
    ang = jnp.arange(seq, dtype=F32)[:, None] * inv[None, :]
    cos, sin = jnp.cos(ang), jnp.sin(ang)
    pad = LANES - rot_dim
    c = jnp.concatenate([cos, cos, jnp.ones((seq, pad), F32)], axis=1)
    s1 = jnp.concatenate([-sin, jnp.zeros((seq, half + pad), F32)], axis=1)
    s2 = jnp.concatenate([jnp.zeros((seq, half), F32), sin, jnp.zeros((seq, pad), F32)], axis=1)
    return c, s1, s2


def _rope(t, c, s1, s2, half):
    return t * c + pltpu.roll(t, LANES - half, 1) * s1 + pltpu.roll(t, half, 1) * s2


_A_Q = 0
_A_K = _A_Q + ATTN_WIDTH
_A_QI = _A_K + KV_WIDTH
_A_KIW = _A_QI + IDX_HEADS * LANES
_A_COLS = _A_KIW + LANES


def _proj_attn_kernel(x_ref, nw_ref, w_ref, wvt_ref, ca_ref, sa1_ref, sa2_ref, ci_ref, si1_ref, si2_ref,
                      q_ref, k_ref, vt_ref, qi_ref, kiwf_ref, kiwb_ref):
    h = _rms(x_ref[...], nw_ref[...]).astype(BF16)
    ca, sa1, sa2 = ca_ref[...], sa1_ref[...], sa2_ref[...]
    ci, si1, si2 = ci_ref[...], si1_ref[...], si2_ref[...]
    ha = HEAD_DIM // ROPE_FRACTION // 2
    hi = IDX_DIM // ROPE_FRACTION // 2
    scale = HEAD_DIM ** -0.5 * LOG2_E
    for j in range(N_HEADS):
        t = _dot(h, w_ref[:, _A_Q + j * LANES:_A_Q + (j + 1) * LANES])
        q_ref[:, j * LANES:(j + 1) * LANES] = (_rope(t, ca, sa1, sa2, ha) * scale).astype(BF16)
    for j in range(N_KV_HEADS):
        t = _dot(h, w_ref[:, _A_K + j * LANES:_A_K + (j + 1) * LANES])
        k_ref[:, j * LANES:(j + 1) * LANES] = _rope(t, ca, sa1, sa2, ha).astype(BF16)
    vt_ref[...] = lax.dot_general(wvt_ref[...], h, _NT, preferred_element_type=F32).astype(BF16)
    for j in range(IDX_HEADS):
        t = _dot(h, w_ref[:, _A_QI + j * LANES:_A_QI + (j + 1) * LANES])
        qi_ref[:, j * LANES:(j + 1) * LANES] = _rope(t, ci, si1, si2, hi).astype(BF16)
    t = _rope(_dot(h, w_ref[:, _A_KIW:_A_KIW + LANES]), ci, si1, si2, hi)
    kiwf_ref[...] = t
    kiwb_ref[...] = t.astype(BF16)


def _proj_attn(x2, nw, w_a, w_vt, tabs_a, tabs_i, seq):
    tokens = x2.shape[0]
    tm = PROJ_ROWS_A
    per_seq = seq // tm
    row = lambda w: pl.BlockSpec((tm, w), lambda i: (i, 0))
    tab = pl.BlockSpec((tm, LANES), lambda i: (i % per_seq, 0))
    sds = jax.ShapeDtypeStruct
    return pl.pallas_call(
        _proj_attn_kernel,
        grid=(tokens // tm,),
        in_specs=[row(D_MODEL), _resident((1, D_MODEL)), _resident((D_MODEL, _A_COLS)),
                  _resident((KV_WIDTH, D_MODEL))] + [tab] * 6,
        out_specs=[row(ATTN_WIDTH), row(KV_WIDTH), pl.BlockSpec((KV_WIDTH, tm), lambda i: (0, i)),
                   row(IDX_HEADS * LANES), row(LANES), row(LANES)],
        out_shape=[sds((tokens, ATTN_WIDTH), BF16), sds((tokens, KV_WIDTH), BF16), sds((KV_WIDTH, tokens), BF16),
                   sds((tokens, IDX_HEADS * LANES), BF16), sds((tokens, LANES), F32), sds((tokens, LANES), BF16)],
        compiler_params=_params("parallel"),
        name="proj_attn",
    )(x2, nw, w_a, w_vt, *tabs_a, *tabs_i)


_B_Z = 0
_B_XBC = _B_Z + SSD_INNER
_B_DT = _B_XBC + SSD_CONV_DIM
_B_COLS = _B_DT + LANES


def _proj_ssd_kernel(x_ref, nw_ref, w_ref, cw_ref, cb_ref, zs_ref, xa_ref, dt_ref, ext_ref):
    tm = PROJ_ROWS_B
    tail = SUBLANES

    @pl.when(pl.program_id(1) == 0)
    def _():
        ext_ref[0:tail, :] = jnp.zeros((tail, SSD_CONV_DIM), F32)

    h = _rms(x_ref[...], nw_ref[...]).astype(BF16)
    z = _dot(h, w_ref[:, _B_Z:_B_Z + SSD_INNER])
    zs_ref[...] = z * _sigmoid(z)
    dt_ref[...] = _dot(h, w_ref[:, _B_DT:_B_DT + LANES])
    cstep = 512
    for c0 in range(0, SSD_CONV_DIM, cstep):
        cs = slice(c0, c0 + cstep)
        ext_ref[tail:tail + tm, cs] = _dot(h, w_ref[:, _B_XBC + c0:_B_XBC + c0 + cstep])
        ue = ext_ref[:, cs]
        u = cb_ref[:, cs] + cw_ref[SSD_CONV - 1:SSD_CONV, cs] * ue[tail:tail + tm]
        for j in range(SSD_CONV - 1):
            back = SSD_CONV - 1 - j
            u = u + cw_ref[j:j + 1, cs] * pltpu.roll(ue, back, 0)[tail:tail + tm]
        xa_ref[:, cs] = u * _sigmoid(u)
        ext_ref[0:tail, cs] = ue[tm:tm + tail]


def _proj_ssd(x2, nw, w_b, cw, cb, bsz, seq):
    tm = PROJ_ROWS_B
    ns = seq // tm
    row = lambda w: pl.BlockSpec((tm, w), lambda b, s: (b * ns + s, 0))
    widths = (SSD_INNER, SSD_CONV_DIM, LANES)
    return pl.pallas_call(
        _proj_ssd_kernel,
        grid=(bsz, ns),
        in_specs=[row(D_MODEL), _resident((1, D_MODEL)), _resident((D_MODEL, _B_COLS)),
                  _resident((SSD_CONV, SSD_CONV_DIM)), _resident((1, SSD_CONV_DIM))],
        out_specs=[row(w) for w in widths],
        out_shape=[jax.ShapeDtypeStruct((bsz * seq, w), F32) for w in widths],
        scratch_shapes=[pltpu.VMEM((tm + SUBLANES, SSD_CONV_DIM), F32)],
        compiler_params=_params("parallel", "arbitrary"),
        name="proj_ssd",
    )(x2, nw, w_b, cw, cb)


def _sortable(x):
    return x ^ ((x >> 31) & jnp.int32(0x7FFFFFFF))


def _reduce_rows(x, op):
    for sh in (4, 2, 1):
        x = op(x, pltpu.roll(x, sh, 0))
    return x[0:1, :]


def _dsa_kernel(q_ref, qi_ref, wq_ref, k_ref, vt_ref, kib_ref, o_ref, key_ref, sel_ref, *, topk, seq):
    i = pl.program_id(1)
    n_idx = (i + IDX_TILE // Q_BLOCK) // (IDX_TILE // Q_BLOCK)
    n_att = (i + ATT_TILE // Q_BLOCK) // (ATT_TILE // Q_BLOCK)
    kpos = lax.broadcasted_iota(jnp.int32, (IDX_TILE, Q_BLOCK), 0)
    qpos = lax.broadcasted_iota(jnp.int32, (IDX_TILE, Q_BLOCK), 1) + i * Q_BLOCK
    w_t = wq_ref[...].T
    qi_all = jnp.concatenate([qi_ref[:, h * LANES:(h + 1) * LANES] for h in range(IDX_HEADS)], axis=0)
    kf = float(topk)
    int_max = jnp.int32(2 ** 31 - 1)

    def tile(j):
        return pl.ds(pl.multiple_of(j * IDX_TILE, IDX_TILE), IDX_TILE)

    def idx_body(j, carry):
        kmax, kmin = carry
        s = lax.dot_general(kib_ref[tile(j), :], qi_all, _NT, preferred_element_type=F32)
        acc = jnp.zeros((IDX_TILE, Q_BLOCK), F32)
        for h in range(IDX_HEADS):
            acc = acc + w_t[IDX_DIM + h:IDX_DIM + h + 1, :] * jnp.maximum(s[:, h * Q_BLOCK:(h + 1) * Q_BLOCK], 0.0)
        key = _sortable(pltpu.bitcast(acc, jnp.int32))
        causal = kpos + j * IDX_TILE <= qpos
        low = jnp.where(causal, key, jnp.int32(INT_MIN))
        key_ref[tile(j), :] = low
        kmax = jnp.maximum(kmax, _fold_rows(low, SUBLANES, jnp.maximum))
        kmin = jnp.minimum(kmin, _fold_rows(jnp.where(causal, key, int_max), SUBLANES, jnp.minimum))
        return kmax, kmin

    kmax, kmin = lax.fori_loop(0, n_idx, idx_body, (jnp.full((SUBLANES, Q_BLOCK), INT_MIN, jnp.int32),
                                                    jnp.full((SUBLANES, Q_BLOCK), 2 ** 31 - 1, jnp.int32)))
    kmax = _reduce_rows(kmax, jnp.maximum)
    kmin = _reduce_rows(kmin, jnp.minimum)

    def count(pred):
        def body(j, cnt):
            m = jnp.where(pred(key_ref[tile(j), :], kpos + j * IDX_TILE), 1.0, 0.0)
            return cnt + _fold_rows(m, SUBLANES)
        cnt = lax.fori_loop(0, n_idx, body, jnp.zeros((SUBLANES, Q_BLOCK), F32))
        return jnp.sum(cnt, axis=0, keepdims=True)

    n_valid = (qpos[0:1, :] + 1).astype(F32)
    few = n_valid <= kf
    flat = kmin == kmax
    start = few | flat
    thr0 = jnp.where(few, jnp.int32(INT_MIN + 1), kmin)
    state = (jnp.where(start, thr0, kmin), jnp.where(start, thr0, kmax + 1), n_valid)

    def search_pass(it, st):
        lo, hi, c_lo = st
        f_lo = pltpu.bitcast(_sortable(lo), F32)
        f_hi = pltpu.bitcast(_sortable(hi - 1), F32)
        piv = _sortable(pltpu.bitcast(0.5 * f_lo + 0.5 * f_hi, jnp.int32))
        piv = jnp.where(it % 4 == 3, lo + lax.shift_right_logical(hi - lo, 1), piv)
        piv = jnp.where(it == 0, jnp.int32(1), piv)
        piv = jnp.where(it == 1, jnp.where(hi == 1, jnp.int32(-1), piv), piv)
        piv = jnp.minimum(jnp.maximum(piv, lo + 1), hi - 1)
        c = count(lambda kt, pos: kt >= piv)
        live = lo != hi
        ge = c >= kf
        hit = c == kf
        lo_n = jnp.where(live & ge, piv, lo)
        c_lo_n = jnp.where(live & ge, c, c_lo)
        hi_n = jnp.where(live & jnp.logical_not(ge), piv, hi)
        hi_n = jnp.where(live & (hit | (hi_n - lo_n == 1)), lo_n, hi_n)
        return lo_n, hi_n, c_lo_n

    state = lax.fori_loop(0, SEARCH_PASSES, search_pass, state)

    def pending(st):
        return jnp.max(jnp.where(st[0] != st[1], 1.0, 0.0))

    def more_cond(carry):
        return jnp.logical_and(carry[1] > 0.0, carry[0] < 1024)

    def more_body(carry):
        st = search_pass(carry[0] + 1, search_pass(carry[0], carry[2:]))
        return (carry[0] + 2, pending(st)) + st

    thr, _, c_ge = lax.while_loop(more_cond, more_body, (jnp.int32(SEARCH_PASSES), pending(state)) + state)[2:]
    tied = jnp.max(c_ge) > kf

    def write_sel(pred):
        def body(j, carry):
            sel = pred(key_ref[tile(j), :], kpos + j * IDX_TILE)
            sel_ref[tile(j), :] = jnp.where(sel, 1.0, 0.0).astype(BF16)
            return carry
        lax.fori_loop(0, n_idx, body, 0)

    @pl.when(jnp.logical_not(tied))
    def _():
        write_sel(lambda kt, pos: kt >= thr)

    @pl.when(tied)
    def _():
        need = kf - count(lambda kt, pos: kt > thr)
        nbits = max(1, int(np.ceil(np.log2(seq))))

        def pos_body(p, lim):
            cand = lim + lax.shift_left(jnp.int32(1), nbits - 1 - p)
            c = count(lambda kt, pos: jnp.where(kt == thr, pos, jnp.int32(seq)) < cand)
            return jnp.where(c < need, cand, lim)

        lim = lax.fori_loop(0, nbits, pos_body, jnp.zeros((1, Q_BLOCK), jnp.int32))
        lim = jnp.where(c_ge > kf, lim, jnp.int32(seq))
        write_sel(lambda kt, pos: (kt > thr) | ((kt == thr) & (pos <= lim)))

    rows = Q_PER_KV * Q_BLOCK
    pk = 2 * SUBLANES
    qgs = [jnp.concatenate([q_ref[:, (Q_PER_KV * g + e) * LANES:(Q_PER_KV * g + e + 1) * LANES]
                            for e in range(Q_PER_KV)], axis=0) for g in range(N_KV_HEADS)]
    ones_rows = jnp.ones((pk, ATT_TILE), BF16)

    def logits(j, g):
        ks = pl.ds(pl.multiple_of(j * ATT_TILE, ATT_TILE), ATT_TILE)
        return lax.dot_general(k_ref[ks, g * LANES:(g + 1) * LANES], qgs[g], _NT, preferred_element_type=F32)

    def weighted_v(j, g, p):
        ks = pl.ds(pl.multiple_of(j * ATT_TILE, ATT_TILE), ATT_TILE)
        sel = jnp.concatenate([sel_ref[ks, :]] * Q_PER_KV, axis=1)
        v_aug = jnp.concatenate([vt_ref[g * LANES:(g + 1) * LANES, ks], ones_rows], axis=0)
        return _dot(v_aug, p.astype(BF16) * sel)

    shifts = [jnp.max(logits(0, g), axis=0, keepdims=True) for g in range(N_KV_HEADS)]

    def fast_body(j, accs):
        return tuple(accs[g] + weighted_v(j, g, jnp.exp2(logits(j, g) - shifts[g])) for g in range(N_KV_HEADS))

    accs = lax.fori_loop(0, n_att, fast_body, (jnp.zeros((LANES + pk, rows), F32),) * N_KV_HEADS)
    lo_ok, hi_ok = 2.0 ** -60, 2.0 ** 60
    worst = [jnp.max(jnp.where((a[LANES:LANES + 1] >= lo_ok) & (a[LANES:LANES + 1] <= hi_ok), 0.0, 1.0))
             + jnp.max(jnp.where(jnp.abs(a[0:LANES]) < jnp.inf, 0.0, 1.0)) for a in accs]
    redo = (worst[0] + worst[1]) > 0.0

    def store(g, acc):
        o_t = acc[0:LANES] / acc[LANES:LANES + 1]
        for e in range(Q_PER_KV):
            col = (Q_PER_KV * g + e) * LANES
            o_ref[:, col:col + LANES] = o_t[:, e * Q_BLOCK:(e + 1) * Q_BLOCK].T.astype(BF16)

    @pl.when(jnp.logical_not(redo))
    def _():
        for g in range(N_KV_HEADS):
            store(g, accs[g])

    @pl.when(redo)
    def _():
        def safe_body(j, carry):
            ks = pl.ds(pl.multiple_of(j * ATT_TILE, ATT_TILE), ATT_TILE)
            sel = jnp.concatenate([sel_ref[ks, :]] * Q_PER_KV, axis=1).astype(F32)
            out = []
            for g in range(N_KV_HEADS):
                m, acc = carry[g]
                s = logits(j, g) + (sel - 1.0) * (-MASKED)
                m_new = jnp.maximum(m, jnp.max(s, axis=0, keepdims=True))
                out.append((m_new, jnp.exp2(m - m_new) * acc + weighted_v(j, g, jnp.exp2(s - m_new))))
            return tuple(out)

        init = (jnp.full((1, rows), MASKED, F32), jnp.zeros((LANES + pk, rows), F32))
        res = lax.fori_loop(0, n_att, safe_body, (init,) * N_KV_HEADS)
        for g in range(N_KV_HEADS):
            store(g, res[g][1])


def _dsa(q, k, vt, qi, kiwf, kiwb, bsz, seq):
    assert seq % IDX_TILE == 0 and seq % ATT_TILE == 0
    nq = seq // Q_BLOCK
    topk = min(TOPK_MAX, seq // 4)
    qrow = lambda w: pl.BlockSpec((Q_BLOCK, w), lambda b, i: (b * nq + i, 0))
    full = lambda w: pl.BlockSpec((seq, w), lambda b, i: (b, 0))
    return pl.pallas_call(
        functools.partial(_dsa_kernel, topk=topk, seq=seq),
        grid=(bsz, nq),
        in_specs=[qrow(ATTN_WIDTH), qrow(IDX_HEADS * LANES), qrow(LANES), full(KV_WIDTH),
                  pl.BlockSpec((KV_WIDTH, seq), lambda b, i: (0, b)), full(LANES)],
        out_specs=qrow(ATTN_WIDTH),
        out_shape=jax.ShapeDtypeStruct((bsz * seq, ATTN_WIDTH), BF16),
        scratch_shapes=[pltpu.VMEM((seq, Q_BLOCK), jnp.int32), pltpu.VMEM((seq, Q_BLOCK), BF16)],
        compiler_params=_params("parallel", "arbitrary"),
        name="dsa_attention",
    )(q, qi, kiwf, k, vt, kiwb)


def _split3(x):
    hi = x.astype(BF16)
    r = x - hi.astype(F32)
    mid = r.astype(BF16)
    return hi, mid, (r - mid.astype(F32)).astype(BF16)


def _ssd_kernel(zs_ref, xa_ref, dt_ref, dtb_ref, alog_ref, dexp_ref, nw_ref, e_ref, o_ref, st_ref):
    L = SSD_CHUNK

    @pl.when(pl.program_id(1) == 0)
    def _():
        st_ref[...] = jnp.zeros_like(st_ref)

    dtv = dt_ref[...] + dtb_ref[...]
    dt = jnp.maximum(dtv, 0.0) + jnp.log1p(jnp.exp(-jnp.abs(dtv)))
    adt = dt * (-jnp.exp(alog_ref[...]))
    ri = lax.broadcasted_iota(jnp.int32, (L, L), 0)
    ci = lax.broadcasted_iota(jnp.int32, (L, L), 1)
    tril = ci <= ri
    tril_b = jnp.where(tril, 1.0, 0.0).astype(BF16)
    acs = sum(_dot(tril_b, t) for t in _split3(adt))
    last = acs[L - 1:L, :]
    acs_t = acs.T
    dt_t = dt.T
    stack = _split3(jnp.concatenate([dt * jnp.exp(last - acs), jnp.exp(acs)], axis=0))
    first_half = ci < SSD_HEAD_DIM

    gw = SSD_GROUP_WIDTH
    for g in range(SSD_GROUPS):
        gs = slice(g * gw, (g + 1) * gw)
        b_g = xa_ref[:, SSD_INNER + g * SSD_STATE:SSD_INNER + (g + 1) * SSD_STATE]
        c_g = xa_ref[:, SSD_INNER + SSD_GN + g * SSD_STATE:SSD_INNER + SSD_GN + (g + 1) * SSD_STATE]
        b_b, c_b = b_g.astype(BF16), c_g.astype(BF16)
        cb = lax.dot_general(c_b, b_b, _NT, preferred_element_type=F32)
        b_t = b_g.T.astype(BF16)
        ex = sum(_dot(t, e_ref[:, gs]) for t in stack)
        dtdec, eacs = ex[0:L], ex[L:2 * L]
        xs = xa_ref[:, gs]
        s_in = st_ref[:, gs]
        y = _dot(c_b, s_in.astype(BF16)) * eacs + xs * dexp_ref[:, gs]
        st_ref[:, gs] = s_in * eacs[L - 1:L, :] + _dot(b_t, (xs * dtdec).astype(BF16))
        ys = []
        for c2 in range(gw // LANES):
            xp = xs[:, c2 * LANES:(c2 + 1) * LANES].astype(BF16)
            pair = []
            for hh in range(LANES // SSD_HEAD_DIM):
                h = g * SSD_HEADS_PER_GROUP + c2 * (LANES // SSD_HEAD_DIM) + hh
                seg = acs[:, h:h + 1] - acs_t[h:h + 1, :]
                lm = jnp.exp(jnp.where(tril, seg, -jnp.inf))
                pair.append(_dot((cb * lm * dt_t[h:h + 1, :]).astype(BF16), xp))
            ys.append(jnp.where(first_half, pair[0], pair[1]))
        y = y + jnp.concatenate(ys, axis=1)
        yz = y * zs_ref[:, gs]
        ms = jnp.mean(yz * yz, axis=1, keepdims=True)
        o_ref[:, gs] = (yz * lax.rsqrt(ms + NORM_EPS) * nw_ref[:, gs]).astype(BF16)


def _ssd(zs, xa, dtp, dtb, alog, dexp, nw, emat, bsz, seq):
    L = SSD_CHUNK
    nc = seq // L
    row = lambda w: pl.BlockSpec((L, w), lambda b, c: (b * nc + c, 0))
    return pl.pallas_call(
        _ssd_kernel,
        grid=(bsz, nc),
        in_specs=[row(SSD_INNER), row(SSD_CONV_DIM), row(LANES),
                  _resident((1, LANES)), _resident((1, LANES)), _resident((1, SSD_INNER)),
                  _resident((1, SSD_INNER)), _resident((LANES, SSD_INNER))],
        out_specs=row(SSD_INNER),
        out_shape=jax.ShapeDtypeStruct((bsz * seq, SSD_INNER), BF16),
        scratch_shapes=[pltpu.VMEM((SSD_STATE, SSD_INNER), F32)],
        compiler_params=_params("parallel", "arbitrary"),
        name="ssd_branch",
    )(zs, xa, dtp, dtb, alog, dexp, nw, emat)


def _merge_kernel(x_ref, a_ref, b_ref, nw_ref, wg_ref, wpa_ref, wps_ref, wo_ref, o_ref):
    x = x_ref[...]
    h = _rms(x, nw_ref[...]).astype(BF16)
    merged = _sigmoid(_dot(h, wg_ref[:, 0:D_MODEL])) * _dot(a_ref[...], wpa_ref[...])
    merged = merged + _sigmoid(_dot(h, wg_ref[:, D_MODEL:2 * D_MODEL])) * _dot(b_ref[...], wps_ref[...])
    o_ref[...] = x + _dot(merged.astype(BF16), wo_ref[...])


def _merge(x2, a, b, nw, wg, wpa, wps, wo):
    tokens = x2.shape[0]
    tm = MERGE_ROWS
    row = lambda w: pl.BlockSpec((tm, w), lambda i: (i, 0))
    return pl.pallas_call(
        _merge_kernel,
        grid=(tokens // tm,),
        in_specs=[row(D_MODEL), row(ATTN_WIDTH), row(SSD_INNER), _resident((1, D_MODEL)),
                  _resident((D_MODEL, 2 * D_MODEL)), _resident((ATTN_WIDTH, D_MODEL)),
                  _resident((SSD_INNER, D_MODEL)), _resident((D_MODEL, D_MODEL))],
        out_specs=row(D_MODEL),
        out_shape=jax.ShapeDtypeStruct((tokens, D_MODEL), F32),
        compiler_params=_params("parallel"),
        name="merge_out",
    )(x2, a, b, nw, wg, wpa, wps, wo)


def _ffn_kernel(x_ref, nw_ref, wup_ref, cw_ref, cb_ref, wdn_ref, fnw_ref, o_ref, act_ref, carry_ref, *, final_norm):
    tm = FFN_ROWS
    tail = SUBLANES
    nf = FFN_DIM // FFN_TILE

    @pl.when(pl.program_id(1) == 0)
    def _():
        carry_ref[...] = jnp.zeros_like(carry_ref)

    x = x_ref[...]
    h = _rms(x, nw_ref[...]).astype(BF16)
    for f in range(nf):
        parts = []
        for part in range(2):
            col = part * FFN_DIM + f * FFN_TILE
            cs = slice(col, col + FFN_TILE)
            ct = part * nf + f
            u = _dot(h, wup_ref[:, cs])
            ue = jnp.concatenate([carry_ref[ct], u], axis=0)
            cv = cb_ref[:, cs] + cw_ref[FFN_CONV - 1:FFN_CONV, cs] * u
            for j in range(FFN_CONV - 1):
                back = FFN_CONV - 1 - j
                cv = cv + cw_ref[j:j + 1, cs] * pltpu.roll(ue, back, 0)[tail:tail + tm]
            carry_ref[ct] = u[tm - tail:tm, :]
            parts.append(cv)
        act_ref[:, f * FFN_TILE:(f + 1) * FFN_TILE] = (parts[0] * _sigmoid(parts[0]) * parts[1]).astype(BF16)
    y = x + _dot(act_ref[...], wdn_ref[...])
    if final_norm:
        y = _rms(y, fnw_ref[...])
    o_ref[...] = y


def _ffn(x2, nw, wup, cw, cb, wdn, fnw, bsz, seq, final_norm):
    tm = FFN_ROWS
    ns = seq // tm
    nf = FFN_DIM // FFN_TILE
    row = pl.BlockSpec((tm, D_MODEL), lambda b, s: (b * ns + s, 0))
    return pl.pallas_call(
        functools.partial(_ffn_kernel, final_norm=final_norm),
        grid=(bsz, ns),
        in_specs=[row, _resident((1, D_MODEL)), _resident((D_MODEL, 2 * FFN_DIM)),
                  _resident((FFN_CONV, 2 * FFN_DIM)), _resident((1, 2 * FFN_DIM)),
                  _resident((FFN_DIM, D_MODEL)), _resident((1, D_MODEL))],
        out_specs=row,
        out_shape=jax.ShapeDtypeStruct((bsz * seq, D_MODEL), F32),
        scratch_shapes=[pltpu.VMEM((tm, FFN_DIM), BF16),
                        pltpu.VMEM((2 * nf, SUBLANES, FFN_TILE), F32)],
        compiler_params=_params("parallel", "arbitrary"),
        name="conv_glu_ffn",
    )(x2, nw, wup, cw, cb, wdn, fnw)


def _split_w_in(w_in):
    o = np.cumsum([0, ATTN_WIDTH, KV_WIDTH, KV_WIDTH, IDX_HEADS * IDX_DIM, IDX_DIM, IDX_HEADS,
                   SSD_INNER, SSD_CONV_DIM, SSD_HEADS, D_MODEL, D_MODEL]).tolist()
    seg = lambda j: w_in[:, o[j]:o[j + 1]]
    d = w_in.shape[0]
    qi = seg(3).reshape(d, IDX_HEADS, IDX_DIM)
    qi = jnp.pad(qi, ((0, 0), (0, 0), (0, LANES - IDX_DIM))).reshape(d, IDX_HEADS * LANES)
    kiw = jnp.pad(jnp.concatenate([seg(4), seg(5)], axis=1), ((0, 0), (0, LANES - IDX_DIM - IDX_HEADS)))
    w_a = jnp.concatenate([seg(0), seg(1), qi, kiw], axis=1).astype(BF16)
    w_vt = seg(2).T.astype(BF16)
    dt = jnp.pad(seg(8), ((0, 0), (0, LANES - SSD_HEADS)))
    w_b = jnp.concatenate([seg(6), seg(7), dt], axis=1).astype(BF16)
    w_g = jnp.concatenate([seg(9), seg(10)], axis=1).astype(BF16)
    return w_a, w_vt, w_b, w_g


def _lane_pad_row(v):
    return jnp.pad(v, (0, LANES - v.shape[0]))[None, :]


def kernel(x, norm_mix_w, w_in, ssd_conv_w, ssd_conv_b, ssd_dt_bias, ssd_a_log, ssd_d, ssd_norm_w, w_proj_attn, w_proj_ssd, w_out, norm_ffn_w, ffn_w_up, ffn_conv_w, ffn_conv_b, ffn_w_down, norm_final_w):
    bsz, seq, d = x.shape
    depth = w_in.shape[0]
    assert d == D_MODEL and seq % max(PROJ_ROWS_A, FFN_ROWS, MERGE_ROWS, IDX_TILE) == 0
    tabs_a = _rope_tables(seq, HEAD_DIM // ROPE_FRACTION)
    tabs_i = _rope_tables(seq, IDX_DIM // ROPE_FRACTION)
    emat = (jnp.arange(SSD_INNER)[None, :] // SSD_HEAD_DIM == jnp.arange(LANES)[:, None]).astype(BF16)
    x2 = x.reshape(bsz * seq, d)
    for l in range(depth):
        w_a, w_vt, w_b, w_g = _split_w_in(w_in[l])
        nw = norm_mix_w[l][None, :]
        q, k, vt, qi, kiwf, kiwb = _proj_attn(x2, nw, w_a, w_vt, tabs_a, tabs_i, seq)
        a = _dsa(q, k, vt, qi, kiwf, kiwb, bsz, seq)
        zs, xa, dtp = _proj_ssd(x2, nw, w_b, ssd_conv_w[l], ssd_conv_b[l][None, :], bsz, seq)
        b = _ssd(zs, xa, dtp, _lane_pad_row(ssd_dt_bias[l]), _lane_pad_row(ssd_a_log[l]),
                 jnp.repeat(ssd_d[l], SSD_HEAD_DIM)[None, :], ssd_norm_w[l][None, :], emat, bsz, seq)
        x2 = _merge(x2, a, b, nw, w_g, w_proj_attn[l].astype(BF16), w_proj_ssd[l].astype(BF16),
                    w_out[l].astype(BF16))
        x2 = _ffn(x2, norm_ffn_w[l][None, :], ffn_w_up[l].astype(BF16), ffn_conv_w[l], ffn_conv_b[l][None, :],
                  ffn_w_down[l].astype(BF16), norm_final_w[None, :], bsz, seq, final_norm=(l == depth - 1))
    return x2.reshape(bsz, seq, d)
```

```python
import functools

import numpy as np
import jax
import jax.numpy as jnp
from jax import lax
from jax.experimental import pallas as pl
from jax.experimental.pallas import tpu as pltpu

D_MODEL = 1024
N_HEADS = 8
HEAD_DIM = 128
N_KV_HEADS = 2
Q_PER_KV = N_HEADS // N_KV_HEADS
ATTN_WIDTH = N_HEADS * HEAD_DIM
KV_WIDTH = N_KV_HEADS * HEAD_DIM
IDX_HEADS = 8
IDX_DIM = 64
TOPK_MAX = 256
ROPE_THETA = 500000.0
ROPE_FRACTION = 4
SSD_INNER = 2 * D_MODEL
SSD_HEAD_DIM = 64
SSD_HEADS = SSD_INNER // SSD_HEAD_DIM
SSD_GROUPS = 4
SSD_HEADS_PER_GROUP = SSD_HEADS // SSD_GROUPS
SSD_STATE = 128
SSD_CONV = 4
SSD_CHUNK = 128
SSD_GN = SSD_GROUPS * SSD_STATE
SSD_CONV_DIM = SSD_INNER + 2 * SSD_GN
SSD_GROUP_WIDTH = SSD_INNER // SSD_GROUPS
FFN_DIM = 2816
FFN_CONV = 3
NORM_EPS = 1e-6

LANES = 128
SUBLANES = 8
VMEM_LIMIT = 56 * 1024 * 1024
Q_BLOCK = 256
IDX_TILE = 512
ATT_TILE = 512
SEARCH_PASSES = 18
PROJ_ROWS_A = 512
PROJ_ROWS_B = 256
MERGE_ROWS = 512
FFN_ROWS = 512
FFN_TILE = 256
MASKED = -1e30
LOG2_E = 1.4426950408889634
INT_MIN = -(2 ** 31)

F32 = jnp.float32
BF16 = jnp.bfloat16
_NT = (((1,), (1,)), ((), ()))
_HI = lax.Precision.HIGHEST


def _params(*sem):
    return pltpu.CompilerParams(dimension_semantics=sem, vmem_limit_bytes=VMEM_LIMIT)


def _resident(shape):
    nd = len(shape)
    return pl.BlockSpec(shape, lambda *_: (0,) * nd, pipeline_mode=pl.Buffered(1))


def _rms(x, w):
    return x * lax.rsqrt(jnp.mean(x * x, axis=-1, keepdims=True) + NORM_EPS) * w


def _sigmoid(x):
    return 1.0 / (1.0 + jnp.exp(-x))


def _dot(a, b):
    return jnp.dot(a, b, preferred_element_type=F32)


def _fold_rows(x, rows, op=jnp.add):
    parts = [x[r:r + rows] for r in range(0, x.shape[0], rows)]
    while len(parts) > 1:
        parts = [op(parts[a], parts[a + 1]) for a in range(0, len(parts) - 1, 2)] + parts[len(parts) & ~1:]
    return parts[0]


def _rope_tables(seq, rot_dim):
    half = rot_dim // 2
    inv = ROPE_THETA ** (-jnp.arange(0, rot_dim, 2, dtype=F32) / rot_dim)
    ang = jnp.arange(seq, dtype=F32)[:, None] * inv[None, :]
    cos, sin = jnp.cos(ang), jnp.sin(ang)
    pad = LANES - rot_dim
    c = jnp.concatenate([cos, cos, jnp.ones((seq, pad), F32)], axis=1)
    s1 = jnp.concatenate([-sin, jnp.zeros((seq, half + pad), F32)], axis=1)
    s2 = jnp.concatenate([jnp.zeros((seq, half), F32), sin, jnp.zeros((seq, pad), F32)], axis=1)
    return c, s1, s2


def _rope(t, c, s1, s2, half):
    return t * c + pltpu.roll(t, LANES - half, 1) * s1 + pltpu.roll(t, half, 1) * s2


_A_Q = 0
_A_K = _A_Q + ATTN_WIDTH
_A_QI = _A_K + KV_WIDTH
_A_KIW = _A_QI + IDX_HEADS * LANES
_A_COLS = _A_KIW + LANES


def _proj_attn_kernel(x_ref, nw_ref, w_ref, wvt_ref, ca_ref, sa1_ref, sa2_ref, ci_ref, si1_ref, si2_ref,
                      q_ref, k_ref, vt_ref, qi_ref, kiwf_ref, kiwb_ref):
    h = _rms(x_ref[...], nw_ref[...]).astype(BF16)
    ca, sa1, sa2 = ca_ref[...], sa1_ref[...], sa2_ref[...]
    ci, si1, si2 = ci_ref[...], si1_ref[...], si2_ref[...]
    ha = HEAD_DIM // ROPE_FRACTION // 2
    hi = IDX_DIM // ROPE_FRACTION // 2
    scale = HEAD_DIM ** -0.5 * LOG2_E
    t = _dot(h, w_ref[:, _A_Q:_A_Q + ATTN_WIDTH])
    for j in range(N_HEADS):
        js = slice(j * LANES, (j + 1) * LANES)
        q_ref[:, js] = (_rope(t[:, js], ca, sa1, sa2, ha) * scale).astype(BF16)
    t = _dot(h, w_ref[:, _A_K:_A_K + KV_WIDTH])
    for j in range(N_KV_HEADS):
        js = slice(j * LANES, (j + 1) * LANES)
        k_ref[:, js] = _rope(t[:, js], ca, sa1, sa2, ha).astype(BF16)
    vt_ref[...] = lax.dot_general(wvt_ref[...], h, _NT, preferred_element_type=F32).astype(BF16)
    t = _dot(h, w_ref[:, _A_QI:_A_QI + IDX_HEADS * LANES])
    for j in range(IDX_HEADS):
        js = slice(j * LANES, (j + 1) * LANES)
        qi_ref[:, js] = _rope(t[:, js], ci, si1, si2, hi).astype(BF16)
    t = _rope(_dot(h, w_ref[:, _A_KIW:_A_KIW + LANES]), ci, si1, si2, hi)
    kiwf_ref[...] = t
    kiwb_ref[...] = t.astype(BF16)


def _proj_attn(x2, nw, w_a, w_vt, tabs_a, tabs_i, seq):
    tokens = x2.shape[0]
    tm = PROJ_ROWS_A
    per_seq = seq // tm
    row = lambda w: pl.BlockSpec((tm, w), lambda i: (i, 0))
    tab = pl.BlockSpec((tm, LANES), lambda i: (i % per_seq, 0))
    sds = jax.ShapeDtypeStruct
    return pl.pallas_call(
        _proj_attn_kernel,
        grid=(tokens // tm,),
        in_specs=[row(D_MODEL), _resident((1, D_MODEL)), _resident((D_MODEL, _A_COLS)),
                  _resident((KV_WIDTH, D_MODEL))] + [tab] * 6,
        out_specs=[row(ATTN_WIDTH), row(KV_WIDTH), pl.BlockSpec((KV_WIDTH, tm), lambda i: (0, i)),
                   row(IDX_HEADS * LANES), row(LANES), row(LANES)],
        out_shape=[sds((tokens, ATTN_WIDTH), BF16), sds((tokens, KV_WIDTH), BF16), sds((KV_WIDTH, tokens), BF16),
                   sds((tokens, IDX_HEADS * LANES), BF16), sds((tokens, LANES), F32), sds((tokens, LANES), BF16)],
        compiler_params=_params("parallel"),
        name="proj_attn",
    )(x2, nw, w_a, w_vt, *tabs_a, *tabs_i)


_B_Z = 0
_B_XBC = _B_Z + SSD_INNER
_B_DT = _B_XBC + SSD_CONV_DIM
_B_COLS = _B_DT + LANES


def _proj_ssd_kernel(x_ref, nw_ref, w_ref, cw_ref, cb_ref, zs_ref, xa_ref, dt_ref, ext_ref):
    tm = PROJ_ROWS_B
    tail = SUBLANES

    @pl.when(pl.program_id(1) == 0)
    def _():
        ext_ref[0:tail, :] = jnp.zeros((tail, SSD_CONV_DIM), F32)

    h = _rms(x_ref[...], nw_ref[...]).astype(BF16)
    z = _dot(h, w_ref[:, _B_Z:_B_Z + SSD_INNER])
    zs_ref[...] = z * _sigmoid(z)
    dt_ref[...] = _dot(h, w_ref[:, _B_DT:_B_DT + LANES])
    cstep = 512
    for c0 in range(0, SSD_CONV_DIM, cstep):
        cs = slice(c0, c0 + cstep)
        ext_ref[tail:tail + tm, cs] = _dot(h, w_ref[:, _B_XBC + c0:_B_XBC + c0 + cstep])
        ue = ext_ref[:, cs]
        u = cb_ref[:, cs] + cw_ref[SSD_CONV - 1:SSD_CONV, cs] * ue[tail:tail + tm]
        for j in range(SSD_CONV - 1):
            back = SSD_CONV - 1 - j
            u = u + cw_ref[j:j + 1, cs] * pltpu.roll(ue, back, 0)[tail:tail + tm]
        xa_ref[:, cs] = u * _sigmoid(u)
        ext_ref[0:tail, cs] = ue[tm:tm + tail]


def _proj_ssd(x2, nw, w_b, cw, cb, bsz, seq):
    tm = PROJ_ROWS_B
    ns = seq // tm
    row = lambda w: pl.BlockSpec((tm, w), lambda b, s: (b * ns + s, 0))
    widths = (SSD_INNER, SSD_CONV_DIM, LANES)
    return pl.pallas_call(
        _proj_ssd_kernel,
        grid=(bsz, ns),
        in_specs=[row(D_MODEL), _resident((1, D_MODEL)), _resident((D_MODEL, _B_COLS)),
                  _resident((SSD_CONV, SSD_CONV_DIM)), _resident((1, SSD_CONV_DIM))],
        out_specs=[row(w) for w in widths],
        out_shape=[jax.ShapeDtypeStruct((bsz * seq, w), F32) for w in widths],
        scratch_shapes=[pltpu.VMEM((tm + SUBLANES, SSD_CONV_DIM), F32)],
        compiler_params=_params("parallel", "arbitrary"),
        name="proj_ssd",
    )(x2, nw, w_b, cw, cb)


def _sortable(x):
    return x ^ ((x >> 31) & jnp.int32(0x7FFFFFFF))


def _reduce_rows(x, op):
    for sh in (4, 2, 1):
        x = op(x, pltpu.roll(x, sh, 0))
    return x[0:1, :]


def _dsa_kernel(q_ref, qi_ref, wq_ref, k_ref, vt_ref, kib_ref, o_ref, key_ref, sel_ref, *, topk, seq):
    i = pl.program_id(1)
    n_idx = (i + IDX_TILE // Q_BLOCK) // (IDX_TILE // Q_BLOCK)
    n_att = (i + ATT_TILE // Q_BLOCK) // (ATT_TILE // Q_BLOCK)
    kpos = lax.broadcasted_iota(jnp.int32, (IDX_TILE, Q_BLOCK), 0)
    qpos = lax.broadcasted_iota(jnp.int32, (IDX_TILE, Q_BLOCK), 1) + i * Q_BLOCK
    w_t = wq_ref[...].T
    qi_all = jnp.concatenate([qi_ref[:, h * LANES:(h + 1) * LANES] for h in range(IDX_HEADS)], axis=0)
    kf = float(topk)
    int_max = jnp.int32(2 ** 31 - 1)

    def tile(j):
        return pl.ds(pl.multiple_of(j * IDX_TILE, IDX_TILE), IDX_TILE)

    def idx_body(j, carry):
        kmax, kmin = carry
        s = lax.dot_general(kib_ref[tile(j), :], qi_all, _NT, preferred_element_type=F32)
        acc = jnp.zeros((IDX_TILE, Q_BLOCK), F32)
        for h in range(IDX_HEADS):
            acc = acc + w_t[IDX_DIM + h:IDX_DIM + h + 1, :] * jnp.maximum(s[:, h * Q_BLOCK:(h + 1) * Q_BLOCK], 0.0)
        key = _sortable(pltpu.bitcast(acc, jnp.int32))
        causal = kpos + j * IDX_TILE <= qpos
        low = jnp.where(causal, key, jnp.int32(INT_MIN))
        key_ref[tile(j), :] = low
        kmax = jnp.maximum(kmax, _fold_rows(low, SUBLANES, jnp.maximum))
        kmin = jnp.minimum(kmin, _fold_rows(jnp.where(causal, key, int_max), SUBLANES, jnp.minimum))
        return kmax, kmin

    kmax, kmin = lax.fori_loop(0, n_idx, idx_body, (jnp.full((SUBLANES, Q_BLOCK), INT_MIN, jnp.int32),
                                                    jnp.full((SUBLANES, Q_BLOCK), 2 ** 31 - 1, jnp.int32)))
    kmax = _reduce_rows(kmax, jnp.maximum)
    kmin = _reduce_rows(kmin, jnp.minimum)

    def count(pred):
        def body(j, cnt):
            m = jnp.where(pred(key_ref[tile(j), :], kpos + j * IDX_TILE), 1.0, 0.0)
            return cnt + _fold_rows(m, SUBLANES)
        cnt = lax.fori_loop(0, n_idx, body, jnp.zeros((SUBLANES, Q_BLOCK), F32))
        return jnp.sum(cnt, axis=0, keepdims=True)

    n_valid = (qpos[0:1, :] + 1).astype(F32)
    few = n_valid <= kf
    flat = kmin == kmax
    start = few | flat
    thr0 = jnp.where(few, jnp.int32(INT_MIN + 1), kmin)
    state = (jnp.where(start, thr0, kmin), jnp.where(start, thr0, kmax + 1), n_valid)

    def search_pass(it, st):
        lo, hi, c_lo = st
        f_lo = pltpu.bitcast(_sortable(lo), F32)
        f_hi = pltpu.bitcast(_sortable(hi - 1), F32)
        piv = _sortable(pltpu.bitcast(0.5 * f_lo + 0.5 * f_hi, jnp.int32))
        piv = jnp.where(it % 4 == 3, lo + lax.shift_right_logical(hi - lo, 1), piv)
        piv = jnp.where(it == 0, jnp.int32(1), piv)
        piv = jnp.where(it == 1, jnp.where(hi == 1, jnp.int32(-1), piv), piv)
        piv = jnp.minimum(jnp.maximum(piv, lo + 1), hi - 1)
        c = count(lambda kt, pos: kt >= piv)
        live = lo != hi
        ge = c >= kf
        hit = c == kf
        lo_n = jnp.where(live & ge, piv, lo)
        c_lo_n = jnp.where(live & ge, c, c_lo)
        hi_n = jnp.where(live & jnp.logical_not(ge), piv, hi)
        hi_n = jnp.where(live & (hit | (hi_n - lo_n == 1)), lo_n, hi_n)
        return lo_n, hi_n, c_lo_n

    state = lax.fori_loop(0, SEARCH_PASSES, search_pass, state)

    def pending(st):
        return jnp.max(jnp.where(st[0] != st[1], 1.0, 0.0))

    def more_cond(carry):
        return jnp.logical_and(carry[1] > 0.0, carry[0] < 1024)

    def more_body(carry):
        st = search_pass(carry[0] + 1, search_pass(carry[0], carry[2:]))
        return (carry[0] + 2, pending(st)) + st

    thr, _, c_ge = lax.while_loop(more_cond, more_body, (jnp.int32(SEARCH_PASSES), pending(state)) + state)[2:]
    tied = jnp.max(c_ge) > kf

    def write_sel(pred):
        def body(j, carry):
            sel = pred(key_ref[tile(j), :], kpos + j * IDX_TILE)
            sel_ref[tile(j), :] = jnp.where(sel, 1.0, 0.0).astype(BF16)
            return carry
        lax.fori_loop(0, n_idx, body, 0)

    @pl.when(jnp.logical_not(tied))
    def _():
        write_sel(lambda kt, pos: kt >= thr)

    @pl.when(tied)
    def _():
        need = kf - count(lambda kt, pos: kt > thr)
        nbits = max(1, int(np.ceil(np.log2(seq))))

        def pos_body(p, lim):
            cand = lim + lax.shift_left(jnp.int32(1), nbits - 1 - p)
            c = count(lambda kt, pos: jnp.where(kt == thr, pos, jnp.int32(seq)) < cand)
            return jnp.where(c < need, cand, lim)

        lim = lax.fori_loop(0, nbits, pos_body, jnp.zeros((1, Q_BLOCK), jnp.int32))
        lim = jnp.where(c_ge > kf, lim, jnp.int32(seq))
        write_sel(lambda kt, pos: (kt > thr) | ((kt == thr) & (pos <= lim)))

    rows = Q_PER_KV * Q_BLOCK
    pk = 2 * SUBLANES
    qgs = [jnp.concatenate([q_ref[:, (Q_PER_KV * g + e) * LANES:(Q_PER_KV * g + e + 1) * LANES]
                            for e in range(Q_PER_KV)], axis=0) for g in range(N_KV_HEADS)]
    ones_rows = jnp.ones((pk, ATT_TILE), BF16)

    def logits(j, g):
        ks = pl.ds(pl.multiple_of(j * ATT_TILE, ATT_TILE), ATT_TILE)
        return lax.dot_general(k_ref[ks, g * LANES:(g + 1) * LANES], qgs[g], _NT, preferred_element_type=F32)

    def weighted_v(j, g, p):
        ks = pl.ds(pl.multiple_of(j * ATT_TILE, ATT_TILE), ATT_TILE)
        sel = jnp.concatenate([sel_ref[ks, :]] * Q_PER_KV, axis=1)
        v_aug = jnp.concatenate([vt_ref[g * LANES:(g + 1) * LANES, ks], ones_rows], axis=0)
        return _dot(v_aug, p.astype(BF16) * sel)

    shifts = [jnp.max(logits(0, g), axis=0, keepdims=True) for g in range(N_KV_HEADS)]

    def fast_body(j, accs):
        return tuple(accs[g] + weighted_v(j, g, jnp.exp2(logits(j, g) - shifts[g])) for g in range(N_KV_HEADS))

    accs = lax.fori_loop(0, n_att, fast_body, (jnp.zeros((LANES + pk, rows), F32),) * N_KV_HEADS)
    lo_ok, hi_ok = 2.0 ** -60, 2.0 ** 60
    worst = [jnp.max(jnp.where((a[LANES:LANES + 1] >= lo_ok) & (a[LANES:LANES + 1] <= hi_ok), 0.0, 1.0))
             + jnp.max(jnp.where(jnp.abs(a[0:LANES]) < jnp.inf, 0.0, 1.0)) for a in accs]
    redo = (worst[0] + worst[1]) > 0.0

    def store(g, acc):
        o_t = acc[0:LANES] / acc[LANES:LANES + 1]
        for e in range(Q_PER_KV):
            col = (Q_PER_KV * g + e) * LANES
            o_ref[:, col:col + LANES] = o_t[:, e * Q_BLOCK:(e + 1) * Q_BLOCK].T.astype(BF16)

    @pl.when(jnp.logical_not(redo))
    def _():
        for g in range(N_KV_HEADS):
            store(g, accs[g])

    @pl.when(redo)
    def _():
        def safe_body(j, carry):
            ks = pl.ds(pl.multiple_of(j * ATT_TILE, ATT_TILE), ATT_TILE)
            sel = jnp.concatenate([sel_ref[ks, :]] * Q_PER_KV, axis=1).astype(F32)
            out = []
            for g in range(N_KV_HEADS):
                m, acc = carry[g]
                s = logits(j, g) + (sel - 1.0) * (-MASKED)
                m_new = jnp.maximum(m, jnp.max(s, axis=0, keepdims=True))
                out.append((m_new, jnp.exp2(m - m_new) * acc + weighted_v(j, g, jnp.exp2(s - m_new))))
            return tuple(out)

        init = (jnp.full((1, rows), MASKED, F32), jnp.zeros((LANES + pk, rows), F32))
        res = lax.fori_loop(0, n_att, safe_body, (init,) * N_KV_HEADS)
        for g in range(N_KV_HEADS):
            store(g, res[g][1])


def _dsa(q, k, vt, qi, kiwf, kiwb, bsz, seq):
    assert seq % IDX_TILE == 0 and seq % ATT_TILE == 0
    nq = seq // Q_BLOCK
    topk = min(TOPK_MAX, seq // 4)
    qrow = lambda w: pl.BlockSpec((Q_BLOCK, w), lambda b, i: (b * nq + i, 0))
    full = lambda w: pl.BlockSpec((seq, w), lambda b, i: (b, 0))
    return pl.pallas_call(
        functools.partial(_dsa_kernel, topk=topk, seq=seq),
        grid=(bsz, nq),
        in_specs=[qrow(ATTN_WIDTH), qrow(IDX_HEADS * LANES), qrow(LANES), full(KV_WIDTH),
                  pl.BlockSpec((KV_WIDTH, seq), lambda b, i: (0, b)), full(LANES)],
        out_specs=qrow(ATTN_WIDTH),
        out_shape=jax.ShapeDtypeStruct((bsz * seq, ATTN_WIDTH), BF16),
        scratch_shapes=[pltpu.VMEM((seq, Q_BLOCK), jnp.int32), pltpu.VMEM((seq, Q_BLOCK), BF16)],
        compiler_params=_params("parallel", "arbitrary"),
        name="dsa_attention",
    )(q, qi, kiwf, k, vt, kiwb)


def _split3(x):
    hi = x.astype(BF16)
    r = x - hi.astype(F32)
    mid = r.astype(BF16)
    return hi, mid, (r - mid.astype(F32)).astype(BF16)


def _ssd_kernel(zs_ref, xa_ref, dt_ref, dtb_ref, alog_ref, dexp_ref, nw_ref, e_ref, o_ref, st_ref):
    L = SSD_CHUNK

    @pl.when(pl.program_id(1) == 0)
    def _():
        st_ref[...] = jnp.zeros_like(st_ref)

    dtv = dt_ref[...] + dtb_ref[...]
    dt = jnp.maximum(dtv, 0.0) + jnp.log1p(jnp.exp(-jnp.abs(dtv)))
    adt = dt * (-jnp.exp(alog_ref[...]))
    ri = lax.broadcasted_iota(jnp.int32, (L, L), 0)
    ci = lax.broadcasted_iota(jnp.int32, (L, L), 1)
    tril = ci <= ri
    tril_b = jnp.where(tril, 1.0, 0.0).astype(BF16)
    acs = sum(_dot(tril_b, t) for t in _split3(adt))
    last = acs[L - 1:L, :]
    acs_t = acs.T
    dt_t = dt.T
    hi = jnp.concatenate([dt * jnp.exp(last - acs), jnp.exp(acs)], axis=0)
    stack = (hi.astype(BF16), (hi - hi.astype(BF16).astype(F32)).astype(BF16))
    first_half = ci < SSD_HEAD_DIM
    gw = SSD_GROUP_WIDTH

    for g in range(SSD_GROUPS):
        gs = slice(g * gw, (g + 1) * gw)
        b_g = xa_ref[:, SSD_INNER + g * SSD_STATE:SSD_INNER + (g + 1) * SSD_STATE]
        c_g = xa_ref[:, SSD_INNER + SSD_GN + g * SSD_STATE:SSD_INNER + SSD_GN + (g + 1) * SSD_STATE]
        b_b, c_b = b_g.astype(BF16), c_g.astype(BF16)
        cb = lax.dot_general(c_b, b_b, _NT, preferred_element_type=F32)
        b_t = b_g.T.astype(BF16)
        ex = sum(_dot(t, e_ref[:, gs]) for t in stack)
        dtdec, eacs = ex[0:L], ex[L:2 * L]
        xs = xa_ref[:, gs]
        s_in = st_ref[:, gs]
        y = _dot(c_b, s_in.astype(BF16)) * eacs + xs * dexp_ref[:, gs]
        st_ref[:, gs] = s_in * eacs[L - 1:L, :] + _dot(b_t, (xs * dtdec).astype(BF16))
        ys = []
        for c2 in range(gw // LANES):
            xp = xs[:, c2 * LANES:(c2 + 1) * LANES].astype(BF16)
            pair = []
            for hh in range(LANES // SSD_HEAD_DIM):
                h = g * SSD_HEADS_PER_GROUP + c2 * (LANES // SSD_HEAD_DIM) + hh
                seg = acs[:, h:h + 1] - acs_t[h:h + 1, :]
                lm = jnp.exp(jnp.where(tril, seg, -jnp.inf))
                pair.append(_dot((cb * lm * dt_t[h:h + 1, :]).astype(BF16), xp))
            ys.append(jnp.where(first_half, pair[0], pair[1]))
        y = y + jnp.concatenate(ys, axis=1)
        yz = y * zs_ref[:, gs]
        ms = jnp.mean(yz * yz, axis=1, keepdims=True)
        o_ref[:, gs] = (yz * lax.rsqrt(ms + NORM_EPS) * nw_ref[:, gs]).astype(BF16)


def _ssd(zs, xa, dtp, dtb, alog, dexp, nw, emat, bsz, seq):
    L = SSD_CHUNK
    nc = seq // L
    row = lambda w: pl.BlockSpec((L, w), lambda b, c: (b * nc + c, 0))
    return pl.pallas_call(
        _ssd_kernel,
        grid=(bsz, nc),
        in_specs=[row(SSD_INNER), row(SSD_CONV_DIM), row(LANES),
                  _resident((1, LANES)), _resident((1, LANES)), _resident((1, SSD_INNER)),
                  _resident((1, SSD_INNER)), _resident((LANES, SSD_INNER))],
        out_specs=row(SSD_INNER),
        out_shape=jax.ShapeDtypeStruct((bsz * seq, SSD_INNER), BF16),
        scratch_shapes=[pltpu.VMEM((SSD_STATE, SSD_INNER), F32)],
        compiler_params=_params("parallel", "arbitrary"),
        name="ssd_branch",
    )(zs, xa, dtp, dtb, alog, dexp, nw, emat)


def _merge_kernel(x_ref, a_ref, b_ref, nw_ref, wg_ref, wpa_ref, wps_ref, wo_ref, o_ref):
    x = x_ref[...]
    h = _rms(x, nw_ref[...]).astype(BF16)
    merged = _sigmoid(_dot(h, wg_ref[:, 0:D_MODEL])) * _dot(a_ref[...], wpa_ref[...])
    merged = merged + _sigmoid(_dot(h, wg_ref[:, D_MODEL:2 * D_MODEL])) * _dot(b_ref[...], wps_ref[...])
    o_ref[...] = x + _dot(merged.astype(BF16), wo_ref[...])


def _merge(x2, a, b, nw, wg, wpa, wps, wo):
    tokens = x2.shape[0]
    tm = MERGE_ROWS
    row = lambda w: pl.BlockSpec((tm, w), lambda i: (i, 0))
    return pl.pallas_call(
        _merge_kernel,
        grid=(tokens // tm,),
        in_specs=[row(D_MODEL), row(ATTN_WIDTH), row(SSD_INNER), _resident((1, D_MODEL)),
                  _resident((D_MODEL, 2 * D_MODEL)), _resident((ATTN_WIDTH, D_MODEL)),
                  _resident((SSD_INNER, D_MODEL)), _resident((D_MODEL, D_MODEL))],
        out_specs=row(D_MODEL),
        out_shape=jax.ShapeDtypeStruct((tokens, D_MODEL), F32),
        compiler_params=_params("parallel"),
        name="merge_out",
    )(x2, a, b, nw, wg, wpa, wps, wo)


def _ffn_kernel(x_ref, nw_ref, wup_ref, cw_ref, cb_ref, wdn_ref, fnw_ref, o_ref, act_ref, carry_ref, *, final_norm):
    tm = FFN_ROWS
    tail = SUBLANES
    nf = FFN_DIM // FFN_TILE

    @pl.when(pl.program_id(1) == 0)
    def _():
        carry_ref[...] = jnp.zeros_like(carry_ref)

    x = x_ref[...]
    h = _rms(x, nw_ref[...]).astype(BF16)
    for f in range(nf):
        parts = []
        for part in range(2):
            col = part * FFN_DIM + f * FFN_TILE
            cs = slice(col, col + FFN_TILE)
            ct = part * nf + f
            u = _dot(h, wup_ref[:, cs])
            ue = jnp.concatenate([carry_ref[ct], u], axis=0)
            cv = cb_ref[:, cs] + cw_ref[FFN_CONV - 1:FFN_CONV, cs] * u
            for j in range(FFN_CONV - 1):
                back = FFN_CONV - 1 - j
                cv = cv + cw_ref[j:j + 1, cs] * pltpu.roll(ue, back, 0)[tail:tail + tm]
            carry_ref[ct] = u[tm - tail:tm, :]
            parts.append(cv)
        act_ref[:, f * FFN_TILE:(f + 1) * FFN_TILE] = (parts[0] * _sigmoid(parts[0]) * parts[1]).astype(BF16)
    y = x + _dot(act_ref[...], wdn_ref[...])
    if final_norm:
        y = _rms(y, fnw_ref[...])
    o_ref[...] = y


def _ffn(x2, nw, wup, cw, cb, wdn, fnw, bsz, seq, final_norm):
    tm = FFN_ROWS
    ns = seq // tm
    nf = FFN_DIM // FFN_TILE
    row = pl.BlockSpec((tm, D_MODEL), lambda b, s: (b * ns + s, 0))
    return pl.pallas_call(
        functools.partial(_ffn_kernel, final_norm=final_norm),
        grid=(bsz, ns),
        in_specs=[row, _resident((1, D_MODEL)), _resident((D_MODEL, 2 * FFN_DIM)),
                  _resident((FFN_CONV, 2 * FFN_DIM)), _resident((1, 2 * FFN_DIM)),
                  _resident((FFN_DIM, D_MODEL)), _resident((1, D_MODEL))],
        out_specs=row,
        out_shape=jax.ShapeDtypeStruct((bsz * seq, D_MODEL), F32),
        scratch_shapes=[pltpu.VMEM((tm, FFN_DIM), BF16),
                        pltpu.VMEM((2 * nf, SUBLANES, FFN_TILE), F32)],
        compiler_params=_params("parallel", "arbitrary"),
        name="conv_glu_ffn",
    )(x2, nw, wup, cw, cb, wdn, fnw)


def _split_w_in(w_in):
    o = np.cumsum([0, ATTN_WIDTH, KV_WIDTH, KV_WIDTH, IDX_HEADS * IDX_DIM, IDX_DIM, IDX_HEADS,
                   SSD_INNER, SSD_CONV_DIM, SSD_HEADS, D_MODEL, D_MODEL]).tolist()
    seg = lambda j: w_in[:, o[j]:o[j + 1]]
    d = w_in.shape[0]
    qi = seg(3).reshape(d, IDX_HEADS, IDX_DIM)
    qi = jnp.pad(qi, ((0, 0), (0, 0), (0, LANES - IDX_DIM))).reshape(d, IDX_HEADS * LANES)
    kiw = jnp.pad(jnp.concatenate([seg(4), seg(5)], axis=1), ((0, 0), (0, LANES - IDX_DIM - IDX_HEADS)))
    w_a = jnp.concatenate([seg(0), seg(1), qi, kiw], axis=1).astype(BF16)
    w_vt = seg(2).T.astype(BF16)
    dt = jnp.pad(seg(8), ((0, 0), (0, LANES - SSD_HEADS)))
    w_b = jnp.concatenate([seg(6), seg(7), dt], axis=1).astype(BF16)
    w_g = jnp.concatenate([seg(9), seg(10)], axis=1).astype(BF16)
    return w_a, w_vt, w_b, w_g


def _lane_pad_row(v):
    return jnp.pad(v, (0, LANES - v.shape[0]))[None, :]


def kernel(x, norm_mix_w, w_in, ssd_conv_w, ssd_conv_b, ssd_dt_bias, ssd_a_log, ssd_d, ssd_norm_w, w_proj_attn, w_proj_ssd, w_out, norm_ffn_w, ffn_w_up, ffn_conv_w, ffn_conv_b, ffn_w_down, norm_final_w):
    bsz, seq, d = x.shape
    depth = w_in.shape[0]
    assert d == D_MODEL and seq % max(PROJ_ROWS_A, FFN_ROWS, MERGE_ROWS, IDX_TILE) == 0
    tabs_a = _rope_tables(seq, HEAD_DIM // ROPE_FRACTION)
    tabs_i = _rope_tables(seq, IDX_DIM // ROPE_FRACTION)
    emat = (jnp.arange(SSD_INNER)[None, :] // SSD_HEAD_DIM == jnp.arange(LANES)[:, None]).astype(BF16)
    x2 = x.reshape(bsz * seq, d)
    for l in range(depth):
        w_a, w_vt, w_b, w_g = _split_w_in(w_in[l])
        nw = norm_mix_w[l][None, :]
        q, k, vt, qi, kiwf, kiwb = _proj_attn(x2, nw, w_a, w_vt, tabs_a, tabs_i, seq)
        a = _dsa(q, k, vt, qi, kiwf, kiwb, bsz, seq)
        zs, xa, dtp = _proj_ssd(x2, nw, w_b, ssd_conv_w[l], ssd_conv_b[l][None, :], bsz, seq)
        b = _ssd(zs, xa, dtp, _lane_pad_row(ssd_dt_bias[l]), _lane_pad_row(ssd_a_log[l]),
                 jnp.repeat(ssd_d[l], SSD_HEAD_DIM)[None, :], ssd_norm_w[l][None, :], emat, bsz, seq)
        x2 = _merge(x2, a, b, nw, w_g, w_proj_attn[l].astype(BF16), w_proj_ssd[l].astype(BF16),
                    w_out[l].astype(BF16))
        x2 = _ffn(x2, norm_ffn_w[l][None, :], ffn_w_up[l].astype(BF16), ffn_conv_w[l], ffn_conv_b[l][None, :],
                  ffn_w_down[l].astype(BF16), norm_final_w[None, :], bsz, seq, final_norm=(l == depth - 1))
    return x2.reshape(bsz, seq, d)
```

```python
import functools

import numpy as np
import jax
import jax.numpy as jnp
from jax import lax
from jax.experimental import pallas as pl
from jax.experimental.pallas import tpu as pltpu

D_MODEL = 1024
N_HEADS = 8
HEAD_DIM = 128
N_KV_HEADS = 2
Q_PER_KV = N_HEADS // N_KV_HEADS
ATTN_WIDTH = N_HEADS * HEAD_DIM
KV_WIDTH = N_KV_HEADS * HEAD_DIM
IDX_HEADS = 8
IDX_DIM = 64
TOPK_MAX = 256
ROPE_THETA = 500000.0
ROPE_FRACTION = 4
SSD_INNER = 2 * D_MODEL
SSD_HEAD_DIM = 64
SSD_HEADS = SSD_INNER // SSD_HEAD_DIM
SSD_GROUPS = 4
SSD_HEADS_PER_GROUP = SSD_HEADS // SSD_GROUPS
SSD_STATE = 128
SSD_CONV = 4
SSD_CHUNK = 128
SSD_GN = SSD_GROUPS * SSD_STATE
SSD_CONV_DIM = SSD_INNER + 2 * SSD_GN
SSD_GROUP_WIDTH = SSD_INNER // SSD_GROUPS
FFN_DIM = 2816
FFN_CONV = 3
NORM_EPS = 1e-6

LANES = 128
SUBLANES = 8
VMEM_LIMIT = 56 * 1024 * 1024
Q_BLOCK = 256
IDX_TILE = 512
ATT_TILE = 512
SEARCH_PASSES = 18
SHIFT_KEYS = 128
PROJ_ROWS_A = 512
PROJ_ROWS_B = 256
MERGE_ROWS = 512
FFN_ROWS = 512
FFN_TILE = 256
MASKED = -1e30
LOG2_E = 1.4426950408889634
INT_MIN = -(2 ** 31)

F32 = jnp.float32
BF16 = jnp.bfloat16
_NT = (((1,), (1,)), ((), ()))
_HI = lax.Precision.HIGHEST


def _params(*sem):
    return pltpu.CompilerParams(dimension_semantics=sem, vmem_limit_bytes=VMEM_LIMIT)


def _resident(shape):
    nd = len(shape)
    return pl.BlockSpec(shape, lambda *_: (0,) * nd, pipeline_mode=pl.Buffered(1))


def _rms(x, w):
    return x * lax.rsqrt(jnp.mean(x * x, axis=-1, keepdims=True) + NORM_EPS) * w


def _sigmoid(x):
    return 1.0 / (1.0 + jnp.exp(-x))


def _dot(a, b):
    return jnp.dot(a, b, preferred_element_type=F32)


def _fold_rows(x, rows, op=jnp.add):
    parts = [x[r:r + rows] for r in range(0, x.shape[0], rows)]
    while len(parts) > 1:
        parts = [op(parts[a], parts[a + 1]) for a in range(0, len(parts) - 1, 2)] + parts[len(parts) & ~1:]
    return parts[0]


def _rope_tables(seq, rot_dim):
    half = rot_dim // 2
    inv = ROPE_THETA ** (-jnp.arange(0, rot_dim, 2, dtype=F32) / rot_dim)
    ang = jnp.arange(seq, dtype=F32)[:, None] * inv[None, :]
    cos, sin = jnp.cos(ang), jnp.sin(ang)
    pad = LANES - rot_dim
    c = jnp.concatenate([cos, cos, jnp.ones((seq, pad), F32)], axis=1)
    s1 = jnp.concatenate([-sin, jnp.zeros((seq, half + pad), F32)], axis=1)
    s2 = jnp.concatenate([jnp.zeros((seq, half), F32), sin, jnp.zeros((seq, pad), F32)], axis=1)
    return c, s1, s2


def _rope(t, c, s1, s2, half):
    return t * c + pltpu.roll(t, LANES - half, 1) * s1 + pltpu.roll(t, half, 1) * s2


_A_Q = 0
_A_K = _A_Q + ATTN_WIDTH
_A_QI = _A_K + KV_WIDTH
_A_KIW = _A_QI + IDX_HEADS * LANES
_A_COLS = _A_KIW + LANES


def _proj_attn_kernel(x_ref, nw_ref, w_ref, wvt_ref, ca_ref, sa1_ref, sa2_ref, ci_ref, si1_ref, si2_ref,
                      q_ref, k_ref, vt_ref, qi_ref, kiwf_ref, kiwb_ref):
    h = _rms(x_ref[...], nw_ref[...]).astype(BF16)
    ca, sa1, sa2 = ca_ref[...], sa1_ref[...], sa2_ref[...]
    ci, si1, si2 = ci_ref[...], si1_ref[...], si2_ref[...]
    ha = HEAD_DIM // ROPE_FRACTION // 2
    hi = IDX_DIM // ROPE_FRACTION // 2
    scale = HEAD_DIM ** -0.5 * LOG2_E
    t = _dot(h, w_ref[:, _A_Q:_A_Q + ATTN_WIDTH])
    for j in range(N_HEADS):
        js = slice(j * LANES, (j + 1) * LANES)
        q_ref[:, js] = (_rope(t[:, js], ca, sa1, sa2, ha) * scale).astype(BF16)
    t = _dot(h, w_ref[:, _A_K:_A_K + KV_WIDTH])
    for j in range(N_KV_HEADS):
        js = slice(j * LANES, (j + 1) * LANES)
        k_ref[:, js] = _rope(t[:, js], ca, sa1, sa2, ha).astype(BF16)
    vt_ref[...] = lax.dot_general(wvt_ref[...], h, _NT, preferred_element_type=F32).astype(BF16)
    t = _dot(h, w_ref[:, _A_QI:_A_QI + IDX_HEADS * LANES])
    for j in range(IDX_HEADS):
        js = slice(j * LANES, (j + 1) * LANES)
        qi_ref[:, js] = _rope(t[:, js], ci, si1, si2, hi).astype(BF16)
    t = _rope(_dot(h, w_ref[:, _A_KIW:_A_KIW + LANES]), ci, si1, si2, hi)
    kiwf_ref[...] = t
    kiwb_ref[...] = t.astype(BF16)


def _proj_attn(x2, nw, w_a, w_vt, tabs_a, tabs_i, seq):
    tokens = x2.shape[0]
    tm = PROJ_ROWS_A
    per_seq = seq // tm
    row = lambda w: pl.BlockSpec((tm, w), lambda i: (i, 0))
    tab = pl.BlockSpec((tm, LANES), lambda i: (i % per_seq, 0))
    sds = jax.ShapeDtypeStruct
    return pl.pallas_call(
        _proj_attn_kernel,
        grid=(tokens // tm,),
        in_specs=[row(D_MODEL), _resident((1, D_MODEL)), _resident((D_MODEL, _A_COLS)),
                  _resident((KV_WIDTH, D_MODEL))] + [tab] * 6,
        out_specs=[row(ATTN_WIDTH), row(KV_WIDTH), pl.BlockSpec((KV_WIDTH, tm), lambda i: (0, i)),
                   row(IDX_HEADS * LANES), row(LANES), row(LANES)],
        out_shape=[sds((tokens, ATTN_WIDTH), BF16), sds((tokens, KV_WIDTH), BF16), sds((KV_WIDTH, tokens), BF16),
                   sds((tokens, IDX_HEADS * LANES), BF16), sds((tokens, LANES), F32), sds((tokens, LANES), BF16)],
        compiler_params=_params("parallel"),
        name="proj_attn",
    )(x2, nw, w_a, w_vt, *tabs_a, *tabs_i)


_B_Z = 0
_B_XBC = _B_Z + SSD_INNER
_B_DT = _B_XBC + SSD_CONV_DIM
_B_COLS = _B_DT + LANES


def _proj_ssd_kernel(x_ref, nw_ref, w_ref, cw_ref, cb_ref, zs_ref, xa_ref, dt_ref, ext_ref):
    tm = PROJ_ROWS_B
    tail = SUBLANES

    @pl.when(pl.program_id(1) == 0)
    def _():
        ext_ref[0:tail, :] = jnp.zeros((tail, SSD_CONV_DIM), F32)

    h = _rms(x_ref[...], nw_ref[...]).astype(BF16)
    z = _dot(h, w_ref[:, _B_Z:_B_Z + SSD_INNER])
    zs_ref[...] = z * _sigmoid(z)
    dt_ref[...] = _dot(h, w_ref[:, _B_DT:_B_DT + LANES])
    cstep = 512
    for c0 in range(0, SSD_CONV_DIM, cstep):
        cs = slice(c0, c0 + cstep)
        ext_ref[tail:tail + tm, cs] = _dot(h, w_ref[:, _B_XBC + c0:_B_XBC + c0 + cstep])
        ue = ext_ref[:, cs]
        u = cb_ref[:, cs] + cw_ref[SSD_CONV - 1:SSD_CONV, cs] * ue[tail:tail + tm]
        for j in range(SSD_CONV - 1):
            back = SSD_CONV - 1 - j
            u = u + cw_ref[j:j + 1, cs] * pltpu.roll(ue, back, 0)[tail:tail + tm]
        xa_ref[:, cs] = u * _sigmoid(u)
        ext_ref[0:tail, cs] = ue[tm:tm + tail]


def _proj_ssd(x2, nw, w_b, cw, cb, bsz, seq):
    tm = PROJ_ROWS_B
    ns = seq // tm
    row = lambda w: pl.BlockSpec((tm, w), lambda b, s: (b * ns + s, 0))
    widths = (SSD_INNER, SSD_CONV_DIM, LANES)
    return pl.pallas_call(
        _proj_ssd_kernel,
        grid=(bsz, ns),
        in_specs=[row(D_MODEL), _resident((1, D_MODEL)), _resident((D_MODEL, _B_COLS)),
                  _resident((SSD_CONV, SSD_CONV_DIM)), _resident((1, SSD_CONV_DIM))],
        out_specs=[row(w) for w in widths],
        out_shape=[jax.ShapeDtypeStruct((bsz * seq, w), F32) for w in widths],
        scratch_shapes=[pltpu.VMEM((tm + SUBLANES, SSD_CONV_DIM), F32)],
        compiler_params=_params("parallel", "arbitrary"),
        name="proj_ssd",
    )(x2, nw, w_b, cw, cb)


def _sortable(x):
    return x ^ ((x >> 31) & jnp.int32(0x7FFFFFFF))


def _reduce_rows(x, op):
    for sh in (4, 2, 1):
        x = op(x, pltpu.roll(x, sh, 0))
    return x[0:1, :]


def _dsa_kernel(q_ref, qi_ref, wq_ref, k_ref, vt_ref, kib_ref, o_ref, key_ref, sel_ref, *, topk, seq):
    i = pl.program_id(1)
    n_idx = (i + IDX_TILE // Q_BLOCK) // (IDX_TILE // Q_BLOCK)
    n_att = (i + ATT_TILE // Q_BLOCK) // (ATT_TILE // Q_BLOCK)
    kpos = lax.broadcasted_iota(jnp.int32, (IDX_TILE, Q_BLOCK), 0)
    qpos = lax.broadcasted_iota(jnp.int32, (IDX_TILE, Q_BLOCK), 1) + i * Q_BLOCK
    w_t = wq_ref[...].T
    qi_all = jnp.concatenate([qi_ref[:, h * LANES:(h + 1) * LANES] for h in range(IDX_HEADS)], axis=0)
    kf = float(topk)
    int_max = jnp.int32(2 ** 31 - 1)

    def tile(j):
        return pl.ds(pl.multiple_of(j * IDX_TILE, IDX_TILE), IDX_TILE)

    def idx_body(j, carry, diagonal):
        kmax, kmin = carry
        s = lax.dot_general(kib_ref[tile(j), :], qi_all, _NT, preferred_element_type=F32)
        acc = jnp.zeros((IDX_TILE, Q_BLOCK), F32)
        for h in range(IDX_HEADS):
            acc = acc + w_t[IDX_DIM + h:IDX_DIM + h + 1, :] * jnp.maximum(s[:, h * Q_BLOCK:(h + 1) * Q_BLOCK], 0.0)
        key = _sortable(pltpu.bitcast(acc, jnp.int32))
        low = high = key
        if diagonal:
            causal = kpos + j * IDX_TILE <= qpos
            low = jnp.where(causal, key, jnp.int32(INT_MIN))
            high = jnp.where(causal, key, int_max)
        key_ref[tile(j), :] = low
        kmax = jnp.maximum(kmax, _fold_rows(low, SUBLANES, jnp.maximum))
        kmin = jnp.minimum(kmin, _fold_rows(high, SUBLANES, jnp.minimum))
        return kmax, kmin

    assert IDX_TILE >= Q_BLOCK
    extremes = lax.fori_loop(0, n_idx - 1, functools.partial(idx_body, diagonal=False),
                             (jnp.full((SUBLANES, Q_BLOCK), INT_MIN, jnp.int32),
                              jnp.full((SUBLANES, Q_BLOCK), 2 ** 31 - 1, jnp.int32)))
    kmax, kmin = idx_body(n_idx - 1, extremes, diagonal=True)
    kmax = _reduce_rows(kmax, jnp.maximum)
    kmin = _reduce_rows(kmin, jnp.minimum)

    def count(pred):
        def body(j, cnt):
            m = jnp.where(pred(key_ref[tile(j), :], kpos + j * IDX_TILE), 1.0, 0.0)
            return cnt + _fold_rows(m, SUBLANES)
        cnt = lax.fori_loop(0, n_idx, body, jnp.zeros((SUBLANES, Q_BLOCK), F32))
        return jnp.sum(cnt, axis=0, keepdims=True)

    n_valid = (qpos[0:1, :] + 1).astype(F32)
    few = n_valid <= kf
    flat = kmin == kmax
    start = few | flat
    thr0 = jnp.where(few, jnp.int32(INT_MIN + 1), kmin)
    state = (jnp.where(start, thr0, kmin), jnp.where(start, thr0, kmax + 1), n_valid)

    def search_pass(it, st):
        lo, hi, c_lo = st
        f_lo = pltpu.bitcast(_sortable(lo), F32)
        f_hi = pltpu.bitcast(_sortable(hi - 1), F32)
        piv = _sortable(pltpu.bitcast(0.5 * f_lo + 0.5 * f_hi, jnp.int32))
        piv = jnp.where(it % 4 == 3, lo + lax.shift_right_logical(hi - lo, 1), piv)
        piv = jnp.where(it == 0, jnp.int32(1), piv)
        piv = jnp.where(it == 1, jnp.where(hi == 1, jnp.int32(-1), piv), piv)
        piv = jnp.minimum(jnp.maximum(piv, lo + 1), hi - 1)
        c = count(lambda kt, pos: kt >= piv)
        live = lo != hi
        ge = c >= kf
        hit = c == kf
        lo_n = jnp.where(live & ge, piv, lo)
        c_lo_n = jnp.where(live & ge, c, c_lo)
        hi_n = jnp.where(live & jnp.logical_not(ge), piv, hi)
        hi_n = jnp.where(live & (hit | (hi_n - lo_n == 1)), lo_n, hi_n)
        return lo_n, hi_n, c_lo_n

    state = lax.fori_loop(0, SEARCH_PASSES, search_pass, state)

    def pending(st):
        return jnp.max(jnp.where(st[0] != st[1], 1.0, 0.0))

    def more_cond(carry):
        return jnp.logical_and(carry[1] > 0.0, carry[0] < 1024)

    def more_body(carry):
        st = search_pass(carry[0] + 1, search_pass(carry[0], carry[2:]))
        return (carry[0] + 2, pending(st)) + st

    thr, _, c_ge = lax.while_loop(more_cond, more_body, (jnp.int32(SEARCH_PASSES), pending(state)) + state)[2:]
    tied = jnp.max(c_ge) > kf

    def write_sel(pred):
        def body(j, carry):
            sel = pred(key_ref[tile(j), :], kpos + j * IDX_TILE)
            sel_ref[tile(j), :] = jnp.where(sel, 1.0, 0.0).astype(BF16)
            return carry
        lax.fori_loop(0, n_idx, body, 0)

    @pl.when(jnp.logical_not(tied))
    def _():
        write_sel(lambda kt, pos: kt >= thr)

    @pl.when(tied)
    def _():
        need = kf - count(lambda kt, pos: kt > thr)
        nbits = max(1, int(np.ceil(np.log2(seq))))

        def pos_body(p, lim):
            cand = lim + lax.shift_left(jnp.int32(1), nbits - 1 - p)
            c = count(lambda kt, pos: jnp.where(kt == thr, pos, jnp.int32(seq)) < cand)
            return jnp.where(c < need, cand, lim)

        lim = lax.fori_loop(0, nbits, pos_body, jnp.zeros((1, Q_BLOCK), jnp.int32))
        lim = jnp.where(c_ge > kf, lim, jnp.int32(seq))
        write_sel(lambda kt, pos: (kt > thr) | ((kt == thr) & (pos <= lim)))

    rows = Q_PER_KV * Q_BLOCK
    pk = 2 * SUBLANES
    qgs = [jnp.concatenate([q_ref[:, (Q_PER_KV * g + e) * LANES:(Q_PER_KV * g + e + 1) * LANES]
                            for e in range(Q_PER_KV)], axis=0) for g in range(N_KV_HEADS)]
    ones_rows = jnp.ones((pk, ATT_TILE), BF16)

    def logits(j, g):
        ks = pl.ds(pl.multiple_of(j * ATT_TILE, ATT_TILE), ATT_TILE)
        return lax.dot_general(k_ref[ks, g * LANES:(g + 1) * LANES], qgs[g], _NT, preferred_element_type=F32)

    def weighted_v(j, g, p):
        ks = pl.ds(pl.multiple_of(j * ATT_TILE, ATT_TILE), ATT_TILE)
        sel = jnp.concatenate([sel_ref[ks, :]] * Q_PER_KV, axis=1)
        v_aug = jnp.concatenate([vt_ref[g * LANES:(g + 1) * LANES, ks], ones_rows], axis=0)
        return _dot(v_aug, p.astype(BF16) * sel)

    shifts = [jnp.max(lax.dot_general(k_ref[0:SHIFT_KEYS, g * LANES:(g + 1) * LANES], qgs[g], _NT,
                                      preferred_element_type=F32), axis=0, keepdims=True)
              for g in range(N_KV_HEADS)]

    def fast_body(j, accs):
        return tuple(accs[g] + weighted_v(j, g, jnp.exp2(logits(j, g) - shifts[g])) for g in range(N_KV_HEADS))

    accs = lax.fori_loop(0, n_att, fast_body, (jnp.zeros((LANES + pk, rows), F32),) * N_KV_HEADS)
    lo_ok, hi_ok = 2.0 ** -60, 2.0 ** 60
    worst = [jnp.max(jnp.where((a[LANES:LANES + 1] >= lo_ok) & (a[LANES:LANES + 1] <= hi_ok), 0.0, 1.0))
             + jnp.max(jnp.where(jnp.abs(a[0:LANES]) < jnp.inf, 0.0, 1.0)) for a in accs]
    redo = (worst[0] + worst[1]) > 0.0

    def store(g, acc):
        o_t = acc[0:LANES] / acc[LANES:LANES + 1]
        for e in range(Q_PER_KV):
            col = (Q_PER_KV * g + e) * LANES
            o_ref[:, col:col + LANES] = o_t[:, e * Q_BLOCK:(e + 1) * Q_BLOCK].T.astype(BF16)

    @pl.when(jnp.logical_not(redo))
    def _():
        for g in range(N_KV_HEADS):
            store(g, accs[g])

    @pl.when(redo)
    def _():
        def safe_body(j, carry):
            ks = pl.ds(pl.multiple_of(j * ATT_TILE, ATT_TILE), ATT_TILE)
            sel = jnp.concatenate([sel_ref[ks, :]] * Q_PER_KV, axis=1).astype(F32)
            out = []
            for g in range(N_KV_HEADS):
                m, acc = carry[g]
                s = logits(j, g) + (sel - 1.0) * (-MASKED)
                m_new = jnp.maximum(m, jnp.max(s, axis=0, keepdims=True))
                out.append((m_new, jnp.exp2(m - m_new) * acc + weighted_v(j, g, jnp.exp2(s - m_new))))
            return tuple(out)

        init = (jnp.full((1, rows), MASKED, F32), jnp.zeros((LANES + pk, rows), F32))
        res = lax.fori_loop(0, n_att, safe_body, (init,) * N_KV_HEADS)
        for g in range(N_KV_HEADS):
            store(g, res[g][1])


def _dsa(q, k, vt, qi, kiwf, kiwb, bsz, seq):
    assert seq % IDX_TILE == 0 and seq % ATT_TILE == 0
    nq = seq // Q_BLOCK
    topk = min(TOPK_MAX, seq // 4)
    qrow = lambda w: pl.BlockSpec((Q_BLOCK, w), lambda b, i: (b * nq + i, 0))
    full = lambda w: pl.BlockSpec((seq, w), lambda b, i: (b, 0))
    return pl.pallas_call(
        functools.partial(_dsa_kernel, topk=topk, seq=seq),
        grid=(bsz, nq),
        in_specs=[qrow(ATTN_WIDTH), qrow(IDX_HEADS * LANES), qrow(LANES), full(KV_WIDTH),
                  pl.BlockSpec((KV_WIDTH, seq), lambda b, i: (0, b)), full(LANES)],
        out_specs=qrow(ATTN_WIDTH),
        out_shape=jax.ShapeDtypeStruct((bsz * seq, ATTN_WIDTH), BF16),
        scratch_shapes=[pltpu.VMEM((seq, Q_BLOCK), jnp.int32), pltpu.VMEM((seq, Q_BLOCK), BF16)],
        compiler_params=_params("parallel", "arbitrary"),
        name="dsa_attention",
    )(q, qi, kiwf, k, vt, kiwb)


def _split3(x):
    hi = x.astype(BF16)
    r = x - hi.astype(F32)
    mid = r.astype(BF16)
    return hi, mid, (r - mid.astype(F32)).astype(BF16)


def _ssd_kernel(zs_ref, xa_ref, dt_ref, dtb_ref, alog_ref, dexp_ref, nw_ref, e_ref, o_ref, st_ref):
    L = SSD_CHUNK

    @pl.when(pl.program_id(1) == 0)
    def _():
        st_ref[...] = jnp.zeros_like(st_ref)

    dtv = dt_ref[...] + dtb_ref[...]
    dt = jnp.maximum(dtv, 0.0) + jnp.log1p(jnp.exp(-jnp.abs(dtv)))
    adt = dt * (-jnp.exp(alog_ref[...]))
    ri = lax.broadcasted_iota(jnp.int32, (L, L), 0)
    ci = lax.broadcasted_iota(jnp.int32, (L, L), 1)
    tril = ci <= ri
    tril_b = jnp.where(tril, 1.0, 0.0).astype(BF16)
    acs = sum(_dot(tril_b, t) for t in _split3(adt))
    last = acs[L - 1:L, :]
    acs_t = acs.T
    dt_t = dt.T
    hi = jnp.concatenate([dt * jnp.exp(last - acs), jnp.exp(acs)], axis=0)
    stack = (hi.astype(BF16), (hi - hi.astype(BF16).astype(F32)).astype(BF16))
    first_half = ci < SSD_HEAD_DIM
    gw = SSD_GROUP_WIDTH

    for g in range(SSD_GROUPS):
        gs = slice(g * gw, (g + 1) * gw)
        b_g = xa_ref[:, SSD_INNER + g * SSD_STATE:SSD_INNER + (g + 1) * SSD_STATE]
        c_g = xa_ref[:, SSD_INNER + SSD_GN + g * SSD_STATE:SSD_INNER + SSD_GN + (g + 1) * SSD_STATE]
        b_b, c_b = b_g.astype(BF16), c_g.astype(BF16)
        cb = lax.dot_general(c_b, b_b, _NT, preferred_element_type=F32)
        b_t = b_g.T.astype(BF16)
        ex = sum(_dot(t, e_ref[:, gs]) for t in stack)
        dtdec, eacs = ex[0:L], ex[L:2 * L]
        xs = xa_ref[:, gs]
        s_in = st_ref[:, gs]
        y = _dot(c_b, s_in.astype(BF16)) * eacs + xs * dexp_ref[:, gs]
        st_ref[:, gs] = s_in * eacs[L - 1:L, :] + _dot(b_t, (xs * dtdec).astype(BF16))
        ys = []
        for c2 in range(gw // LANES):
            xp = xs[:, c2 * LANES:(c2 + 1) * LANES].astype(BF16)
            pair = []
            for hh in range(LANES // SSD_HEAD_DIM):
                h = g * SSD_HEADS_PER_GROUP + c2 * (LANES // SSD_HEAD_DIM) + hh
                seg = acs[:, h:h + 1] - acs_t[h:h + 1, :]
                lm = jnp.exp(jnp.where(tril, seg, -jnp.inf))
                pair.append(_dot((cb * lm * dt_t[h:h + 1, :]).astype(BF16), xp))
            ys.append(jnp.where(first_half, pair[0], pair[1]))
        y = y + jnp.concatenate(ys, axis=1)
        yz = y * zs_ref[:, gs]
        ms = jnp.mean(yz * yz, axis=1, keepdims=True)
        o_ref[:, gs] = (yz * lax.rsqrt(ms + NORM_EPS) * nw_ref[:, gs]).astype(BF16)


def _ssd(zs, xa, dtp, dtb, alog, dexp, nw, emat, bsz, seq):
    L = SSD_CHUNK
    nc = seq // L
    row = lambda w: pl.BlockSpec((L, w), lambda b, c: (b * nc + c, 0))
    return pl.pallas_call(
        _ssd_kernel,
        grid=(bsz, nc),
        in_specs=[row(SSD_INNER), row(SSD_CONV_DIM), row(LANES),
                  _resident((1, LANES)), _resident((1, LANES)), _resident((1, SSD_INNER)),
                  _resident((1, SSD_INNER)), _resident((LANES, SSD_INNER))],
        out_specs=row(SSD_INNER),
        out_shape=jax.ShapeDtypeStruct((bsz * seq, SSD_INNER), BF16),
        scratch_shapes=[pltpu.VMEM((SSD_STATE, SSD_INNER), F32)],
        compiler_params=_params("parallel", "arbitrary"),
        name="ssd_branch",
    )(zs, xa, dtp, dtb, alog, dexp, nw, emat)


def _merge_kernel(x_ref, a_ref, b_ref, nw_ref, wg_ref, wpa_ref, wps_ref, wo_ref, o_ref):
    x = x_ref[...]
    h = _rms(x, nw_ref[...]).astype(BF16)
    merged = _sigmoid(_dot(h, wg_ref[:, 0:D_MODEL])) * _dot(a_ref[...], wpa_ref[...])
    merged = merged + _sigmoid(_dot(h, wg_ref[:, D_MODEL:2 * D_MODEL])) * _dot(b_ref[...], wps_ref[...])
    o_ref[...] = x + _dot(merged.astype(BF16), wo_ref[...])


def _merge(x2, a, b, nw, wg, wpa, wps, wo):
    tokens = x2.shape[0]
    tm = MERGE_ROWS
    row = lambda w: pl.BlockSpec((tm, w), lambda i: (i, 0))
    return pl.pallas_call(
        _merge_kernel,
        grid=(tokens // tm,),
        in_specs=[row(D_MODEL), row(ATTN_WIDTH), row(SSD_INNER), _resident((1, D_MODEL)),
                  _resident((D_MODEL, 2 * D_MODEL)), _resident((ATTN_WIDTH, D_MODEL)),
                  _resident((SSD_INNER, D_MODEL)), _resident((D_MODEL, D_MODEL))],
        out_specs=row(D_MODEL),
        out_shape=jax.ShapeDtypeStruct((tokens, D_MODEL), F32),
        compiler_params=_params("parallel"),
        name="merge_out",
    )(x2, a, b, nw, wg, wpa, wps, wo)


def _ffn_kernel(x_ref, nw_ref, wup_ref, cw_ref, cb_ref, wdn_ref, fnw_ref, o_ref, act_ref, carry_ref, *, final_norm):
    tm = FFN_ROWS
    tail = SUBLANES
    nf = FFN_DIM // FFN_TILE

    @pl.when(pl.program_id(1) == 0)
    def _():
        carry_ref[...] = jnp.zeros_like(carry_ref)

    x = x_ref[...]
    h = _rms(x, nw_ref[...]).astype(BF16)
    for f in range(nf):
        parts = []
        for part in range(2):
            col = part * FFN_DIM + f * FFN_TILE
            cs = slice(col, col + FFN_TILE)
            ct = part * nf + f
            u = _dot(h, wup_ref[:, cs])
            ue = jnp.concatenate([carry_ref[ct], u], axis=0)
            cv = cb_ref[:, cs] + cw_ref[FFN_CONV - 1:FFN_CONV, cs] * u
            for j in range(FFN_CONV - 1):
                back = FFN_CONV - 1 - j
                cv = cv + cw_ref[j:j + 1, cs] * pltpu.roll(ue, back, 0)[tail:tail + tm]
            carry_ref[ct] = u[tm - tail:tm, :]
            parts.append(cv)
        act_ref[:, f * FFN_TILE:(f + 1) * FFN_TILE] = (parts[0] * _sigmoid(parts[0]) * parts[1]).astype(BF16)
    y = x + _dot(act_ref[...], wdn_ref[...])
    if final_norm:
        y = _rms(y, fnw_ref[...])
    o_ref[...] = y


def _ffn(x2, nw, wup, cw, cb, wdn, fnw, bsz, seq, final_norm):
    tm = FFN_ROWS
    ns = seq // tm
    nf = FFN_DIM // FFN_TILE
    row = pl.BlockSpec((tm, D_MODEL), lambda b, s: (b * ns + s, 0))
    return pl.pallas_call(
        functools.partial(_ffn_kernel, final_norm=final_norm),
        grid=(bsz, ns),
        in_specs=[row, _resident((1, D_MODEL)), _resident((D_MODEL, 2 * FFN_DIM)),
                  _resident((FFN_CONV, 2 * FFN_DIM)), _resident((1, 2 * FFN_DIM)),
                  _resident((FFN_DIM, D_MODEL)), _resident((1, D_MODEL))],
        out_specs=row,
        out_shape=jax.ShapeDtypeStruct((bsz * seq, D_MODEL), F32),
        scratch_shapes=[pltpu.VMEM((tm, FFN_DIM), BF16),
                        pltpu.VMEM((2 * nf, SUBLANES, FFN_TILE), F32)],
        compiler_params=_params("parallel", "arbitrary"),
        name="conv_glu_ffn",
    )(x2, nw, wup, cw, cb, wdn, fnw)


def _split_w_in(w_in):
    o = np.cumsum([0, ATTN_WIDTH, KV_WIDTH, KV_WIDTH, IDX_HEADS * IDX_DIM, IDX_DIM, IDX_HEADS,
                   SSD_INNER, SSD_CONV_DIM, SSD_HEADS, D_MODEL, D_MODEL]).tolist()
    seg = lambda j: w_in[:, o[j]:o[j + 1]]
    d = w_in.shape[0]
    qi = seg(3).reshape(d, IDX_HEADS, IDX_DIM)
    qi = jnp.pad(qi, ((0, 0), (0, 0), (0, LANES - IDX_DIM))).reshape(d, IDX_HEADS * LANES)
    kiw = jnp.pad(jnp.concatenate([seg(4), seg(5)], axis=1), ((0, 0), (0, LANES - IDX_DIM - IDX_HEADS)))
    w_a = jnp.concatenate([seg(0), seg(1), qi, kiw], axis=1).astype(BF16)
    w_vt = seg(2).T.astype(BF16)
    dt = jnp.pad(seg(8), ((0, 0), (0, LANES - SSD_HEADS)))
    w_b = jnp.concatenate([seg(6), seg(7), dt], axis=1).astype(BF16)
    w_g = jnp.concatenate([seg(9), seg(10)], axis=1).astype(BF16)
    return w_a, w_vt, w_b, w_g


def _lane_pad_row(v):
    return jnp.pad(v, (0, LANES - v.shape[0]))[None, :]


def kernel(x, norm_mix_w, w_in, ssd_conv_w, ssd_conv_b, ssd_dt_bias, ssd_a_log, ssd_d, ssd_norm_w, w_proj_attn, w_proj_ssd, w_out, norm_ffn_w, ffn_w_up, ffn_conv_w, ffn_conv_b, ffn_w_down, norm_final_w):
    bsz, seq, d = x.shape
    depth = w_in.shape[0]
    assert d == D_MODEL and seq % max(PROJ_ROWS_A, FFN_ROWS, MERGE_ROWS, IDX_TILE) == 0
    tabs_a = _rope_tables(seq, HEAD_DIM // ROPE_FRACTION)
    tabs_i = _rope_tables(seq, IDX_DIM // ROPE_FRACTION)
    emat = (jnp.arange(SSD_INNER)[None, :] // SSD_HEAD_DIM == jnp.arange(LANES)[:, None]).astype(BF16)
    x2 = x.reshape(bsz * seq, d)
    for l in range(depth):
        w_a, w_vt, w_b, w_g = _split_w_in(w_in[l])
        nw = norm_mix_w[l][None, :]
        q, k, vt, qi, kiwf, kiwb = _proj_attn(x2, nw, w_a, w_vt, tabs_a, tabs_i, seq)
        a = _dsa(q, k, vt, qi, kiwf, kiwb, bsz, seq)
        zs, xa, dtp = _proj_ssd(x2, nw, w_b, ssd_conv_w[l], ssd_conv_b[l][None, :], bsz, seq)
        b = _ssd(zs, xa, dtp, _lane_pad_row(ssd_dt_bias[l]), _lane_pad_row(ssd_a_log[l]),
                 jnp.repeat(ssd_d[l], SSD_HEAD_DIM)[None, :], ssd_norm_w[l][None, :], emat, bsz, seq)
        x2 = _merge(x2, a, b, nw, w_g, w_proj_attn[l].astype(BF16), w_proj_ssd[l].astype(BF16),
                    w_out[l].astype(BF16))
        x2 = _ffn(x2, norm_ffn_w[l][None, :], ffn_w_up[l].astype(BF16), ffn_conv_w[l], ffn_conv_b[l][None, :],
                  ffn_w_down[l].astype(BF16), norm_final_w[None, :], bsz, seq, final_norm=(l == depth - 1))
    return x2.reshape(bsz, seq, d)
```

```python
import functools

import numpy as np
import jax
import jax.numpy as jnp
from jax import lax
from jax.experimental import pallas as pl
from jax.experimental.pallas import tpu as pltpu

D_MODEL = 1024
N_HEADS = 8
HEAD_DIM = 128
N_KV_HEADS = 2
Q_PER_KV = N_HEADS // N_KV_HEADS
ATTN_WIDTH = N_HEADS * HEAD_DIM
KV_WIDTH = N_KV_HEADS * HEAD_DIM
IDX_HEADS = 8
IDX_DIM = 64
TOPK_MAX = 256
ROPE_THETA = 500000.0
ROPE_FRACTION = 4
SSD_INNER = 2 * D_MODEL
SSD_HEAD_DIM = 64
SSD_HEADS = SSD_INNER // SSD_HEAD_DIM
SSD_GROUPS = 4
SSD_HEADS_PER_GROUP = SSD_HEADS // SSD_GROUPS
SSD_STATE = 128
SSD_CONV = 4
SSD_CHUNK = 128
SSD_GN = SSD_GROUPS * SSD_STATE
SSD_CONV_DIM = SSD_INNER + 2 * SSD_GN
SSD_GROUP_WIDTH = SSD_INNER // SSD_GROUPS
FFN_DIM = 2816
FFN_CONV = 3
NORM_EPS = 1e-6

LANES = 128
SUBLANES = 8
VMEM_LIMIT = 56 * 1024 * 1024
Q_BLOCK = 256
IDX_TILE = 512
ATT_TILE = 512
SEARCH_PASSES = 18
COARSE_PASSES = 8
SHIFT_KEYS = 128
PROJ_ROWS_A = 512
PROJ_ROWS_B = 256
MERGE_ROWS = 512
FFN_ROWS = 512
FFN_TILE = 256
MASKED = -1e30
LOG2_E = 1.4426950408889634
INT_MIN = -(2 ** 31)

F32 = jnp.float32
BF16 = jnp.bfloat16
_NT = (((1,), (1,)), ((), ()))
_HI = lax.Precision.HIGHEST


def _params(*sem):
    return pltpu.CompilerParams(dimension_semantics=sem, vmem_limit_bytes=VMEM_LIMIT)


def _resident(shape):
    nd = len(shape)
    return pl.BlockSpec(shape, lambda *_: (0,) * nd, pipeline_mode=pl.Buffered(1))


def _rms(x, w):
    return x * lax.rsqrt(jnp.mean(x * x, axis=-1, keepdims=True) + NORM_EPS) * w


def _sigmoid(x):
    return 1.0 / (1.0 + jnp.exp(-x))


def _dot(a, b):
    return jnp.dot(a, b, preferred_element_type=F32)


def _fold_rows(x, rows, op=jnp.add):
    parts = [x[r:r + rows] for r in range(0, x.shape[0], rows)]
    while len(parts) > 1:
        parts = [op(parts[a], parts[a + 1]) for a in range(0, len(parts) - 1, 2)] + parts[len(parts) & ~1:]
    return parts[0]


def _rope_tables(seq, rot_dim):
    half = rot_dim // 2
    inv = ROPE_THETA ** (-jnp.arange(0, rot_dim, 2, dtype=F32) / rot_dim)
    ang = jnp.arange(seq, dtype=F32)[:, None] * inv[None, :]
    cos, sin = jnp.cos(ang), jnp.sin(ang)
    pad = LANES - rot_dim
    c = jnp.concatenate([cos, cos, jnp.ones((seq, pad), F32)], axis=1)
    s1 = jnp.concatenate([-sin, jnp.zeros((seq, half + pad), F32)], axis=1)
    s2 = jnp.concatenate([jnp.zeros((seq, half), F32), sin, jnp.zeros((seq, pad), F32)], axis=1)
    return c, s1, s2


def _rope(t, c, s1, s2, half):
    return t * c + pltpu.roll(t, LANES - half, 1) * s1 + pltpu.roll(t, half, 1) * s2


_A_Q = 0
_A_K = _A_Q + ATTN_WIDTH
_A_QI = _A_K + KV_WIDTH
_A_KIW = _A_QI + IDX_HEADS * LANES
_A_COLS = _A_KIW + LANES


def _proj_attn_kernel(x_ref, nw_ref, w_ref, wvt_ref, ca_ref, sa1_ref, sa2_ref, ci_ref, si1_ref, si2_ref,
                      q_ref, k_ref, vt_ref, qi_ref, kiwf_ref, kiwb_ref):
    h = _rms(x_ref[...], nw_ref[...]).astype(BF16)
    ca, sa1, sa2 = ca_ref[...], sa1_ref[...], sa2_ref[...]
    ci, si1, si2 = ci_ref[...], si1_ref[...], si2_ref[...]
    ha = HEAD_DIM // ROPE_FRACTION // 2
    hi = IDX_DIM // ROPE_FRACTION // 2
    scale = HEAD_DIM ** -0.5 * LOG2_E
    t = _dot(h, w_ref[:, _A_Q:_A_Q + ATTN_WIDTH])
    for j in range(N_HEADS):
        js = slice(j * LANES, (j + 1) * LANES)
        q_ref[:, js] = (_rope(t[:, js], ca, sa1, sa2, ha) * scale).astype(BF16)
    t = _dot(h, w_ref[:, _A_K:_A_K + KV_WIDTH])
    for j in range(N_KV_HEADS):
        js = slice(j * LANES, (j + 1) * LANES)
        k_ref[:, js] = _rope(t[:, js], ca, sa1, sa2, ha).astype(BF16)
    vt_ref[...] = lax.dot_general(wvt_ref[...], h, _NT, preferred_element_type=F32).astype(BF16)
    t = _dot(h, w_ref[:, _A_QI:_A_QI + IDX_HEADS * LANES])
    for j in range(IDX_HEADS):
        js = slice(j * LANES, (j + 1) * LANES)
        qi_ref[:, js] = _rope(t[:, js], ci, si1, si2, hi).astype(BF16)
    t = _rope(_dot(h, w_ref[:, _A_KIW:_A_KIW + LANES]), ci, si1, si2, hi)
    kiwf_ref[...] = t
    kiwb_ref[...] = t.astype(BF16)


def _proj_attn(x2, nw, w_a, w_vt, tabs_a, tabs_i, seq):
    tokens = x2.shape[0]
    tm = PROJ_ROWS_A
    per_seq = seq // tm
    row = lambda w: pl.BlockSpec((tm, w), lambda i: (i, 0))
    tab = pl.BlockSpec((tm, LANES), lambda i: (i % per_seq, 0))
    sds = jax.ShapeDtypeStruct
    return pl.pallas_call(
        _proj_attn_kernel,
        grid=(tokens // tm,),
        in_specs=[row(D_MODEL), _resident((1, D_MODEL)), _resident((D_MODEL, _A_COLS)),
                  _resident((KV_WIDTH, D_MODEL))] + [tab] * 6,
        out_specs=[row(ATTN_WIDTH), row(KV_WIDTH), pl.BlockSpec((KV_WIDTH, tm), lambda i: (0, i)),
                   row(IDX_HEADS * LANES), row(LANES), row(LANES)],
        out_shape=[sds((tokens, ATTN_WIDTH), BF16), sds((tokens, KV_WIDTH), BF16), sds((KV_WIDTH, tokens), BF16),
                   sds((tokens, IDX_HEADS * LANES), BF16), sds((tokens, LANES), F32), sds((tokens, LANES), BF16)],
        compiler_params=_params("parallel"),
        name="proj_attn",
    )(x2, nw, w_a, w_vt, *tabs_a, *tabs_i)


_B_Z = 0
_B_XBC = _B_Z + SSD_INNER
_B_DT = _B_XBC + SSD_CONV_DIM
_B_COLS = _B_DT + LANES


def _proj_ssd_kernel(x_ref, nw_ref, w_ref, cw_ref, cb_ref, zs_ref, xa_ref, dt_ref, ext_ref):
    tm = PROJ_ROWS_B
    tail = SUBLANES

    @pl.when(pl.program_id(1) == 0)
    def _():
        ext_ref[0:tail, :] = jnp.zeros((tail, SSD_CONV_DIM), F32)

    h = _rms(x_ref[...], nw_ref[...]).astype(BF16)
    z = _dot(h, w_ref[:, _B_Z:_B_Z + SSD_INNER])
    zs_ref[...] = z * _sigmoid(z)
    dt_ref[...] = _dot(h, w_ref[:, _B_DT:_B_DT + LANES])
    cstep = 512
    for c0 in range(0, SSD_CONV_DIM, cstep):
        cs = slice(c0, c0 + cstep)
        ext_ref[tail:tail + tm, cs] = _dot(h, w_ref[:, _B_XBC + c0:_B_XBC + c0 + cstep])
        ue = ext_ref[:, cs]
        u = cb_ref[:, cs] + cw_ref[SSD_CONV - 1:SSD_CONV, cs] * ue[tail:tail + tm]
        for j in range(SSD_CONV - 1):
            back = SSD_CONV - 1 - j
            u = u + cw_ref[j:j + 1, cs] * pltpu.roll(ue, back, 0)[tail:tail + tm]
        xa_ref[:, cs] = u * _sigmoid(u)
        ext_ref[0:tail, cs] = ue[tm:tm + tail]


def _proj_ssd(x2, nw, w_b, cw, cb, bsz, seq):
    tm = PROJ_ROWS_B
    ns = seq // tm
    row = lambda w: pl.BlockSpec((tm, w), lambda b, s: (b * ns + s, 0))
    widths = (SSD_INNER, SSD_CONV_DIM, LANES)
    return pl.pallas_call(
        _proj_ssd_kernel,
        grid=(bsz, ns),
        in_specs=[row(D_MODEL), _resident((1, D_MODEL)), _resident((D_MODEL, _B_COLS)),
                  _resident((SSD_CONV, SSD_CONV_DIM)), _resident((1, SSD_CONV_DIM))],
        out_specs=[row(w) for w in widths],
        out_shape=[jax.ShapeDtypeStruct((bsz * seq, w), F32) for w in widths],
        scratch_shapes=[pltpu.VMEM((tm + SUBLANES, SSD_CONV_DIM), F32)],
        compiler_params=_params("parallel", "arbitrary"),
        name="proj_ssd",
    )(x2, nw, w_b, cw, cb)


def _sortable(x):
    return x ^ ((x >> 31) & jnp.int32(0x7FFFFFFF))


def _reduce_rows(x, op):
    for sh in (4, 2, 1):
        x = op(x, pltpu.roll(x, sh, 0))
    return x[0:1, :]


def _dsa_kernel(q_ref, qi_ref, wq_ref, k_ref, vt_ref, kib_ref, o_ref, key_ref, floor_ref, sel_ref, *, topk, seq):
    i = pl.program_id(1)
    n_idx = (i + IDX_TILE // Q_BLOCK) // (IDX_TILE // Q_BLOCK)
    n_att = (i + ATT_TILE // Q_BLOCK) // (ATT_TILE // Q_BLOCK)
    kpos = lax.broadcasted_iota(jnp.int32, (IDX_TILE, Q_BLOCK), 0)
    qpos = lax.broadcasted_iota(jnp.int32, (IDX_TILE, Q_BLOCK), 1) + i * Q_BLOCK
    w_t = wq_ref[...].T
    qi_all = jnp.concatenate([qi_ref[:, h * LANES:(h + 1) * LANES] for h in range(IDX_HEADS)], axis=0)
    kf = float(topk)
    int_max = jnp.int32(2 ** 31 - 1)

    def tile(j):
        return pl.ds(pl.multiple_of(j * IDX_TILE, IDX_TILE), IDX_TILE)

    def idx_body(j, carry, diagonal):
        kmax, kmin = carry
        s = lax.dot_general(kib_ref[tile(j), :], qi_all, _NT, preferred_element_type=F32)
        acc = jnp.zeros((IDX_TILE, Q_BLOCK), F32)
        for h in range(IDX_HEADS):
            acc = acc + w_t[IDX_DIM + h:IDX_DIM + h + 1, :] * jnp.maximum(s[:, h * Q_BLOCK:(h + 1) * Q_BLOCK], 0.0)
        bits = pltpu.bitcast(acc, jnp.int32)
        key = _sortable(bits)
        floor = pltpu.bitcast((bits + ((bits >> 31) & jnp.int32(0xFFFF))) & jnp.int32(-0x10000), F32)
        low = high = key
        if diagonal:
            causal = kpos + j * IDX_TILE <= qpos
            low = jnp.where(causal, key, jnp.int32(INT_MIN))
            high = jnp.where(causal, key, int_max)
            floor = jnp.where(causal, floor, -jnp.inf)
        key_ref[tile(j), :] = low
        floor_ref[tile(j), :] = floor.astype(BF16)
        kmax = jnp.maximum(kmax, _fold_rows(low, SUBLANES, jnp.maximum))
        kmin = jnp.minimum(kmin, _fold_rows(high, SUBLANES, jnp.minimum))
        return kmax, kmin

    assert IDX_TILE >= Q_BLOCK
    extremes = lax.fori_loop(0, n_idx - 1, functools.partial(idx_body, diagonal=False),
                             (jnp.full((SUBLANES, Q_BLOCK), INT_MIN, jnp.int32),
                              jnp.full((SUBLANES, Q_BLOCK), 2 ** 31 - 1, jnp.int32)))
    kmax, kmin = idx_body(n_idx - 1, extremes, diagonal=True)
    kmax = _reduce_rows(kmax, jnp.maximum)
    kmin = _reduce_rows(kmin, jnp.minimum)

    def count(pred):
        def body(j, cnt):
            m = jnp.where(pred(key_ref[tile(j), :], kpos + j * IDX_TILE), 1.0, 0.0)
            return cnt + _fold_rows(m, SUBLANES)
        cnt = lax.fori_loop(0, n_idx, body, jnp.zeros((SUBLANES, Q_BLOCK), F32))
        return jnp.sum(cnt, axis=0, keepdims=True)

    n_valid = (qpos[0:1, :] + 1).astype(F32)
    few = n_valid <= kf
    flat = kmin == kmax
    start = few | flat
    thr0 = jnp.where(few, jnp.int32(INT_MIN + 1), kmin)
    state = (jnp.where(start, thr0, kmin), jnp.where(start, thr0, kmax + 1), n_valid)

    def search_pass(it, st):
        lo, hi, c_lo = st
        f_lo = pltpu.bitcast(_sortable(lo), F32)
        f_hi = pltpu.bitcast(_sortable(hi - 1), F32)
        piv = _sortable(pltpu.bitcast(0.5 * f_lo + 0.5 * f_hi, jnp.int32))
        piv = jnp.where(it % 4 == 3, lo + lax.shift_right_logical(hi - lo, 1), piv)
        piv = jnp.where(it == 0, jnp.int32(1), piv)
        piv = jnp.where(it == 1, jnp.where(hi == 1, jnp.int32(-1), piv), piv)
        piv = jnp.minimum(jnp.maximum(piv, lo + 1), hi - 1)
        c = count(lambda kt, pos: kt >= piv)
        live = lo != hi
        ge = c >= kf
        hit = c == kf
        lo_n = jnp.where(live & ge, piv, lo)
        c_lo_n = jnp.where(live & ge, c, c_lo)
        hi_n = jnp.where(live & jnp.logical_not(ge), piv, hi)
        hi_n = jnp.where(live & (hit | (hi_n - lo_n == 1)), lo_n, hi_n)
        return lo_n, hi_n, c_lo_n

    def count_coarse(p):
        pk = 2 * SUBLANES
        def body(j, cnt):
            m = jnp.where(floor_ref[tile(j), :] >= p, jnp.bfloat16(1), jnp.bfloat16(0))
            return cnt + _fold_rows(m, pk).astype(F32)
        cnt = lax.fori_loop(0, n_idx, body, jnp.zeros((pk, Q_BLOCK), F32))
        return jnp.sum(cnt, axis=0, keepdims=True)

    def coarse_pass(it, st):
        lo, hi, c_lo = st
        f_lo = pltpu.bitcast(_sortable(lo), F32)
        f_hi = pltpu.bitcast(_sortable(hi - 1), F32)
        p = (0.5 * f_lo + 0.5 * f_hi).astype(BF16)
        p_bits = pltpu.bitcast(p.astype(F32), jnp.int32)
        piv = _sortable(p_bits)
        expo = p_bits & jnp.int32(0x7F800000)
        ok = (piv > lo) & (piv < hi) & (expo != 0) & (expo != jnp.int32(0x7F800000))
        c = count_coarse(p)
        live = (lo != hi) & ok
        ge = c >= kf
        hit = c == kf
        lo_n = jnp.where(live & ge, piv, lo)
        c_lo_n = jnp.where(live & ge, c, c_lo)
        hi_n = jnp.where(live & jnp.logical_not(ge), piv, hi)
        hi_n = jnp.where(live & (hit | (hi_n - lo_n == 1)), lo_n, hi_n)
        return lo_n, hi_n, c_lo_n

    state = lax.fori_loop(0, 2, search_pass, state)
    state = lax.fori_loop(2, 2 + COARSE_PASSES, coarse_pass, state)
    state = lax.fori_loop(2 + COARSE_PASSES, SEARCH_PASSES, search_pass, state)

    def pending(st):
        return jnp.max(jnp.where(st[0] != st[1], 1.0, 0.0))

    def more_cond(carry):
        return jnp.logical_and(carry[1] > 0.0, carry[0] < 1024)

    def more_body(carry):
        st = search_pass(carry[0] + 1, search_pass(carry[0], carry[2:]))
        return (carry[0] + 2, pending(st)) + st

    thr, _, c_ge = lax.while_loop(more_cond, more_body, (jnp.int32(SEARCH_PASSES), pending(state)) + state)[2:]
    tied = jnp.max(c_ge) > kf

    def write_sel(pred):
        def body(j, carry):
            sel = pred(key_ref[tile(j), :], kpos + j * IDX_TILE)
            sel_ref[tile(j), :] = jnp.where(sel, 1.0, 0.0).astype(BF16)
            return carry
        lax.fori_loop(0, n_idx, body, 0)

    @pl.when(jnp.logical_not(tied))
    def _():
        write_sel(lambda kt, pos: kt >= thr)

    @pl.when(tied)
    def _():
        need = kf - count(lambda kt, pos: kt > thr)
        nbits = max(1, int(np.ceil(np.log2(seq))))

        def pos_body(p, lim):
            cand = lim + lax.shift_left(jnp.int32(1), nbits - 1 - p)
            c = count(lambda kt, pos: jnp.where(kt == thr, pos, jnp.int32(seq)) < cand)
            return jnp.where(c < need, cand, lim)

        lim = lax.fori_loop(0, nbits, pos_body, jnp.zeros((1, Q_BLOCK), jnp.int32))
        lim = jnp.where(c_ge > kf, lim, jnp.int32(seq))
        write_sel(lambda kt, pos: (kt > thr) | ((kt == thr) & (pos <= lim)))

    rows = Q_PER_KV * Q_BLOCK
    pk = 2 * SUBLANES
    qgs = [jnp.concatenate([q_ref[:, (Q_PER_KV * g + e) * LANES:(Q_PER_KV * g + e + 1) * LANES]
                            for e in range(Q_PER_KV)], axis=0) for g in range(N_KV_HEADS)]
    ones_rows = jnp.ones((pk, ATT_TILE), BF16)

    def logits(j, g):
        ks = pl.ds(pl.multiple_of(j * ATT_TILE, ATT_TILE), ATT_TILE)
        return lax.dot_general(k_ref[ks, g * LANES:(g + 1) * LANES], qgs[g], _NT, preferred_element_type=F32)

    def weighted_v(j, g, p):
        ks = pl.ds(pl.multiple_of(j * ATT_TILE, ATT_TILE), ATT_TILE)
        sel = jnp.concatenate([sel_ref[ks, :]] * Q_PER_KV, axis=1)
        v_aug = jnp.concatenate([vt_ref[g * LANES:(g + 1) * LANES, ks], ones_rows], axis=0)
        return _dot(v_aug, p.astype(BF16) * sel)

    shifts = [jnp.max(lax.dot_general(k_ref[0:SHIFT_KEYS, g * LANES:(g + 1) * LANES], qgs[g], _NT,
                                      preferred_element_type=F32), axis=0, keepdims=True)
              for g in range(N_KV_HEADS)]

    def fast_body(j, accs):
        return tuple(accs[g] + weighted_v(j, g, jnp.exp2(logits(j, g) - shifts[g])) for g in range(N_KV_HEADS))

    accs = lax.fori_loop(0, n_att, fast_body, (jnp.zeros((LANES + pk, rows), F32),) * N_KV_HEADS)
    lo_ok, hi_ok = 2.0 ** -60, 2.0 ** 60
    worst = [jnp.max(jnp.where((a[LANES:LANES + 1] >= lo_ok) & (a[LANES:LANES + 1] <= hi_ok), 0.0, 1.0))
             + jnp.max(jnp.where(jnp.abs(a[0:LANES]) < jnp.inf, 0.0, 1.0)) for a in accs]
    redo = (worst[0] + worst[1]) > 0.0

    def store(g, acc):
        o_t = acc[0:LANES] / acc[LANES:LANES + 1]
        for e in range(Q_PER_KV):
            col = (Q_PER_KV * g + e) * LANES
            o_ref[:, col:col + LANES] = o_t[:, e * Q_BLOCK:(e + 1) * Q_BLOCK].T.astype(BF16)

    @pl.when(jnp.logical_not(redo))
    def _():
        for g in range(N_KV_HEADS):
            store(g, accs[g])

    @pl.when(redo)
    def _():
        def safe_body(j, carry):
            ks = pl.ds(pl.multiple_of(j * ATT_TILE, ATT_TILE), ATT_TILE)
            sel = jnp.concatenate([sel_ref[ks, :]] * Q_PER_KV, axis=1).astype(F32)
            out = []
            for g in range(N_KV_HEADS):
                m, acc = carry[g]
                s = logits(j, g) + (sel - 1.0) * (-MASKED)
                m_new = jnp.maximum(m, jnp.max(s, axis=0, keepdims=True))
                out.append((m_new, jnp.exp2(m - m_new) * acc + weighted_v(j, g, jnp.exp2(s - m_new))))
            return tuple(out)

        init = (jnp.full((1, rows), MASKED, F32), jnp.zeros((LANES + pk, rows), F32))
        res = lax.fori_loop(0, n_att, safe_body, (init,) * N_KV_HEADS)
        for g in range(N_KV_HEADS):
            store(g, res[g][1])


def _dsa(q, k, vt, qi, kiwf, kiwb, bsz, seq):
    assert seq % IDX_TILE == 0 and seq % ATT_TILE == 0
    nq = seq // Q_BLOCK
    topk = min(TOPK_MAX, seq // 4)
    qrow = lambda w: pl.BlockSpec((Q_BLOCK, w), lambda b, i: (b * nq + i, 0))
    full = lambda w: pl.BlockSpec((seq, w), lambda b, i: (b, 0))
    return pl.pallas_call(
        functools.partial(_dsa_kernel, topk=topk, seq=seq),
        grid=(bsz, nq),
        in_specs=[qrow(ATTN_WIDTH), qrow(IDX_HEADS * LANES), qrow(LANES), full(KV_WIDTH),
                  pl.BlockSpec((KV_WIDTH, seq), lambda b, i: (0, b)), full(LANES)],
        out_specs=qrow(ATTN_WIDTH),
        out_shape=jax.ShapeDtypeStruct((bsz * seq, ATTN_WIDTH), BF16),
        scratch_shapes=[pltpu.VMEM((seq, Q_BLOCK), jnp.int32), pltpu.VMEM((seq, Q_BLOCK), BF16),
                        pltpu.VMEM((seq, Q_BLOCK), BF16)],
        compiler_params=_params("parallel", "arbitrary"),
        name="dsa_attention",
    )(q, qi, kiwf, k, vt, kiwb)


def _split3(x):
    hi = x.astype(BF16)
    r = x - hi.astype(F32)
    mid = r.astype(BF16)
    return hi, mid, (r - mid.astype(F32)).astype(BF16)


def _ssd_kernel(zs_ref, xa_ref, dt_ref, dtb_ref, alog_ref, dexp_ref, nw_ref, e_ref, o_ref, st_ref):
    L = SSD_CHUNK

    @pl.when(pl.program_id(1) == 0)
    def _():
        st_ref[...] = jnp.zeros_like(st_ref)

    dtv = dt_ref[...] + dtb_ref[...]
    dt = jnp.maximum(dtv, 0.0) + jnp.log1p(jnp.exp(-jnp.abs(dtv)))
    adt = dt * (-jnp.exp(alog_ref[...]))
    ri = lax.broadcasted_iota(jnp.int32, (L, L), 0)
    ci = lax.broadcasted_iota(jnp.int32, (L, L), 1)
    tril = ci <= ri
    tril_b = jnp.where(tril, 1.0, 0.0).astype(BF16)
    acs = sum(_dot(tril_b, t) for t in _split3(adt))
    last = acs[L - 1:L, :]
    acs_t = acs.T
    dt_t = dt.T
    hi = jnp.concatenate([dt * jnp.exp(last - acs), jnp.exp(acs)], axis=0)
    stack = (hi.astype(BF16), (hi - hi.astype(BF16).astype(F32)).astype(BF16))
    first_half = ci < SSD_HEAD_DIM
    gw = SSD_GROUP_WIDTH

    for g in range(SSD_GROUPS):
        gs = slice(g * gw, (g + 1) * gw)
        b_g = xa_ref[:, SSD_INNER + g * SSD_STATE:SSD_INNER + (g + 1) * SSD_STATE]
        c_g = xa_ref[:, SSD_INNER + SSD_GN + g * SSD_STATE:SSD_INNER + SSD_GN + (g + 1) * SSD_STATE]
        b_b, c_b = b_g.astype(BF16), c_g.astype(BF16)
        cb = lax.dot_general(c_b, b_b, _NT, preferred_element_type=F32)
        b_t = b_g.T.astype(BF16)
        ex = sum(_dot(t, e_ref[:, gs]) for t in stack)
        dtdec, eacs = ex[0:L], ex[L:2 * L]
        xs = xa_ref[:, gs]
        s_in = st_ref[:, gs]
        y = _dot(c_b, s_in.astype(BF16)) * eacs + xs * dexp_ref[:, gs]
        st_ref[:, gs] = s_in * eacs[L - 1:L, :] + _dot(b_t, (xs * dtdec).astype(BF16))
        ys = []
        for c2 in range(gw // LANES):
            xp = xs[:, c2 * LANES:(c2 + 1) * LANES].astype(BF16)
            pair = []
            for hh in range(LANES // SSD_HEAD_DIM):
                h = g * SSD_HEADS_PER_GROUP + c2 * (LANES // SSD_HEAD_DIM) + hh
                seg = acs[:, h:h + 1] - acs_t[h:h + 1, :]
                lm = jnp.exp(jnp.where(tril, seg, -jnp.inf))
                pair.append(_dot((cb * lm * dt_t[h:h + 1, :]).astype(BF16), xp))
            ys.append(jnp.where(first_half, pair[0], pair[1]))
        y = y + jnp.concatenate(ys, axis=1)
        yz = y * zs_ref[:, gs]
        ms = jnp.mean(yz * yz, axis=1, keepdims=True)
        o_ref[:, gs] = (yz * lax.rsqrt(ms + NORM_EPS) * nw_ref[:, gs]).astype(BF16)


def _ssd(zs, xa, dtp, dtb, alog, dexp, nw, emat, bsz, seq):
    L = SSD_CHUNK
    nc = seq // L
    row = lambda w: pl.BlockSpec((L, w), lambda b, c: (b * nc + c, 0))
    return pl.pallas_call(
        _ssd_kernel,
        grid=(bsz, nc),
        in_specs=[row(SSD_INNER), row(SSD_CONV_DIM), row(LANES),
                  _resident((1, LANES)), _resident((1, LANES)), _resident((1, SSD_INNER)),
                  _resident((1, SSD_INNER)), _resident((LANES, SSD_INNER))],
        out_specs=row(SSD_INNER),
        out_shape=jax.ShapeDtypeStruct((bsz * seq, SSD_INNER), BF16),
        scratch_shapes=[pltpu.VMEM((SSD_STATE, SSD_INNER), F32)],
        compiler_params=_params("parallel", "arbitrary"),
        name="ssd_branch",
    )(zs, xa, dtp, dtb, alog, dexp, nw, emat)


def _merge_kernel(x_ref, a_ref, b_ref, nw_ref, wg_ref, wpa_ref, wps_ref, wo_ref, o_ref):
    x = x_ref[...]
    h = _rms(x, nw_ref[...]).astype(BF16)
    merged = _sigmoid(_dot(h, wg_ref[:, 0:D_MODEL])) * _dot(a_ref[...], wpa_ref[...])
    merged = merged + _sigmoid(_dot(h, wg_ref[:, D_MODEL:2 * D_MODEL])) * _dot(b_ref[...], wps_ref[...])
    o_ref[...] = x + _dot(merged.astype(BF16), wo_ref[...])


def _merge(x2, a, b, nw, wg, wpa, wps, wo):
    tokens = x2.shape[0]
    tm = MERGE_ROWS
    row = lambda w: pl.BlockSpec((tm, w), lambda i: (i, 0))
    return pl.pallas_call(
        _merge_kernel,
        grid=(tokens // tm,),
        in_specs=[row(D_MODEL), row(ATTN_WIDTH), row(SSD_INNER), _resident((1, D_MODEL)),
                  _resident((D_MODEL, 2 * D_MODEL)), _resident((ATTN_WIDTH, D_MODEL)),
                  _resident((SSD_INNER, D_MODEL)), _resident((D_MODEL, D_MODEL))],
        out_specs=row(D_MODEL),
        out_shape=jax.ShapeDtypeStruct((tokens, D_MODEL), F32),
        compiler_params=_params("parallel"),
        name="merge_out",
    )(x2, a, b, nw, wg, wpa, wps, wo)


def _ffn_kernel(x_ref, nw_ref, wup_ref, cw_ref, cb_ref, wdn_ref, fnw_ref, o_ref, act_ref, carry_ref, *, final_norm):
    tm = FFN_ROWS
    tail = SUBLANES
    nf = FFN_DIM // FFN_TILE

    @pl.when(pl.program_id(1) == 0)
    def _():
        carry_ref[...] = jnp.zeros_like(carry_ref)

    x = x_ref[...]
    h = _rms(x, nw_ref[...]).astype(BF16)
    for f in range(nf):
        parts = []
        for part in range(2):
            col = part * FFN_DIM + f * FFN_TILE
            cs = slice(col, col + FFN_TILE)
            ct = part * nf + f
            u = _dot(h, wup_ref[:, cs])
            ue = jnp.concatenate([carry_ref[ct], u], axis=0)
            cv = cb_ref[:, cs] + cw_ref[FFN_CONV - 1:FFN_CONV, cs] * u
            for j in range(FFN_CONV - 1):
                back = FFN_CONV - 1 - j
                cv = cv + cw_ref[j:j + 1, cs] * pltpu.roll(ue, back, 0)[tail:tail + tm]
            carry_ref[ct] = u[tm - tail:tm, :]
            parts.append(cv)
        act_ref[:, f * FFN_TILE:(f + 1) * FFN_TILE] = (parts[0] * _sigmoid(parts[0]) * parts[1]).astype(BF16)
    y = x + _dot(act_ref[...], wdn_ref[...])
    if final_norm:
        y = _rms(y, fnw_ref[...])
    o_ref[...] = y


def _ffn(x2, nw, wup, cw, cb, wdn, fnw, bsz, seq, final_norm):
    tm = FFN_ROWS
    ns = seq // tm
    nf = FFN_DIM // FFN_TILE
    row = pl.BlockSpec((tm, D_MODEL), lambda b, s: (b * ns + s, 0))
    return pl.pallas_call(
        functools.partial(_ffn_kernel, final_norm=final_norm),
        grid=(bsz, ns),
        in_specs=[row, _resident((1, D_MODEL)), _resident((D_MODEL, 2 * FFN_DIM)),
                  _resident((FFN_CONV, 2 * FFN_DIM)), _resident((1, 2 * FFN_DIM)),
                  _resident((FFN_DIM, D_MODEL)), _resident((1, D_MODEL))],
        out_specs=row,
        out_shape=jax.ShapeDtypeStruct((bsz * seq, D_MODEL), F32),
        scratch_shapes=[pltpu.VMEM((tm, FFN_DIM), BF16),
                        pltpu.VMEM((2 * nf, SUBLANES, FFN_TILE), F32)],
        compiler_params=_params("parallel", "arbitrary"),
        name="conv_glu_ffn",
    )(x2, nw, wup, cw, cb, wdn, fnw)


def _split_w_in(w_in):
    o = np.cumsum([0, ATTN_WIDTH, KV_WIDTH, KV_WIDTH, IDX_HEADS * IDX_DIM, IDX_DIM, IDX_HEADS,
                   SSD_INNER, SSD_CONV_DIM, SSD_HEADS, D_MODEL, D_MODEL]).tolist()
    seg = lambda j: w_in[:, o[j]:o[j + 1]]
    d = w_in.shape[0]
    qi = seg(3).reshape(d, IDX_HEADS, IDX_DIM)
    qi = jnp.pad(qi, ((0, 0), (0, 0), (0, LANES - IDX_DIM))).reshape(d, IDX_HEADS * LANES)
    kiw = jnp.pad(jnp.concatenate([seg(4), seg(5)], axis=1), ((0, 0), (0, LANES - IDX_DIM - IDX_HEADS)))
    w_a = jnp.concatenate([seg(0), seg(1), qi, kiw], axis=1).astype(BF16)
    w_vt = seg(2).T.astype(BF16)
    dt = jnp.pad(seg(8), ((0, 0), (0, LANES - SSD_HEADS)))
    w_b = jnp.concatenate([seg(6), seg(7), dt], axis=1).astype(BF16)
    w_g = jnp.concatenate([seg(9), seg(10)], axis=1).astype(BF16)
    return w_a, w_vt, w_b, w_g


def _lane_pad_row(v):
    return jnp.pad(v, (0, LANES - v.shape[0]))[None, :]


def kernel(x, norm_mix_w, w_in, ssd_conv_w, ssd_conv_b, ssd_dt_bias, ssd_a_log, ssd_d, ssd_norm_w, w_proj_attn, w_proj_ssd, w_out, norm_ffn_w, ffn_w_up, ffn_conv_w, ffn_conv_b, ffn_w_down, norm_final_w):
    bsz, seq, d = x.shape
    depth = w_in.shape[0]
    assert d == D_MODEL and seq % max(PROJ_ROWS_A, FFN_ROWS, MERGE_ROWS, IDX_TILE) == 0
    tabs_a = _rope_tables(seq, HEAD_DIM // ROPE_FRACTION)
    tabs_i = _rope_tables(seq, IDX_DIM // ROPE_FRACTION)
    emat = (jnp.arange(SSD_INNER)[None, :] // SSD_HEAD_DIM == jnp.arange(LANES)[:, None]).astype(BF16)
    x2 = x.reshape(bsz * seq, d)
    for l in range(depth):
        w_a, w_vt, w_b, w_g = _split_w_in(w_in[l])
        nw = norm_mix_w[l][None, :]
        q, k, vt, qi, kiwf, kiwb = _proj_attn(x2, nw, w_a, w_vt, tabs_a, tabs_i, seq)
        a = _dsa(q, k, vt, qi, kiwf, kiwb, bsz, seq)
        zs, xa, dtp = _proj_ssd(x2, nw, w_b, ssd_conv_w[l], ssd_conv_b[l][None, :], bsz, seq)
        b = _ssd(zs, xa, dtp, _lane_pad_row(ssd_dt_bias[l]), _lane_pad_row(ssd_a_log[l]),
                 jnp.repeat(ssd_d[l], SSD_HEAD_DIM)[None, :], ssd_norm_w[l][None, :], emat, bsz, seq)
        x2 = _merge(x2, a, b, nw, w_g, w_proj_attn[l].astype(BF16), w_proj_ssd[l].astype(BF16),
                    w_out[l].astype(BF16))
        x2 = _ffn(x2, norm_ffn_w[l][None, :], ffn_w_up[l].astype(BF16), ffn_conv_w[l], ffn_conv_b[l][None, :],
                  ffn_w_down[l].astype(BF16), norm_final_w[None, :], bsz, seq, final_norm=(l == depth - 1))
    return x2.reshape(bsz, seq, d)
```

```python
import functools

import numpy as np
import jax
import jax.numpy as jnp
from jax import lax
from jax.experimental import pallas as pl
from jax.experimental.pallas import tpu as pltpu

D_MODEL = 1024
N_HEADS = 8
HEAD_DIM = 128
N_KV_HEADS = 2
Q_PER_KV = N_HEADS // N_KV_HEADS
ATTN_WIDTH = N_HEADS * HEAD_DIM
KV_WIDTH = N_KV_HEADS * HEAD_DIM
IDX_HEADS = 8
IDX_DIM = 64
TOPK_MAX = 256
ROPE_THETA = 500000.0
ROPE_FRACTION = 4
SSD_INNER = 2 * D_MODEL
SSD_HEAD_DIM = 64
SSD_HEADS = SSD_INNER // SSD_HEAD_DIM
SSD_GROUPS = 4
SSD_HEADS_PER_GROUP = SSD_HEADS // SSD_GROUPS
SSD_STATE = 128
SSD_CONV = 4
SSD_CHUNK = 128
SSD_GN = SSD_GROUPS * SSD_STATE
SSD_CONV_DIM = SSD_INNER + 2 * SSD_GN
SSD_GROUP_WIDTH = SSD_INNER // SSD_GROUPS
FFN_DIM = 2816
FFN_CONV = 3
NORM_EPS = 1e-6

LANES = 128
SUBLANES = 8
VMEM_LIMIT = 56 * 1024 * 1024
Q_BLOCK = 256
IDX_TILE = 512
ATT_TILE = 512
SEARCH_PASSES = 18
COARSE_PASSES = 8
SHIFT_KEYS = 128
PROJ_ROWS_A = 512
PROJ_ROWS_B = 256
MERGE_ROWS = 512
FFN_ROWS = 512
FFN_TILE = 256
MASKED = -1e30
LOG2_E = 1.4426950408889634
INT_MIN = -(2 ** 31)

F32 = jnp.float32
BF16 = jnp.bfloat16
_NT = (((1,), (1,)), ((), ()))
_HI = lax.Precision.HIGHEST


def _params(*sem):
    return pltpu.CompilerParams(dimension_semantics=sem, vmem_limit_bytes=VMEM_LIMIT)


def _resident(shape):
    nd = len(shape)
    return pl.BlockSpec(shape, lambda *_: (0,) * nd, pipeline_mode=pl.Buffered(1))


def _rms(x, w):
    return x * lax.rsqrt(jnp.mean(x * x, axis=-1, keepdims=True) + NORM_EPS) * w


def _sigmoid(x):
    return 1.0 / (1.0 + jnp.exp(-x))


def _dot(a, b):
    return jnp.dot(a, b, preferred_element_type=F32)


def _fold_rows(x, rows, op=jnp.add):
    parts = [x[r:r + rows] for r in range(0, x.shape[0], rows)]
    while len(parts) > 1:
        parts = [op(parts[a], parts[a + 1]) for a in range(0, len(parts) - 1, 2)] + parts[len(parts) & ~1:]
    return parts[0]


def _rope_tables(seq, rot_dim):
    half = rot_dim // 2
    inv = ROPE_THETA ** (-jnp.arange(0, rot_dim, 2, dtype=F32) / rot_dim)
    ang = jnp.arange(seq, dtype=F32)[:, None] * inv[None, :]
    cos, sin = jnp.cos(ang), jnp.sin(ang)
    pad = LANES - rot_dim
    c = jnp.concatenate([cos, cos, jnp.ones((seq, pad), F32)], axis=1)
    s1 = jnp.concatenate([-sin, jnp.zeros((seq, half + pad), F32)], axis=1)
    s2 = jnp.concatenate([jnp.zeros((seq, half), F32), sin, jnp.zeros((seq, pad), F32)], axis=1)
    return c, s1, s2


def _rope(t, c, s1, s2, half):
    return t * c + pltpu.roll(t, LANES - half, 1) * s1 + pltpu.roll(t, half, 1) * s2


_A_Q = 0
_A_K = _A_Q + ATTN_WIDTH
_A_QI = _A_K + KV_WIDTH
_A_KIW = _A_QI + IDX_HEADS * LANES
_A_COLS = _A_KIW + LANES


def _proj_attn_kernel(x_ref, nw_ref, w_ref, wvt_ref, ca_ref, sa1_ref, sa2_ref, ci_ref, si1_ref, si2_ref,
                      q_ref, k_ref, vt_ref, qi_ref, kiwf_ref, kiwb_ref):
    h = _rms(x_ref[...], nw_ref[...]).astype(BF16)
    ca, sa1, sa2 = ca_ref[...], sa1_ref[...], sa2_ref[...]
    ci, si1, si2 = ci_ref[...], si1_ref[...], si2_ref[...]
    ha = HEAD_DIM // ROPE_FRACTION // 2
    hi = IDX_DIM // ROPE_FRACTION // 2
    scale = HEAD_DIM ** -0.5 * LOG2_E
    t = _dot(h, w_ref[:, _A_Q:_A_Q + ATTN_WIDTH])
    for j in range(N_HEADS):
        js = slice(j * LANES, (j + 1) * LANES)
        q_ref[:, js] = (_rope(t[:, js], ca, sa1, sa2, ha) * scale).astype(BF16)
    t = _dot(h, w_ref[:, _A_K:_A_K + KV_WIDTH])
    for j in range(N_KV_HEADS):
        js = slice(j * LANES, (j + 1) * LANES)
        k_ref[:, js] = _rope(t[:, js], ca, sa1, sa2, ha).astype(BF16)
    vt_ref[...] = lax.dot_general(wvt_ref[...], h, _NT, preferred_element_type=F32).astype(BF16)
    t = _dot(h, w_ref[:, _A_QI:_A_QI + IDX_HEADS * LANES])
    for j in range(IDX_HEADS):
        js = slice(j * LANES, (j + 1) * LANES)
        qi_ref[:, js] = _rope(t[:, js], ci, si1, si2, hi).astype(BF16)
    t = _rope(_dot(h, w_ref[:, _A_KIW:_A_KIW + LANES]), ci, si1, si2, hi)
    kiwf_ref[...] = t
    kiwb_ref[...] = t.astype(BF16)


def _proj_attn(x2, nw, w_a, w_vt, tabs_a, tabs_i, seq):
    tokens = x2.shape[0]
    tm = PROJ_ROWS_A
    per_seq = seq // tm
    row = lambda w: pl.BlockSpec((tm, w), lambda i: (i, 0))
    tab = pl.BlockSpec((tm, LANES), lambda i: (i % per_seq, 0))
    sds = jax.ShapeDtypeStruct
    return pl.pallas_call(
        _proj_attn_kernel,
        grid=(tokens // tm,),
        in_specs=[row(D_MODEL), _resident((1, D_MODEL)), _resident((D_MODEL, _A_COLS)),
                  _resident((KV_WIDTH, D_MODEL))] + [tab] * 6,
        out_specs=[row(ATTN_WIDTH), row(KV_WIDTH), pl.BlockSpec((KV_WIDTH, tm), lambda i: (0, i)),
                   row(IDX_HEADS * LANES), row(LANES), row(LANES)],
        out_shape=[sds((tokens, ATTN_WIDTH), BF16), sds((tokens, KV_WIDTH), BF16), sds((KV_WIDTH, tokens), BF16),
                   sds((tokens, IDX_HEADS * LANES), BF16), sds((tokens, LANES), F32), sds((tokens, LANES), BF16)],
        compiler_params=_params("parallel"),
        name="proj_attn",
    )(x2, nw, w_a, w_vt, *tabs_a, *tabs_i)


_B_Z = 0
_B_XBC = _B_Z + SSD_INNER
_B_DT = _B_XBC + SSD_CONV_DIM
_B_COLS = _B_DT + LANES


def _proj_ssd_kernel(x_ref, nw_ref, w_ref, cw_ref, cb_ref, zs_ref, xa_ref, dt_ref, ext_ref):
    tm = PROJ_ROWS_B
    tail = SUBLANES

    @pl.when(pl.program_id(1) == 0)
    def _():
        ext_ref[0:tail, :] = jnp.zeros((tail, SSD_CONV_DIM), F32)

    h = _rms(x_ref[...], nw_ref[...]).astype(BF16)
    z = _dot(h, w_ref[:, _B_Z:_B_Z + SSD_INNER])
    zs_ref[...] = z * _sigmoid(z)
    dt_ref[...] = _dot(h, w_ref[:, _B_DT:_B_DT + LANES])
    cstep = 512
    for c0 in range(0, SSD_CONV_DIM, cstep):
        cs = slice(c0, c0 + cstep)
        ext_ref[tail:tail + tm, cs] = _dot(h, w_ref[:, _B_XBC + c0:_B_XBC + c0 + cstep])
        ue = ext_ref[:, cs]
        u = cb_ref[:, cs] + cw_ref[SSD_CONV - 1:SSD_CONV, cs] * ue[tail:tail + tm]
        for j in range(SSD_CONV - 1):
            back = SSD_CONV - 1 - j
            u = u + cw_ref[j:j + 1, cs] * pltpu.roll(ue, back, 0)[tail:tail + tm]
        xa_ref[:, cs] = u * _sigmoid(u)
        ext_ref[0:tail, cs] = ue[tm:tm + tail]


def _proj_ssd(x2, nw, w_b, cw, cb, bsz, seq):
    tm = PROJ_ROWS_B
    ns = seq // tm
    row = lambda w: pl.BlockSpec((tm, w), lambda b, s: (b * ns + s, 0))
    widths = (SSD_INNER, SSD_CONV_DIM, LANES)
    return pl.pallas_call(
        _proj_ssd_kernel,
        grid=(bsz, ns),
        in_specs=[row(D_MODEL), _resident((1, D_MODEL)), _resident((D_MODEL, _B_COLS)),
                  _resident((SSD_CONV, SSD_CONV_DIM)), _resident((1, SSD_CONV_DIM))],
        out_specs=[row(w) for w in widths],
        out_shape=[jax.ShapeDtypeStruct((bsz * seq, w), F32) for w in widths],
        scratch_shapes=[pltpu.VMEM((tm + SUBLANES, SSD_CONV_DIM), F32)],
        compiler_params=_params("parallel", "arbitrary"),
        name="proj_ssd",
    )(x2, nw, w_b, cw, cb)


def _sortable(x):
    return x ^ ((x >> 31) & jnp.int32(0x7FFFFFFF))


def _reduce_rows(x, op):
    for sh in (4, 2, 1):
        x = op(x, pltpu.roll(x, sh, 0))
    return x[0:1, :]


def _dsa_kernel(q_ref, qi_ref, wq_ref, k_ref, vt_ref, kib_ref, o_ref, key_ref, floor_ref, sel_ref, acc_ref, *,
                topk, seq):
    i = pl.program_id(1)
    n_idx = (i + IDX_TILE // Q_BLOCK) // (IDX_TILE // Q_BLOCK)
    n_att = (i + ATT_TILE // Q_BLOCK) // (ATT_TILE // Q_BLOCK)
    kpos = lax.broadcasted_iota(jnp.int32, (IDX_TILE, Q_BLOCK), 0)
    qpos = lax.broadcasted_iota(jnp.int32, (IDX_TILE, Q_BLOCK), 1) + i * Q_BLOCK
    w_t = wq_ref[...].T
    qi_all = jnp.concatenate([qi_ref[:, h * LANES:(h + 1) * LANES] for h in range(IDX_HEADS)], axis=0)
    kf = float(topk)
    int_max = jnp.int32(2 ** 31 - 1)

    def tile(j):
        return pl.ds(pl.multiple_of(j * IDX_TILE, IDX_TILE), IDX_TILE)

    def idx_body(j, carry, diagonal):
        kmax, kmin = carry
        s = lax.dot_general(kib_ref[tile(j), :], qi_all, _NT, preferred_element_type=F32)
        acc = jnp.zeros((IDX_TILE, Q_BLOCK), F32)
        for h in range(IDX_HEADS):
            acc = acc + w_t[IDX_DIM + h:IDX_DIM + h + 1, :] * jnp.maximum(s[:, h * Q_BLOCK:(h + 1) * Q_BLOCK], 0.0)
        bits = pltpu.bitcast(acc, jnp.int32)
        key = _sortable(bits)
        floor = pltpu.bitcast((bits + ((bits >> 31) & jnp.int32(0xFFFF))) & jnp.int32(-0x10000), F32)
        low = high = key
        if diagonal:
            causal = kpos + j * IDX_TILE <= qpos
            low = jnp.where(causal, key, jnp.int32(INT_MIN))
            high = jnp.where(causal, key, int_max)
            floor = jnp.where(causal, floor, -jnp.inf)
        key_ref[tile(j), :] = low
        floor_ref[tile(j), :] = floor.astype(BF16)
        kmax = jnp.maximum(kmax, _fold_rows(low, SUBLANES, jnp.maximum))
        kmin = jnp.minimum(kmin, _fold_rows(high, SUBLANES, jnp.minimum))
        return kmax, kmin

    assert IDX_TILE >= Q_BLOCK
    extremes = lax.fori_loop(0, n_idx - 1, functools.partial(idx_body, diagonal=False),
                             (jnp.full((SUBLANES, Q_BLOCK), INT_MIN, jnp.int32),
                              jnp.full((SUBLANES, Q_BLOCK), 2 ** 31 - 1, jnp.int32)))
    kmax, kmin = idx_body(n_idx - 1, extremes, diagonal=True)
    kmax = _reduce_rows(kmax, jnp.maximum)
    kmin = _reduce_rows(kmin, jnp.minimum)

    def count(pred):
        def body(j, cnt):
            m = jnp.where(pred(key_ref[tile(j), :], kpos + j * IDX_TILE), 1.0, 0.0)
            return cnt + _fold_rows(m, SUBLANES)
        cnt = lax.fori_loop(0, n_idx, body, jnp.zeros((SUBLANES, Q_BLOCK), F32))
        return jnp.sum(cnt, axis=0, keepdims=True)

    n_valid = (qpos[0:1, :] + 1).astype(F32)
    few = n_valid <= kf
    flat = kmin == kmax
    start = few | flat
    thr0 = jnp.where(few, jnp.int32(INT_MIN + 1), kmin)
    state = (jnp.where(start, thr0, kmin), jnp.where(start, thr0, kmax + 1), n_valid)

    def search_pass(it, st):
        lo, hi, c_lo = st
        f_lo = pltpu.bitcast(_sortable(lo), F32)
        f_hi = pltpu.bitcast(_sortable(hi - 1), F32)
        piv = _sortable(pltpu.bitcast(0.5 * f_lo + 0.5 * f_hi, jnp.int32))
        piv = jnp.where(it % 4 == 3, lo + lax.shift_right_logical(hi - lo, 1), piv)
        piv = jnp.where(it == 0, jnp.int32(1), piv)
        piv = jnp.where(it == 1, jnp.where(hi == 1, jnp.int32(-1), piv), piv)
        piv = jnp.minimum(jnp.maximum(piv, lo + 1), hi - 1)
        c = count(lambda kt, pos: kt >= piv)
        live = lo != hi
        ge = c >= kf
        hit = c == kf
        lo_n = jnp.where(live & ge, piv, lo)
        c_lo_n = jnp.where(live & ge, c, c_lo)
        hi_n = jnp.where(live & jnp.logical_not(ge), piv, hi)
        hi_n = jnp.where(live & (hit | (hi_n - lo_n == 1)), lo_n, hi_n)
        return lo_n, hi_n, c_lo_n

    def count_coarse(p):
        pk = 2 * SUBLANES
        def body(j, cnt):
            m = jnp.where(floor_ref[tile(j), :] >= p, jnp.bfloat16(1), jnp.bfloat16(0))
            return cnt + _fold_rows(m, pk).astype(F32)
        cnt = lax.fori_loop(0, n_idx, body, jnp.zeros((pk, Q_BLOCK), F32))
        return jnp.sum(cnt, axis=0, keepdims=True)

    def coarse_pass(it, st):
        lo, hi, c_lo = st
        f_lo = pltpu.bitcast(_sortable(lo), F32)
        f_hi = pltpu.bitcast(_sortable(hi - 1), F32)
        p = (0.5 * f_lo + 0.5 * f_hi).astype(BF16)
        p_bits = pltpu.bitcast(p.astype(F32), jnp.int32)
        piv = _sortable(p_bits)
        expo = p_bits & jnp.int32(0x7F800000)
        ok = (piv > lo) & (piv < hi) & (expo != 0) & (expo != jnp.int32(0x7F800000))
        c = count_coarse(p)
        live = (lo != hi) & ok
        ge = c >= kf
        hit = c == kf
        lo_n = jnp.where(live & ge, piv, lo)
        c_lo_n = jnp.where(live & ge, c, c_lo)
        hi_n = jnp.where(live & jnp.logical_not(ge), piv, hi)
        hi_n = jnp.where(live & (hit | (hi_n - lo_n == 1)), lo_n, hi_n)
        return lo_n, hi_n, c_lo_n

    state = lax.fori_loop(0, 2, search_pass, state)
    state = lax.fori_loop(2, 2 + COARSE_PASSES, coarse_pass, state)
    state = lax.fori_loop(2 + COARSE_PASSES, SEARCH_PASSES, search_pass, state)

    def pending(st):
        return jnp.max(jnp.where(st[0] != st[1], 1.0, 0.0))

    def more_cond(carry):
        return jnp.logical_and(carry[1] > 0.0, carry[0] < 1024)

    def more_body(carry):
        st = search_pass(carry[0] + 1, search_pass(carry[0], carry[2:]))
        return (carry[0] + 2, pending(st)) + st

    thr, _, c_ge = lax.while_loop(more_cond, more_body, (jnp.int32(SEARCH_PASSES), pending(state)) + state)[2:]
    tied = jnp.max(c_ge) > kf

    def write_sel(pred):
        def body(j, carry):
            sel = pred(key_ref[tile(j), :], kpos + j * IDX_TILE)
            sel_ref[tile(j), :] = jnp.where(sel, 1.0, 0.0).astype(BF16)
            return carry
        lax.fori_loop(0, n_idx, body, 0)

    @pl.when(jnp.logical_not(tied))
    def _():
        write_sel(lambda kt, pos: kt >= thr)

    @pl.when(tied)
    def _():
        need = kf - count(lambda kt, pos: kt > thr)
        nbits = max(1, int(np.ceil(np.log2(seq))))

        def pos_body(p, lim):
            cand = lim + lax.shift_left(jnp.int32(1), nbits - 1 - p)
            c = count(lambda kt, pos: jnp.where(kt == thr, pos, jnp.int32(seq)) < cand)
            return jnp.where(c < need, cand, lim)

        lim = lax.fori_loop(0, nbits, pos_body, jnp.zeros((1, Q_BLOCK), jnp.int32))
        lim = jnp.where(c_ge > kf, lim, jnp.int32(seq))
        write_sel(lambda kt, pos: (kt > thr) | ((kt == thr) & (pos <= lim)))

    rows = Q_PER_KV * Q_BLOCK
    pk = 2 * SUBLANES
    qgs = [jnp.concatenate([q_ref[:, (Q_PER_KV * g + e) * LANES:(Q_PER_KV * g + e + 1) * LANES]
                            for e in range(Q_PER_KV)], axis=0) for g in range(N_KV_HEADS)]
    ones_rows = jnp.ones((pk, ATT_TILE), BF16)

    def logits(j, g):
        ks = pl.ds(pl.multiple_of(j * ATT_TILE, ATT_TILE), ATT_TILE)
        return lax.dot_general(k_ref[ks, g * LANES:(g + 1) * LANES], qgs[g], _NT, preferred_element_type=F32)

    def weighted_v(j, g, p):
        ks = pl.ds(pl.multiple_of(j * ATT_TILE, ATT_TILE), ATT_TILE)
        sel = jnp.concatenate([sel_ref[ks, :]] * Q_PER_KV, axis=1)
        v_aug = jnp.concatenate([vt_ref[g * LANES:(g + 1) * LANES, ks], ones_rows], axis=0)
        return _dot(v_aug, p.astype(BF16) * sel)

    shifts = [jnp.max(lax.dot_general(k_ref[0:SHIFT_KEYS, g * LANES:(g + 1) * LANES], qgs[g], _NT,
                                      preferred_element_type=F32), axis=0, keepdims=True)
              for g in range(N_KV_HEADS)]

    acc_ref[...] = jnp.zeros_like(acc_ref)

    def fast_body(j, carry):
        for g in range(N_KV_HEADS):
            acc_ref[g] += weighted_v(j, g, jnp.exp2(logits(j, g) - shifts[g]))
        return carry

    lax.fori_loop(0, n_att, fast_body, 0)
    accs = [acc_ref[g] for g in range(N_KV_HEADS)]
    lo_ok, hi_ok = 2.0 ** -60, 2.0 ** 60
    worst = [jnp.max(jnp.where((a[LANES:LANES + 1] >= lo_ok) & (a[LANES:LANES + 1] <= hi_ok), 0.0, 1.0))
             + jnp.max(jnp.where(jnp.abs(a[0:LANES]) < jnp.inf, 0.0, 1.0)) for a in accs]
    redo = (worst[0] + worst[1]) > 0.0

    def store(g, acc):
        o_t = acc[0:LANES] / acc[LANES:LANES + 1]
        for e in range(Q_PER_KV):
            col = (Q_PER_KV * g + e) * LANES
            o_ref[:, col:col + LANES] = o_t[:, e * Q_BLOCK:(e + 1) * Q_BLOCK].T.astype(BF16)

    @pl.when(jnp.logical_not(redo))
    def _():
        for g in range(N_KV_HEADS):
            store(g, accs[g])

    @pl.when(redo)
    def _():
        def safe_body(j, carry):
            ks = pl.ds(pl.multiple_of(j * ATT_TILE, ATT_TILE), ATT_TILE)
            sel = jnp.concatenate([sel_ref[ks, :]] * Q_PER_KV, axis=1).astype(F32)
            out = []
            for g in range(N_KV_HEADS):
                m, acc = carry[g]
                s = logits(j, g) + (sel - 1.0) * (-MASKED)
                m_new = jnp.maximum(m, jnp.max(s, axis=0, keepdims=True))
                out.append((m_new, jnp.exp2(m - m_new) * acc + weighted_v(j, g, jnp.exp2(s - m_new))))
            return tuple(out)

        init = (jnp.full((1, rows), MASKED, F32), jnp.zeros((LANES + pk, rows), F32))
        res = lax.fori_loop(0, n_att, safe_body, (init,) * N_KV_HEADS)
        for g in range(N_KV_HEADS):
            store(g, res[g][1])


def _dsa(q, k, vt, qi, kiwf, kiwb, bsz, seq):
    assert seq % IDX_TILE == 0 and seq % ATT_TILE == 0
    nq = seq // Q_BLOCK
    topk = min(TOPK_MAX, seq // 4)
    qrow = lambda w: pl.BlockSpec((Q_BLOCK, w), lambda b, i: (b * nq + i, 0))
    full = lambda w: pl.BlockSpec((seq, w), lambda b, i: (b, 0))
    return pl.pallas_call(
        functools.partial(_dsa_kernel, topk=topk, seq=seq),
        grid=(bsz, nq),
        in_specs=[qrow(ATTN_WIDTH), qrow(IDX_HEADS * LANES), qrow(LANES), full(KV_WIDTH),
                  pl.BlockSpec((KV_WIDTH, seq), lambda b, i: (0, b)), full(LANES)],
        out_specs=qrow(ATTN_WIDTH),
        out_shape=jax.ShapeDtypeStruct((bsz * seq, ATTN_WIDTH), BF16),
        scratch_shapes=[pltpu.VMEM((seq, Q_BLOCK), jnp.int32), pltpu.VMEM((seq, Q_BLOCK), BF16),
                        pltpu.VMEM((seq, Q_BLOCK), BF16),
                        pltpu.VMEM((N_KV_HEADS, LANES + 2 * SUBLANES, Q_PER_KV * Q_BLOCK), F32)],
        compiler_params=_params("parallel", "arbitrary"),
        name="dsa_attention",
    )(q, qi, kiwf, k, vt, kiwb)


def _split3(x):
    hi = x.astype(BF16)
    r = x - hi.astype(F32)
    mid = r.astype(BF16)
    return hi, mid, (r - mid.astype(F32)).astype(BF16)


def _ssd_kernel(zs_ref, xa_ref, dt_ref, dtb_ref, alog_ref, dexp_ref, nw_ref, e_ref, o_ref, st_ref):
    L = SSD_CHUNK

    @pl.when(pl.program_id(1) == 0)
    def _():
        st_ref[...] = jnp.zeros_like(st_ref)

    dtv = dt_ref[...] + dtb_ref[...]
    dt = jnp.maximum(dtv, 0.0) + jnp.log1p(jnp.exp(-jnp.abs(dtv)))
    adt = dt * (-jnp.exp(alog_ref[...]))
    ri = lax.broadcasted_iota(jnp.int32, (L, L), 0)
    ci = lax.broadcasted_iota(jnp.int32, (L, L), 1)
    tril = ci <= ri
    tril_b = jnp.where(tril, 1.0, 0.0).astype(BF16)
    acs = sum(_dot(tril_b, t) for t in _split3(adt))
    last = acs[L - 1:L, :]
    acs_t = acs.T
    dt_t = dt.T
    hi = jnp.concatenate([dt * jnp.exp(last - acs), jnp.exp(acs)], axis=0)
    stack = (hi.astype(BF16), (hi - hi.astype(BF16).astype(F32)).astype(BF16))
    first_half = ci < SSD_HEAD_DIM
    gw = SSD_GROUP_WIDTH

    for g in range(SSD_GROUPS):
        gs = slice(g * gw, (g + 1) * gw)
        b_g = xa_ref[:, SSD_INNER + g * SSD_STATE:SSD_INNER + (g + 1) * SSD_STATE]
        c_g = xa_ref[:, SSD_INNER + SSD_GN + g * SSD_STATE:SSD_INNER + SSD_GN + (g + 1) * SSD_STATE]
        b_b, c_b = b_g.astype(BF16), c_g.astype(BF16)
        cb = lax.dot_general(c_b, b_b, _NT, preferred_element_type=F32)
        b_t = b_g.T.astype(BF16)
        ex = sum(_dot(t, e_ref[:, gs]) for t in stack)
        dtdec, eacs = ex[0:L], ex[L:2 * L]
        xs = xa_ref[:, gs]
        s_in = st_ref[:, gs]
        y = _dot(c_b, s_in.astype(BF16)) * eacs + xs * dexp_ref[:, gs]
        st_ref[:, gs] = s_in * eacs[L - 1:L, :] + _dot(b_t, (xs * dtdec).astype(BF16))
        ys = []
        for c2 in range(gw // LANES):
            xp = xs[:, c2 * LANES:(c2 + 1) * LANES].astype(BF16)
            pair = []
            for hh in range(LANES // SSD_HEAD_DIM):
                h = g * SSD_HEADS_PER_GROUP + c2 * (LANES // SSD_HEAD_DIM) + hh
                seg = acs[:, h:h + 1] - acs_t[h:h + 1, :]
                lm = jnp.exp(jnp.where(tril, seg, -jnp.inf))
                pair.append(_dot((cb * lm * dt_t[h:h + 1, :]).astype(BF16), xp))
            ys.append(jnp.where(first_half, pair[0], pair[1]))
        y = y + jnp.concatenate(ys, axis=1)
        yz = y * zs_ref[:, gs]
        ms = jnp.mean(yz * yz, axis=1, keepdims=True)
        o_ref[:, gs] = (yz * lax.rsqrt(ms + NORM_EPS) * nw_ref[:, gs]).astype(BF16)


def _ssd(zs, xa, dtp, dtb, alog, dexp, nw, emat, bsz, seq):
    L = SSD_CHUNK
    nc = seq // L
    row = lambda w: pl.BlockSpec((L, w), lambda b, c: (b * nc + c, 0))
    return pl.pallas_call(
        _ssd_kernel,
        grid=(bsz, nc),
        in_specs=[row(SSD_INNER), row(SSD_CONV_DIM), row(LANES),
                  _resident((1, LANES)), _resident((1, LANES)), _resident((1, SSD_INNER)),
                  _resident((1, SSD_INNER)), _resident((LANES, SSD_INNER))],
        out_specs=row(SSD_INNER),
        out_shape=jax.ShapeDtypeStruct((bsz * seq, SSD_INNER), BF16),
        scratch_shapes=[pltpu.VMEM((SSD_STATE, SSD_INNER), F32)],
        compiler_params=_params("parallel", "arbitrary"),
        name="ssd_branch",
    )(zs, xa, dtp, dtb, alog, dexp, nw, emat)


def _merge_kernel(x_ref, a_ref, b_ref, nw_ref, wg_ref, wpa_ref, wps_ref, wo_ref, o_ref):
    x = x_ref[...]
    h = _rms(x, nw_ref[...]).astype(BF16)
    merged = _sigmoid(_dot(h, wg_ref[:, 0:D_MODEL])) * _dot(a_ref[...], wpa_ref[...])
    merged = merged + _sigmoid(_dot(h, wg_ref[:, D_MODEL:2 * D_MODEL])) * _dot(b_ref[...], wps_ref[...])
    o_ref[...] = x + _dot(merged.astype(BF16), wo_ref[...])


def _merge(x2, a, b, nw, wg, wpa, wps, wo):
    tokens = x2.shape[0]
    tm = MERGE_ROWS
    row = lambda w: pl.BlockSpec((tm, w), lambda i: (i, 0))
    return pl.pallas_call(
        _merge_kernel,
        grid=(tokens // tm,),
        in_specs=[row(D_MODEL), row(ATTN_WIDTH), row(SSD_INNER), _resident((1, D_MODEL)),
                  _resident((D_MODEL, 2 * D_MODEL)), _resident((ATTN_WIDTH, D_MODEL)),
                  _resident((SSD_INNER, D_MODEL)), _resident((D_MODEL, D_MODEL))],
        out_specs=row(D_MODEL),
        out_shape=jax.ShapeDtypeStruct((tokens, D_MODEL), F32),
        compiler_params=_params("parallel"),
        name="merge_out",
    )(x2, a, b, nw, wg, wpa, wps, wo)


def _ffn_kernel(x_ref, nw_ref, wup_ref, cw_ref, cb_ref, wdn_ref, fnw_ref, o_ref, act_ref, carry_ref, *, final_norm):
    tm = FFN_ROWS
    tail = SUBLANES
    nf = FFN_DIM // FFN_TILE

    @pl.when(pl.program_id(1) == 0)
    def _():
        carry_ref[...] = jnp.zeros_like(carry_ref)

    x = x_ref[...]
    h = _rms(x, nw_ref[...]).astype(BF16)
    for f in range(nf):
        parts = []
        for part in range(2):
            col = part * FFN_DIM + f * FFN_TILE
            cs = slice(col, col + FFN_TILE)
            ct = part * nf + f
            u = _dot(h, wup_ref[:, cs])
            ue = jnp.concatenate([carry_ref[ct], u], axis=0)
            cv = cb_ref[:, cs] + cw_ref[FFN_CONV - 1:FFN_CONV, cs] * u
            for j in range(FFN_CONV - 1):
                back = FFN_CONV - 1 - j
                cv = cv + cw_ref[j:j + 1, cs] * pltpu.roll(ue, back, 0)[tail:tail + tm]
            carry_ref[ct] = u[tm - tail:tm, :]
            parts.append(cv)
        act_ref[:, f * FFN_TILE:(f + 1) * FFN_TILE] = (parts[0] * _sigmoid(parts[0]) * parts[1]).astype(BF16)
    y = x + _dot(act_ref[...], wdn_ref[...])
    if final_norm:
        y = _rms(y, fnw_ref[...])
    o_ref[...] = y


def _ffn(x2, nw, wup, cw, cb, wdn, fnw, bsz, seq, final_norm):
    tm = FFN_ROWS
    ns = seq // tm
    nf = FFN_DIM // FFN_TILE
    row = pl.BlockSpec((tm, D_MODEL), lambda b, s: (b * ns + s, 0))
    return pl.pallas_call(
        functools.partial(_ffn_kernel, final_norm=final_norm),
        grid=(bsz, ns),
        in_specs=[row, _resident((1, D_MODEL)), _resident((D_MODEL, 2 * FFN_DIM)),
                  _resident((FFN_CONV, 2 * FFN_DIM)), _resident((1, 2 * FFN_DIM)),
                  _resident((FFN_DIM, D_MODEL)), _resident((1, D_MODEL))],
        out_specs=row,
        out_shape=jax.ShapeDtypeStruct((bsz * seq, D_MODEL), F32),
        scratch_shapes=[pltpu.VMEM((tm, FFN_DIM), BF16),
                        pltpu.VMEM((2 * nf, SUBLANES, FFN_TILE), F32)],
        compiler_params=_params("parallel", "arbitrary"),
        name="conv_glu_ffn",
    )(x2, nw, wup, cw, cb, wdn, fnw)


def _split_w_in(w_in):
    o = np.cumsum([0, ATTN_WIDTH, KV_WIDTH, KV_WIDTH, IDX_HEADS * IDX_DIM, IDX_DIM, IDX_HEADS,
                   SSD_INNER, SSD_CONV_DIM, SSD_HEADS, D_MODEL, D_MODEL]).tolist()
    seg = lambda j: w_in[:, o[j]:o[j + 1]]
    d = w_in.shape[0]
    qi = seg(3).reshape(d, IDX_HEADS, IDX_DIM)
    qi = jnp.pad(qi, ((0, 0), (0, 0), (0, LANES - IDX_DIM))).reshape(d, IDX_HEADS * LANES)
    kiw = jnp.pad(jnp.concatenate([seg(4), seg(5)], axis=1), ((0, 0), (0, LANES - IDX_DIM - IDX_HEADS)))
    w_a = jnp.concatenate([seg(0), seg(1), qi, kiw], axis=1).astype(BF16)
    w_vt = seg(2).T.astype(BF16)
    dt = jnp.pad(seg(8), ((0, 0), (0, LANES - SSD_HEADS)))
    w_b = jnp.concatenate([seg(6), seg(7), dt], axis=1).astype(BF16)
    w_g = jnp.concatenate([seg(9), seg(10)], axis=1).astype(BF16)
    return w_a, w_vt, w_b, w_g


def _lane_pad_row(v):
    return jnp.pad(v, (0, LANES - v.shape[0]))[None, :]


def kernel(x, norm_mix_w, w_in, ssd_conv_w, ssd_conv_b, ssd_dt_bias, ssd_a_log, ssd_d, ssd_norm_w, w_proj_attn, w_proj_ssd, w_out, norm_ffn_w, ffn_w_up, ffn_conv_w, ffn_conv_b, ffn_w_down, norm_final_w):
    bsz, seq, d = x.shape
    depth = w_in.shape[0]
    assert d == D_MODEL and seq % max(PROJ_ROWS_A, FFN_ROWS, MERGE_ROWS, IDX_TILE) == 0
    tabs_a = _rope_tables(seq, HEAD_DIM // ROPE_FRACTION)
    tabs_i = _rope_tables(seq, IDX_DIM // ROPE_FRACTION)
    emat = (jnp.arange(SSD_INNER)[None, :] // SSD_HEAD_DIM == jnp.arange(LANES)[:, None]).astype(BF16)
    x2 = x.reshape(bsz * seq, d)
    for l in range(depth):
        w_a, w_vt, w_b, w_g = _split_w_in(w_in[l])
        nw = norm_mix_w[l][None, :]
        q, k, vt, qi, kiwf, kiwb = _proj_attn(x2, nw, w_a, w_vt, tabs_a, tabs_i, seq)
        a = _dsa(q, k, vt, qi, kiwf, kiwb, bsz, seq)
        zs, xa, dtp = _proj_ssd(x2, nw, w_b, ssd_conv_w[l], ssd_conv_b[l][None, :], bsz, seq)
        b = _ssd(zs, xa, dtp, _lane_pad_row(ssd_dt_bias[l]), _lane_pad_row(ssd_a_log[l]),
                 jnp.repeat(ssd_d[l], SSD_HEAD_DIM)[None, :], ssd_norm_w[l][None, :], emat, bsz, seq)
        x2 = _merge(x2, a, b, nw, w_g, w_proj_attn[l].astype(BF16), w_proj_ssd[l].astype(BF16),
                    w_out[l].astype(BF16))
        x2 = _ffn(x2, norm_ffn_w[l][None, :], ffn_w_up[l].astype(BF16), ffn_conv_w[l], ffn_conv_b[l][None, :],
                  ffn_w_down[l].astype(BF16), norm_final_w[None, :], bsz, seq, final_norm=(l == depth - 1))
    return x2.reshape(bsz, seq, d)
```

```python
import functools

import numpy as np
import jax
import jax.numpy as jnp
from jax import lax
from jax.experimental import pallas as pl
from jax.experimental.pallas import tpu as pltpu

D_MODEL = 1024
N_HEADS = 8
HEAD_DIM = 128
N_KV_HEADS = 2
Q_PER_KV = N_HEADS // N_KV_HEADS
ATTN_WIDTH = N_HEADS * HEAD_DIM
KV_WIDTH = N_KV_HEADS * HEAD_DIM
IDX_HEADS = 8
IDX_DIM = 64
TOPK_MAX = 256
ROPE_THETA = 500000.0
ROPE_FRACTION = 4
SSD_INNER = 2 * D_MODEL
SSD_HEAD_DIM = 64
SSD_HEADS = SSD_INNER // SSD_HEAD_DIM
SSD_GROUPS = 4
SSD_HEADS_PER_GROUP = SSD_HEADS // SSD_GROUPS
SSD_STATE = 128
SSD_CONV = 4
SSD_CHUNK = 128
SSD_GN = SSD_GROUPS * SSD_STATE
SSD_CONV_DIM = SSD_INNER + 2 * SSD_GN
SSD_GROUP_WIDTH = SSD_INNER // SSD_GROUPS
FFN_DIM = 2816
FFN_CONV = 3
NORM_EPS = 1e-6

LANES = 128
SUBLANES = 8
VMEM_LIMIT = 56 * 1024 * 1024
Q_BLOCK = 256
IDX_TILE = 512
ATT_TILE = 512
SEARCH_PASSES = 18
COARSE_PASSES = 8
SSD_STEP_CHUNKS = 2
SHIFT_KEYS = 128
PROJ_ROWS_A = 512
PROJ_ROWS_B = 256
MERGE_ROWS = 512
FFN_ROWS = 512
FFN_TILE = 256
MASKED = -1e30
LOG2_E = 1.4426950408889634
INT_MIN = -(2 ** 31)

F32 = jnp.float32
BF16 = jnp.bfloat16
_NT = (((1,), (1,)), ((), ()))
_HI = lax.Precision.HIGHEST


def _params(*sem):
    return pltpu.CompilerParams(dimension_semantics=sem, vmem_limit_bytes=VMEM_LIMIT)


def _resident(shape):
    nd = len(shape)
    return pl.BlockSpec(shape, lambda *_: (0,) * nd, pipeline_mode=pl.Buffered(1))


def _rms(x, w):
    return x * lax.rsqrt(jnp.mean(x * x, axis=-1, keepdims=True) + NORM_EPS) * w


def _sigmoid(x):
    return 1.0 / (1.0 + jnp.exp(-x))


def _dot(a, b):
    return jnp.dot(a, b, preferred_element_type=F32)


def _fold_rows(x, rows, op=jnp.add):
    parts = [x[r:r + rows] for r in range(0, x.shape[0], rows)]
    while len(parts) > 1:
        parts = [op(parts[a], parts[a + 1]) for a in range(0, len(parts) - 1, 2)] + parts[len(parts) & ~1:]
    return parts[0]


def _rope_tables(seq, rot_dim):
    half = rot_dim // 2
    inv = ROPE_THETA ** (-jnp.arange(0, rot_dim, 2, dtype=F32) / rot_dim)
    ang = jnp.arange(seq, dtype=F32)[:, None] * inv[None, :]
    cos, sin = jnp.cos(ang), jnp.sin(ang)
    pad = LANES - rot_dim
    c = jnp.concatenate([cos, cos, jnp.ones((seq, pad), F32)], axis=1)
    s1 = jnp.concatenate([-sin, jnp.zeros((seq, half + pad), F32)], axis=1)
    s2 = jnp.concatenate([jnp.zeros((seq, half), F32), sin, jnp.zeros((seq, pad), F32)], axis=1)
    return c, s1, s2


def _rope(t, c, s1, s2, half):
    return t * c + pltpu.roll(t, LANES - half, 1) * s1 + pltpu.roll(t, half, 1) * s2


_A_Q = 0
_A_K = _A_Q + ATTN_WIDTH
_A_QI = _A_K + KV_WIDTH
_A_KIW = _A_QI + IDX_HEADS * LANES
_A_COLS = _A_KIW + LANES


def _proj_attn_kernel(x_ref, nw_ref, w_ref, wvt_ref, ca_ref, sa1_ref, sa2_ref, ci_ref, si1_ref, si2_ref,
                      q_ref, k_ref, vt_ref, qi_ref, kiwf_ref, kiwb_ref):
    h = _rms(x_ref[...], nw_ref[...]).astype(BF16)
    ca, sa1, sa2 = ca_ref[...], sa1_ref[...], sa2_ref[...]
    ci, si1, si2 = ci_ref[...], si1_ref[...], si2_ref[...]
    ha = HEAD_DIM // ROPE_FRACTION // 2
    hi = IDX_DIM // ROPE_FRACTION // 2
    scale = HEAD_DIM ** -0.5 * LOG2_E
    t = _dot(h, w_ref[:, _A_Q:_A_Q + ATTN_WIDTH])
    for j in range(N_HEADS):
        js = slice(j * LANES, (j + 1) * LANES)
        q_ref[:, js] = (_rope(t[:, js], ca, sa1, sa2, ha) * scale).astype(BF16)
    t = _dot(h, w_ref[:, _A_K:_A_K + KV_WIDTH])
    for j in range(N_KV_HEADS):
        js = slice(j * LANES, (j + 1) * LANES)
        k_ref[:, js] = _rope(t[:, js], ca, sa1, sa2, ha).astype(BF16)
    vt_ref[...] = lax.dot_general(wvt_ref[...], h, _NT, preferred_element_type=F32).astype(BF16)
    t = _dot(h, w_ref[:, _A_QI:_A_QI + IDX_HEADS * LANES])
    for j in range(IDX_HEADS):
        js = slice(j * LANES, (j + 1) * LANES)
        qi_ref[:, js] = _rope(t[:, js], ci, si1, si2, hi).astype(BF16)
    t = _rope(_dot(h, w_ref[:, _A_KIW:_A_KIW + LANES]), ci, si1, si2, hi)
    kiwf_ref[...] = t
    kiwb_ref[...] = t.astype(BF16)


def _proj_attn(x2, nw, w_a, w_vt, tabs_a, tabs_i, seq):
    tokens = x2.shape[0]
    tm = PROJ_ROWS_A
    per_seq = seq // tm
    row = lambda w: pl.BlockSpec((tm, w), lambda i: (i, 0))
    tab = pl.BlockSpec((tm, LANES), lambda i: (i % per_seq, 0))
    sds = jax.ShapeDtypeStruct
    return pl.pallas_call(
        _proj_attn_kernel,
        grid=(tokens // tm,),
        in_specs=[row(D_MODEL), _resident((1, D_MODEL)), _resident((D_MODEL, _A_COLS)),
                  _resident((KV_WIDTH, D_MODEL))] + [tab] * 6,
        out_specs=[row(ATTN_WIDTH), row(KV_WIDTH), pl.BlockSpec((KV_WIDTH, tm), lambda i: (0, i)),
                   row(IDX_HEADS * LANES), row(LANES), row(LANES)],
        out_shape=[sds((tokens, ATTN_WIDTH), BF16), sds((tokens, KV_WIDTH), BF16), sds((KV_WIDTH, tokens), BF16),
                   sds((tokens, IDX_HEADS * LANES), BF16), sds((tokens, LANES), F32), sds((tokens, LANES), BF16)],
        compiler_params=_params("parallel"),
        name="proj_attn",
    )(x2, nw, w_a, w_vt, *tabs_a, *tabs_i)


_B_Z = 0
_B_XBC = _B_Z + SSD_INNER
_B_DT = _B_XBC + SSD_CONV_DIM
_B_COLS = _B_DT + LANES


def _proj_ssd_kernel(x_ref, nw_ref, w_ref, cw_ref, cb_ref, zs_ref, xa_ref, dt_ref, ext_ref):
    tm = PROJ_ROWS_B
    tail = SUBLANES

    @pl.when(pl.program_id(1) == 0)
    def _():
        ext_ref[0:tail, :] = jnp.zeros((tail, SSD_CONV_DIM), F32)

    h = _rms(x_ref[...], nw_ref[...]).astype(BF16)
    dt_ref[...] = _dot(h, w_ref[:, _B_DT:_B_DT + LANES])
    cstep = 512
    z_chunks = list(range(0, SSD_INNER, cstep))
    for c0 in range(0, SSD_CONV_DIM, cstep):
        cs = slice(c0, c0 + cstep)
        if z_chunks:
            z0 = z_chunks.pop(0)
            z = _dot(h, w_ref[:, _B_Z + z0:_B_Z + z0 + cstep])
            zs_ref[:, z0:z0 + cstep] = z * _sigmoid(z)
        ext_ref[tail:tail + tm, cs] = _dot(h, w_ref[:, _B_XBC + c0:_B_XBC + c0 + cstep])
        for c1 in range(c0, c0 + cstep, LANES):
            cl = slice(c1, c1 + LANES)
            ue = ext_ref[:, cl]
            u = cb_ref[:, cl] + cw_ref[SSD_CONV - 1:SSD_CONV, cl] * ue[tail:tail + tm]
            for j in range(SSD_CONV - 1):
                back = SSD_CONV - 1 - j
                u = u + cw_ref[j:j + 1, cl] * pltpu.roll(ue, back, 0)[tail:tail + tm]
            xa_ref[:, cl] = u * _sigmoid(u)
            ext_ref[0:tail, cl] = ue[tm:tm + tail]


def _proj_ssd(x2, nw, w_b, cw, cb, bsz, seq):
    tm = PROJ_ROWS_B
    ns = seq // tm
    row = lambda w: pl.BlockSpec((tm, w), lambda b, s: (b * ns + s, 0))
    widths = (SSD_INNER, SSD_CONV_DIM, LANES)
    return pl.pallas_call(
        _proj_ssd_kernel,
        grid=(bsz, ns),
        in_specs=[row(D_MODEL), _resident((1, D_MODEL)), _resident((D_MODEL, _B_COLS)),
                  _resident((SSD_CONV, SSD_CONV_DIM)), _resident((1, SSD_CONV_DIM))],
        out_specs=[row(w) for w in widths],
        out_shape=[jax.ShapeDtypeStruct((bsz * seq, w), F32) for w in widths],
        scratch_shapes=[pltpu.VMEM((tm + SUBLANES, SSD_CONV_DIM), F32)],
        compiler_params=_params("parallel", "arbitrary"),
        name="proj_ssd",
    )(x2, nw, w_b, cw, cb)


def _sortable(x):
    return x ^ ((x >> 31) & jnp.int32(0x7FFFFFFF))


def _reduce_rows(x, op):
    for sh in (4, 2, 1):
        x = op(x, pltpu.roll(x, sh, 0))
    return x[0:1, :]


def _dsa_kernel(q_ref, qi_ref, wq_ref, k_ref, vt_ref, kib_ref, o_ref, key_ref, floor_ref, sel_ref, acc_ref, *,
                topk, seq):
    i = pl.program_id(1)
    n_idx = (i + IDX_TILE // Q_BLOCK) // (IDX_TILE // Q_BLOCK)
    n_att = (i + ATT_TILE // Q_BLOCK) // (ATT_TILE // Q_BLOCK)
    kpos = lax.broadcasted_iota(jnp.int32, (IDX_TILE, Q_BLOCK), 0)
    qpos = lax.broadcasted_iota(jnp.int32, (IDX_TILE, Q_BLOCK), 1) + i * Q_BLOCK
    w_t = wq_ref[...].T
    qi_all = jnp.concatenate([qi_ref[:, h * LANES:(h + 1) * LANES] for h in range(IDX_HEADS)], axis=0)
    kf = float(topk)
    int_max = jnp.int32(2 ** 31 - 1)

    def tile(j):
        return pl.ds(pl.multiple_of(j * IDX_TILE, IDX_TILE), IDX_TILE)

    def idx_body(j, carry, diagonal):
        kmax, kmin = carry
        s = lax.dot_general(kib_ref[tile(j), :], qi_all, _NT, preferred_element_type=F32)
        acc = jnp.zeros((IDX_TILE, Q_BLOCK), F32)
        for h in range(IDX_HEADS):
            acc = acc + w_t[IDX_DIM + h:IDX_DIM + h + 1, :] * jnp.maximum(s[:, h * Q_BLOCK:(h + 1) * Q_BLOCK], 0.0)
        bits = pltpu.bitcast(acc, jnp.int32)
        key = _sortable(bits)
        floor = pltpu.bitcast((bits + ((bits >> 31) & jnp.int32(0xFFFF))) & jnp.int32(-0x10000), F32)
        low = high = key
        if diagonal:
            causal = kpos + j * IDX_TILE <= qpos
            low = jnp.where(causal, key, jnp.int32(INT_MIN))
            high = jnp.where(causal, key, int_max)
            floor = jnp.where(causal, floor, -jnp.inf)
        key_ref[tile(j), :] = low
        floor_ref[tile(j), :] = floor.astype(BF16)
        kmax = jnp.maximum(kmax, _fold_rows(low, SUBLANES, jnp.maximum))
        kmin = jnp.minimum(kmin, _fold_rows(high, SUBLANES, jnp.minimum))
        return kmax, kmin

    assert IDX_TILE >= Q_BLOCK
    extremes = lax.fori_loop(0, n_idx - 1, functools.partial(idx_body, diagonal=False),
                             (jnp.full((SUBLANES, Q_BLOCK), INT_MIN, jnp.int32),
                              jnp.full((SUBLANES, Q_BLOCK), 2 ** 31 - 1, jnp.int32)))
    kmax, kmin = idx_body(n_idx - 1, extremes, diagonal=True)
    kmax = _reduce_rows(kmax, jnp.maximum)
    kmin = _reduce_rows(kmin, jnp.minimum)

    def count(pred):
        def body(j, cnt):
            m = jnp.where(pred(key_ref[tile(j), :], kpos + j * IDX_TILE), 1.0, 0.0)
            return cnt + _fold_rows(m, SUBLANES)
        cnt = lax.fori_loop(0, n_idx, body, jnp.zeros((SUBLANES, Q_BLOCK), F32))
        return jnp.sum(cnt, axis=0, keepdims=True)

    n_valid = (qpos[0:1, :] + 1).astype(F32)
    few = n_valid <= kf
    flat = kmin == kmax
    start = few | flat
    thr0 = jnp.where(few, jnp.int32(INT_MIN + 1), kmin)
    state = (jnp.where(start, thr0, kmin), jnp.where(start, thr0, kmax + 1), n_valid)

    def search_pass(it, st):
        lo, hi, c_lo = st
        f_lo = pltpu.bitcast(_sortable(lo), F32)
        f_hi = pltpu.bitcast(_sortable(hi - 1), F32)
        piv = _sortable(pltpu.bitcast(0.5 * f_lo + 0.5 * f_hi, jnp.int32))
        piv = jnp.where(it % 4 == 3, lo + lax.shift_right_logical(hi - lo, 1), piv)
        piv = jnp.where(it == 0, jnp.int32(1), piv)
        piv = jnp.where(it == 1, jnp.where(hi == 1, jnp.int32(-1), piv), piv)
        piv = jnp.minimum(jnp.maximum(piv, lo + 1), hi - 1)
        c = count(lambda kt, pos: kt >= piv)
        live = lo != hi
        ge = c >= kf
        hit = c == kf
        lo_n = jnp.where(live & ge, piv, lo)
        c_lo_n = jnp.where(live & ge, c, c_lo)
        hi_n = jnp.where(live & jnp.logical_not(ge), piv, hi)
        hi_n = jnp.where(live & (hit | (hi_n - lo_n == 1)), lo_n, hi_n)
        return lo_n, hi_n, c_lo_n

    def count_coarse(p):
        pk = 2 * SUBLANES
        def body(j, cnt):
            m = jnp.where(floor_ref[tile(j), :] >= p, jnp.bfloat16(1), jnp.bfloat16(0))
            return cnt + _fold_rows(m, pk).astype(F32)
        cnt = lax.fori_loop(0, n_idx, body, jnp.zeros((pk, Q_BLOCK), F32))
        return jnp.sum(cnt, axis=0, keepdims=True)

    def coarse_pass(it, st):
        lo, hi, c_lo = st
        f_lo = pltpu.bitcast(_sortable(lo), F32)
        f_hi = pltpu.bitcast(_sortable(hi - 1), F32)
        p = (0.5 * f_lo + 0.5 * f_hi).astype(BF16)
        p_bits = pltpu.bitcast(p.astype(F32), jnp.int32)
        piv = _sortable(p_bits)
        expo = p_bits & jnp.int32(0x7F800000)
        ok = (piv > lo) & (piv < hi) & (expo != 0) & (expo != jnp.int32(0x7F800000))
        c = count_coarse(p)
        live = (lo != hi) & ok
        ge = c >= kf
        hit = c == kf
        lo_n = jnp.where(live & ge, piv, lo)
        c_lo_n = jnp.where(live & ge, c, c_lo)
        hi_n = jnp.where(live & jnp.logical_not(ge), piv, hi)
        hi_n = jnp.where(live & (hit | (hi_n - lo_n == 1)), lo_n, hi_n)
        return lo_n, hi_n, c_lo_n

    state = lax.fori_loop(0, 2, search_pass, state)
    state = lax.fori_loop(2, 2 + COARSE_PASSES, coarse_pass, state)
    state = lax.fori_loop(2 + COARSE_PASSES, SEARCH_PASSES, search_pass, state)

    def pending(st):
        return jnp.max(jnp.where(st[0] != st[1], 1.0, 0.0))

    def more_cond(carry):
        return jnp.logical_and(carry[1] > 0.0, carry[0] < 1024)

    def more_body(carry):
        st = search_pass(carry[0] + 1, search_pass(carry[0], carry[2:]))
        return (carry[0] + 2, pending(st)) + st

    thr, _, c_ge = lax.while_loop(more_cond, more_body, (jnp.int32(SEARCH_PASSES), pending(state)) + state)[2:]
    tied = jnp.max(c_ge) > kf

    def write_sel(pred):
        def body(j, carry):
            sel = pred(key_ref[tile(j), :], kpos + j * IDX_TILE)
            sel_ref[tile(j), :] = jnp.where(sel, 1.0, 0.0).astype(BF16)
            return carry
        lax.fori_loop(0, n_idx, body, 0)

    @pl.when(jnp.logical_not(tied))
    def _():
        write_sel(lambda kt, pos: kt >= thr)

    @pl.when(tied)
    def _():
        need = kf - count(lambda kt, pos: kt > thr)
        nbits = max(1, int(np.ceil(np.log2(seq))))

        def pos_body(p, lim):
            cand = lim + lax.shift_left(jnp.int32(1), nbits - 1 - p)
            c = count(lambda kt, pos: jnp.where(kt == thr, pos, jnp.int32(seq)) < cand)
            return jnp.where(c < need, cand, lim)

        lim = lax.fori_loop(0, nbits, pos_body, jnp.zeros((1, Q_BLOCK), jnp.int32))
        lim = jnp.where(c_ge > kf, lim, jnp.int32(seq))
        write_sel(lambda kt, pos: (kt > thr) | ((kt == thr) & (pos <= lim)))

    rows = Q_PER_KV * Q_BLOCK
    pk = 2 * SUBLANES
    qgs = [jnp.concatenate([q_ref[:, (Q_PER_KV * g + e) * LANES:(Q_PER_KV * g + e + 1) * LANES]
                            for e in range(Q_PER_KV)], axis=0) for g in range(N_KV_HEADS)]
    ones_rows = jnp.ones((pk, ATT_TILE), BF16)

    def logits(j, g):
        ks = pl.ds(pl.multiple_of(j * ATT_TILE, ATT_TILE), ATT_TILE)
        return lax.dot_general(k_ref[ks, g * LANES:(g + 1) * LANES], qgs[g], _NT, preferred_element_type=F32)

    def weighted_v(j, g, p):
        ks = pl.ds(pl.multiple_of(j * ATT_TILE, ATT_TILE), ATT_TILE)
        sel = jnp.concatenate([sel_ref[ks, :]] * Q_PER_KV, axis=1)
        v_aug = jnp.concatenate([vt_ref[g * LANES:(g + 1) * LANES, ks], ones_rows], axis=0)
        return _dot(v_aug, p.astype(BF16) * sel)

    shifts = [jnp.max(lax.dot_general(k_ref[0:SHIFT_KEYS, g * LANES:(g + 1) * LANES], qgs[g], _NT,
                                      preferred_element_type=F32), axis=0, keepdims=True)
              for g in range(N_KV_HEADS)]

    acc_ref[...] = jnp.zeros_like(acc_ref)

    def fast_body(j, carry):
        for g in range(N_KV_HEADS):
            acc_ref[g] += weighted_v(j, g, jnp.exp2(logits(j, g) - shifts[g]))
        return carry

    lax.fori_loop(0, n_att, fast_body, 0)
    accs = [acc_ref[g] for g in range(N_KV_HEADS)]
    lo_ok, hi_ok = 2.0 ** -60, 2.0 ** 60
    worst = [jnp.max(jnp.where((a[LANES:LANES + 1] >= lo_ok) & (a[LANES:LANES + 1] <= hi_ok), 0.0, 1.0))
             + jnp.max(jnp.where(jnp.abs(a[0:LANES]) < jnp.inf, 0.0, 1.0)) for a in accs]
    redo = (worst[0] + worst[1]) > 0.0

    def store(g, acc):
        o_t = acc[0:LANES] / acc[LANES:LANES + 1]
        for e in range(Q_PER_KV):
            col = (Q_PER_KV * g + e) * LANES
            o_ref[:, col:col + LANES] = o_t[:, e * Q_BLOCK:(e + 1) * Q_BLOCK].T.astype(BF16)

    @pl.when(jnp.logical_not(redo))
    def _():
        for g in range(N_KV_HEADS):
            store(g, accs[g])

    @pl.when(redo)
    def _():
        def safe_body(j, carry):
            ks = pl.ds(pl.multiple_of(j * ATT_TILE, ATT_TILE), ATT_TILE)
            sel = jnp.concatenate([sel_ref[ks, :]] * Q_PER_KV, axis=1).astype(F32)
            out = []
            for g in range(N_KV_HEADS):
                m, acc = carry[g]
                s = logits(j, g) + (sel - 1.0) * (-MASKED)
                m_new = jnp.maximum(m, jnp.max(s, axis=0, keepdims=True))
                out.append((m_new, jnp.exp2(m - m_new) * acc + weighted_v(j, g, jnp.exp2(s - m_new))))
            return tuple(out)

        init = (jnp.full((1, rows), MASKED, F32), jnp.zeros((LANES + pk, rows), F32))
        res = lax.fori_loop(0, n_att, safe_body, (init,) * N_KV_HEADS)
        for g in range(N_KV_HEADS):
            store(g, res[g][1])


def _dsa(q, k, vt, qi, kiwf, kiwb, bsz, seq):
    assert seq % IDX_TILE == 0 and seq % ATT_TILE == 0
    nq = seq // Q_BLOCK
    topk = min(TOPK_MAX, seq // 4)
    qrow = lambda w: pl.BlockSpec((Q_BLOCK, w), lambda b, i: (b * nq + i, 0))
    full = lambda w: pl.BlockSpec((seq, w), lambda b, i: (b, 0))
    return pl.pallas_call(
        functools.partial(_dsa_kernel, topk=topk, seq=seq),
        grid=(bsz, nq),
        in_specs=[qrow(ATTN_WIDTH), qrow(IDX_HEADS * LANES), qrow(LANES), full(KV_WIDTH),
                  pl.BlockSpec((KV_WIDTH, seq), lambda b, i: (0, b)), full(LANES)],
        out_specs=qrow(ATTN_WIDTH),
        out_shape=jax.ShapeDtypeStruct((bsz * seq, ATTN_WIDTH), BF16),
        scratch_shapes=[pltpu.VMEM((seq, Q_BLOCK), jnp.int32), pltpu.VMEM((seq, Q_BLOCK), BF16),
                        pltpu.VMEM((seq, Q_BLOCK), BF16),
                        pltpu.VMEM((N_KV_HEADS, LANES + 2 * SUBLANES, Q_PER_KV * Q_BLOCK), F32)],
        compiler_params=_params("parallel", "arbitrary"),
        name="dsa_attention",
    )(q, qi, kiwf, k, vt, kiwb)


def _split3(x):
    hi = x.astype(BF16)
    r = x - hi.astype(F32)
    mid = r.astype(BF16)
    return hi, mid, (r - mid.astype(F32)).astype(BF16)


def _ssd_kernel(zs_ref, xa_ref, dt_ref, dtb_ref, alog_ref, dexp_ref, nw_ref, e_ref, o_ref, st_ref):
    L = SSD_CHUNK

    @pl.when(pl.program_id(1) == 0)
    def _():
        st_ref[...] = jnp.zeros_like(st_ref)

    for sub in range(SSD_STEP_CHUNKS):
        _ssd_chunk(slice(sub * L, (sub + 1) * L), zs_ref, xa_ref, dt_ref, dtb_ref, alog_ref, dexp_ref, nw_ref, e_ref,
                   o_ref, st_ref)


def _ssd_chunk(rs, zs_ref, xa_ref, dt_ref, dtb_ref, alog_ref, dexp_ref, nw_ref, e_ref, o_ref, st_ref):
    L = SSD_CHUNK
    dtv = dt_ref[rs, :] + dtb_ref[...]
    dt = jnp.maximum(dtv, 0.0) + jnp.log1p(jnp.exp(-jnp.abs(dtv)))
    adt = dt * (-jnp.exp(alog_ref[...]))
    ri = lax.broadcasted_iota(jnp.int32, (L, L), 0)
    ci = lax.broadcasted_iota(jnp.int32, (L, L), 1)
    tril = ci <= ri
    tril_b = jnp.where(tril, 1.0, 0.0).astype(BF16)
    acs = sum(_dot(tril_b, t) for t in _split3(adt))
    last = acs[L - 1:L, :]
    acs_t = acs.T
    dt_t = dt.T
    hi = jnp.concatenate([dt * jnp.exp(last - acs), jnp.exp(acs)], axis=0)
    stack = (hi.astype(BF16), (hi - hi.astype(BF16).astype(F32)).astype(BF16))
    first_half = ci < SSD_HEAD_DIM
    gw = SSD_GROUP_WIDTH

    for g in range(SSD_GROUPS):
        gs = slice(g * gw, (g + 1) * gw)
        b_g = xa_ref[rs, SSD_INNER + g * SSD_STATE:SSD_INNER + (g + 1) * SSD_STATE]
        c_g = xa_ref[rs, SSD_INNER + SSD_GN + g * SSD_STATE:SSD_INNER + SSD_GN + (g + 1) * SSD_STATE]
        b_b, c_b = b_g.astype(BF16), c_g.astype(BF16)
        cb = lax.dot_general(c_b, b_b, _NT, preferred_element_type=F32)
        b_t = b_g.T.astype(BF16)
        ex = sum(_dot(t, e_ref[:, gs]) for t in stack)
        dtdec, eacs = ex[0:L], ex[L:2 * L]
        xs = xa_ref[rs, gs]
        s_in = st_ref[:, gs]
        y = _dot(c_b, s_in.astype(BF16)) * eacs + xs * dexp_ref[:, gs]
        st_ref[:, gs] = s_in * eacs[L - 1:L, :] + _dot(b_t, (xs * dtdec).astype(BF16))
        ys = []
        for c2 in range(gw // LANES):
            xp = xs[:, c2 * LANES:(c2 + 1) * LANES].astype(BF16)
            pair = []
            for hh in range(LANES // SSD_HEAD_DIM):
                h = g * SSD_HEADS_PER_GROUP + c2 * (LANES // SSD_HEAD_DIM) + hh
                seg = acs[:, h:h + 1] - acs_t[h:h + 1, :]
                lm = jnp.exp(jnp.where(tril, seg, -jnp.inf))
                pair.append(_dot((cb * lm * dt_t[h:h + 1, :]).astype(BF16), xp))
            ys.append(jnp.where(first_half, pair[0], pair[1]))
        y = y + jnp.concatenate(ys, axis=1)
        yz = y * zs_ref[rs, gs]
        ms = jnp.mean(yz * yz, axis=1, keepdims=True)
        o_ref[rs, gs] = (yz * lax.rsqrt(ms + NORM_EPS) * nw_ref[:, gs]).astype(BF16)


def _ssd(zs, xa, dtp, dtb, alog, dexp, nw, emat, bsz, seq):
    L = SSD_CHUNK * SSD_STEP_CHUNKS
    assert seq % L == 0
    nc = seq // L
    row = lambda w: pl.BlockSpec((L, w), lambda b, c: (b * nc + c, 0))
    return pl.pallas_call(
        _ssd_kernel,
        grid=(bsz, nc),
        in_specs=[row(SSD_INNER), row(SSD_CONV_DIM), row(LANES),
                  _resident((1, LANES)), _resident((1, LANES)), _resident((1, SSD_INNER)),
                  _resident((1, SSD_INNER)), _resident((LANES, SSD_INNER))],
        out_specs=row(SSD_INNER),
        out_shape=jax.ShapeDtypeStruct((bsz * seq, SSD_INNER), BF16),
        scratch_shapes=[pltpu.VMEM((SSD_STATE, SSD_INNER), F32)],
        compiler_params=_params("parallel", "arbitrary"),
        name="ssd_branch",
    )(zs, xa, dtp, dtb, alog, dexp, nw, emat)


def _merge_kernel(x_ref, a_ref, b_ref, nw_ref, wg_ref, wpa_ref, wps_ref, wo_ref, o_ref):
    x = x_ref[...]
    h = _rms(x, nw_ref[...]).astype(BF16)
    merged = _sigmoid(_dot(h, wg_ref[:, 0:D_MODEL])) * _dot(a_ref[...], wpa_ref[...])
    merged = merged + _sigmoid(_dot(h, wg_ref[:, D_MODEL:2 * D_MODEL])) * _dot(b_ref[...], wps_ref[...])
    o_ref[...] = x + _dot(merged.astype(BF16), wo_ref[...])


def _merge(x2, a, b, nw, wg, wpa, wps, wo):
    tokens = x2.shape[0]
    tm = MERGE_ROWS
    row = lambda w: pl.BlockSpec((tm, w), lambda i: (i, 0))
    return pl.pallas_call(
        _merge_kernel,
        grid=(tokens // tm,),
        in_specs=[row(D_MODEL), row(ATTN_WIDTH), row(SSD_INNER), _resident((1, D_MODEL)),
                  _resident((D_MODEL, 2 * D_MODEL)), _resident((ATTN_WIDTH, D_MODEL)),
                  _resident((SSD_INNER, D_MODEL)), _resident((D_MODEL, D_MODEL))],
        out_specs=row(D_MODEL),
        out_shape=jax.ShapeDtypeStruct((tokens, D_MODEL), F32),
        compiler_params=_params("parallel"),
        name="merge_out",
    )(x2, a, b, nw, wg, wpa, wps, wo)


def _ffn_kernel(x_ref, nw_ref, wup_ref, cw_ref, cb_ref, wdn_ref, fnw_ref, o_ref, act_ref, carry_ref, *, final_norm):
    tm = FFN_ROWS
    tail = SUBLANES
    nf = FFN_DIM // FFN_TILE

    @pl.when(pl.program_id(1) == 0)
    def _():
        carry_ref[...] = jnp.zeros_like(carry_ref)

    x = x_ref[...]
    h = _rms(x, nw_ref[...]).astype(BF16)
    for f in range(nf):
        parts = []
        for part in range(2):
            col = part * FFN_DIM + f * FFN_TILE
            cs = slice(col, col + FFN_TILE)
            ct = part * nf + f
            u = _dot(h, wup_ref[:, cs])
            ue = jnp.concatenate([carry_ref[ct], u], axis=0)
            cv = cb_ref[:, cs] + cw_ref[FFN_CONV - 1:FFN_CONV, cs] * u
            for j in range(FFN_CONV - 1):
                back = FFN_CONV - 1 - j
                cv = cv + cw_ref[j:j + 1, cs] * pltpu.roll(ue, back, 0)[tail:tail + tm]
            carry_ref[ct] = u[tm - tail:tm, :]
            parts.append(cv)
        act_ref[:, f * FFN_TILE:(f + 1) * FFN_TILE] = (parts[0] * _sigmoid(parts[0]) * parts[1]).astype(BF16)
    y = x + _dot(act_ref[...], wdn_ref[...])
    if final_norm:
        y = _rms(y, fnw_ref[...])
    o_ref[...] = y


def _ffn(x2, nw, wup, cw, cb, wdn, fnw, bsz, seq, final_norm):
    tm = FFN_ROWS
    ns = seq // tm
    nf = FFN_DIM // FFN_TILE
    row = pl.BlockSpec((tm, D_MODEL), lambda b, s: (b * ns + s, 0))
    return pl.pallas_call(
        functools.partial(_ffn_kernel, final_norm=final_norm),
        grid=(bsz, ns),
        in_specs=[row, _resident((1, D_MODEL)), _resident((D_MODEL, 2 * FFN_DIM)),
                  _resident((FFN_CONV, 2 * FFN_DIM)), _resident((1, 2 * FFN_DIM)),
                  _resident((FFN_DIM, D_MODEL)), _resident((1, D_MODEL))],
        out_specs=row,
        out_shape=jax.ShapeDtypeStruct((bsz * seq, D_MODEL), F32),
        scratch_shapes=[pltpu.VMEM((tm, FFN_DIM), BF16),
                        pltpu.VMEM((2 * nf, SUBLANES, FFN_TILE), F32)],
        compiler_params=_params("parallel", "arbitrary"),
        name="conv_glu_ffn",
    )(x2, nw, wup, cw, cb, wdn, fnw)


def _split_w_in(w_in):
    o = np.cumsum([0, ATTN_WIDTH, KV_WIDTH, KV_WIDTH, IDX_HEADS * IDX_DIM, IDX_DIM, IDX_HEADS,
                   SSD_INNER, SSD_CONV_DIM, SSD_HEADS, D_MODEL, D_MODEL]).tolist()
    seg = lambda j: w_in[:, o[j]:o[j + 1]]
    d = w_in.shape[0]
    qi = seg(3).reshape(d, IDX_HEADS, IDX_DIM)
    qi = jnp.pad(qi, ((0, 0), (0, 0), (0, LANES - IDX_DIM))).reshape(d, IDX_HEADS * LANES)
    kiw = jnp.pad(jnp.concatenate([seg(4), seg(5)], axis=1), ((0, 0), (0, LANES - IDX_DIM - IDX_HEADS)))
    w_a = jnp.concatenate([seg(0), seg(1), qi, kiw], axis=1).astype(BF16)
    w_vt = seg(2).T.astype(BF16)
    dt = jnp.pad(seg(8), ((0, 0), (0, LANES - SSD_HEADS)))
    w_b = jnp.concatenate([seg(6), seg(7), dt], axis=1).astype(BF16)
    w_g = jnp.concatenate([seg(9), seg(10)], axis=1).astype(BF16)
    return w_a, w_vt, w_b, w_g


def _lane_pad_row(v):
    return jnp.pad(v, (0, LANES - v.shape[0]))[None, :]


def kernel(x, norm_mix_w, w_in, ssd_conv_w, ssd_conv_b, ssd_dt_bias, ssd_a_log, ssd_d, ssd_norm_w, w_proj_attn, w_proj_ssd, w_out, norm_ffn_w, ffn_w_up, ffn_conv_w, ffn_conv_b, ffn_w_down, norm_final_w):
    bsz, seq, d = x.shape
    depth = w_in.shape[0]
    assert d == D_MODEL and seq % max(PROJ_ROWS_A, FFN_ROWS, MERGE_ROWS, IDX_TILE) == 0
    tabs_a = _rope_tables(seq, HEAD_DIM // ROPE_FRACTION)
    tabs_i = _rope_tables(seq, IDX_DIM // ROPE_FRACTION)
    emat = (jnp.arange(SSD_INNER)[None, :] // SSD_HEAD_DIM == jnp.arange(LANES)[:, None]).astype(BF16)
    x2 = x.reshape(bsz * seq, d)
    for l in range(depth):
        w_a, w_vt, w_b, w_g = _split_w_in(w_in[l])
        nw = norm_mix_w[l][None, :]
        q, k, vt, qi, kiwf, kiwb = _proj_attn(x2, nw, w_a, w_vt, tabs_a, tabs_i, seq)
        a = _dsa(q, k, vt, qi, kiwf, kiwb, bsz, seq)
        zs, xa, dtp = _proj_ssd(x2, nw, w_b, ssd_conv_w[l], ssd_conv_b[l][None, :], bsz, seq)
        b = _ssd(zs, xa, dtp, _lane_pad_row(ssd_dt_bias[l]), _lane_pad_row(ssd_a_log[l]),
                 jnp.repeat(ssd_d[l], SSD_HEAD_DIM)[None, :], ssd_norm_w[l][None, :], emat, bsz, seq)
        x2 = _merge(x2, a, b, nw, w_g, w_proj_attn[l].astype(BF16), w_proj_ssd[l].astype(BF16),
                    w_out[l].astype(BF16))
        x2 = _ffn(x2, norm_ffn_w[l][None, :], ffn_w_up[l].astype(BF16), ffn_conv_w[l], ffn_conv_b[l][None, :],
                  ffn_w_down[l].astype(BF16), norm_final_w[None, :], bsz, seq, final_norm=(l == depth - 1))
    return x2.reshape(bsz, seq, d)
```

```python
import functools

import numpy as np
import jax
import jax.numpy as jnp
from jax import lax
from jax.experimental import pallas as pl
from jax.experimental.pallas import tpu as pltpu

D_MODEL = 1024
N_HEADS = 8
HEAD_DIM = 128
N_KV_HEADS = 2
Q_PER_KV = N_HEADS // N_KV_HEADS
ATTN_WIDTH = N_HEADS * HEAD_DIM
KV_WIDTH = N_KV_HEADS * HEAD_DIM
IDX_HEADS = 8
IDX_DIM = 64
TOPK_MAX = 256
ROPE_THETA = 500000.0
ROPE_FRACTION = 4
SSD_INNER = 2 * D_MODEL
SSD_HEAD_DIM = 64
SSD_HEADS = SSD_INNER // SSD_HEAD_DIM
SSD_GROUPS = 4
SSD_HEADS_PER_GROUP = SSD_HEADS // SSD_GROUPS
SSD_STATE = 128
SSD_CONV = 4
SSD_CHUNK = 128
SSD_GN = SSD_GROUPS * SSD_STATE
SSD_CONV_DIM = SSD_INNER + 2 * SSD_GN
SSD_GROUP_WIDTH = SSD_INNER // SSD_GROUPS
FFN_DIM = 2816
FFN_CONV = 3
NORM_EPS = 1e-6

LANES = 128
SUBLANES = 8
VMEM_LIMIT = 56 * 1024 * 1024
Q_BLOCK = 256
IDX_TILE = 512
ATT_TILE = 512
SEARCH_PASSES = 18
COARSE_PASSES = 8
SSD_STEP_CHUNKS = 2
SHIFT_KEYS = 128
PROJ_ROWS_A = 512
PROJ_ROWS_B = 256
MERGE_ROWS = 512
FFN_ROWS = 512
FFN_TILE = 256
MASKED = -1e30
LOG2_E = 1.4426950408889634
INT_MIN = -(2 ** 31)
MIN_NORMAL = 2.0 ** -126
LOWEST_KEY = int(np.array(np.finfo(np.float32).min, np.float32).view(np.int32)) ^ 0x7FFFFFFF

F32 = jnp.float32
BF16 = jnp.bfloat16
_NT = (((1,), (1,)), ((), ()))


def _params(*sem):
    return pltpu.CompilerParams(dimension_semantics=sem, vmem_limit_bytes=VMEM_LIMIT)


def _resident(shape):
    nd = len(shape)
    return pl.BlockSpec(shape, lambda *_: (0,) * nd, pipeline_mode=pl.Buffered(1))


def _rms(x, w):
    return x * lax.rsqrt(jnp.mean(x * x, axis=-1, keepdims=True) + NORM_EPS) * w


def _sigmoid(x):
    return 1.0 / (1.0 + jnp.exp(-x))


def _dot(a, b):
    return jnp.dot(a, b, preferred_element_type=F32)


def _fold_rows(x, rows, op=jnp.add):
    parts = [x[r:r + rows] for r in range(0, x.shape[0], rows)]
    while len(parts) > 1:
        parts = [op(parts[a], parts[a + 1]) for a in range(0, len(parts) - 1, 2)] + parts[len(parts) & ~1:]
    return parts[0]


def _rope_tables(seq, rot_dim):
    half = rot_dim // 2
    inv = ROPE_THETA ** (-jnp.arange(0, rot_dim, 2, dtype=F32) / rot_dim)
    ang = jnp.arange(seq, dtype=F32)[:, None] * inv[None, :]
    cos, sin = jnp.cos(ang), jnp.sin(ang)
    pad = LANES - rot_dim
    c = jnp.concatenate([cos, cos, jnp.ones((seq, pad), F32)], axis=1)
    s1 = jnp.concatenate([-sin, jnp.zeros((seq, half + pad), F32)], axis=1)
    s2 = jnp.concatenate([jnp.zeros((seq, half), F32), sin, jnp.zeros((seq, pad), F32)], axis=1)
    return c, s1, s2


def _rope(t, c, s1, s2, half):
    return t * c + pltpu.roll(t, LANES - half, 1) * s1 + pltpu.roll(t, half, 1) * s2


_A_Q = 0
_A_K = _A_Q + ATTN_WIDTH
_A_QI = _A_K + KV_WIDTH
_A_KIW = _A_QI + IDX_HEADS * LANES
_A_COLS = _A_KIW + LANES


def _proj_attn_kernel(x_ref, nw_ref, w_ref, wvt_ref, ca_ref, sa1_ref, sa2_ref, ci_ref, si1_ref, si2_ref,
                      q_ref, k_ref, vt_ref, qi_ref, kiwf_ref, kiwb_ref):
    h = _rms(x_ref[...], nw_ref[...]).astype(BF16)
    ca, sa1, sa2 = ca_ref[...], sa1_ref[...], sa2_ref[...]
    ci, si1, si2 = ci_ref[...], si1_ref[...], si2_ref[...]
    ha = HEAD_DIM // ROPE_FRACTION // 2
    hi = IDX_DIM // ROPE_FRACTION // 2
    scale = HEAD_DIM ** -0.5 * LOG2_E
    t = _dot(h, w_ref[:, _A_Q:_A_Q + ATTN_WIDTH])
    for j in range(N_HEADS):
        js = slice(j * LANES, (j + 1) * LANES)
        q_ref[:, js] = (_rope(t[:, js], ca, sa1, sa2, ha) * scale).astype(BF16)
    t = _dot(h, w_ref[:, _A_K:_A_K + KV_WIDTH])
    for j in range(N_KV_HEADS):
        js = slice(j * LANES, (j + 1) * LANES)
        k_ref[:, js] = _rope(t[:, js], ca, sa1, sa2, ha).astype(BF16)
    vt_ref[...] = lax.dot_general(wvt_ref[...], h, _NT, preferred_element_type=F32).astype(BF16)
    t = _dot(h, w_ref[:, _A_QI:_A_QI + IDX_HEADS * LANES])
    for j in range(IDX_HEADS):
        js = slice(j * LANES, (j + 1) * LANES)
        qi_ref[:, js] = _rope(t[:, js], ci, si1, si2, hi).astype(BF16)
    t = _rope(_dot(h, w_ref[:, _A_KIW:_A_KIW + LANES]), ci, si1, si2, hi)
    kiwf_ref[...] = t
    kiwb_ref[...] = t.astype(BF16)


def _proj_attn(x2, nw, w_a, w_vt, tabs_a, tabs_i, seq):
    tokens = x2.shape[0]
    tm = PROJ_ROWS_A
    per_seq = seq // tm
    row = lambda w: pl.BlockSpec((tm, w), lambda i: (i, 0))
    tab = pl.BlockSpec((tm, LANES), lambda i: (i % per_seq, 0))
    sds = jax.ShapeDtypeStruct
    return pl.pallas_call(
        _proj_attn_kernel,
        grid=(tokens // tm,),
        in_specs=[row(D_MODEL), _resident((1, D_MODEL)), _resident((D_MODEL, _A_COLS)),
                  _resident((KV_WIDTH, D_MODEL))] + [tab] * 6,
        out_specs=[row(ATTN_WIDTH), row(KV_WIDTH), pl.BlockSpec((KV_WIDTH, tm), lambda i: (0, i)),
                   row(IDX_HEADS * LANES), row(LANES), row(LANES)],
        out_shape=[sds((tokens, ATTN_WIDTH), BF16), sds((tokens, KV_WIDTH), BF16), sds((KV_WIDTH, tokens), BF16),
                   sds((tokens, IDX_HEADS * LANES), BF16), sds((tokens, LANES), F32), sds((tokens, LANES), BF16)],
        compiler_params=_params("parallel"),
        name="proj_attn",
    )(x2, nw, w_a, w_vt, *tabs_a, *tabs_i)


_B_Z = 0
_B_XBC = _B_Z + SSD_INNER
_B_DT = _B_XBC + SSD_CONV_DIM
_B_COLS = _B_DT + LANES


def _proj_ssd_kernel(x_ref, nw_ref, w_ref, cw_ref, cb_ref, zs_ref, xa_ref, dt_ref, ext_ref):
    tm = PROJ_ROWS_B
    tail = SUBLANES

    @pl.when(pl.program_id(1) == 0)
    def _():
        ext_ref[0:tail, :] = jnp.zeros((tail, SSD_CONV_DIM), F32)

    h = _rms(x_ref[...], nw_ref[...]).astype(BF16)
    dt_ref[...] = _dot(h, w_ref[:, _B_DT:_B_DT + LANES])
    cstep = 512
    z_chunks = list(range(0, SSD_INNER, cstep))
    for c0 in range(0, SSD_CONV_DIM, cstep):
        cs = slice(c0, c0 + cstep)
        if z_chunks:
            z0 = z_chunks.pop(0)
            z = _dot(h, w_ref[:, _B_Z + z0:_B_Z + z0 + cstep])
            zs_ref[:, z0:z0 + cstep] = z * _sigmoid(z)
        ext_ref[tail:tail + tm, cs] = _dot(h, w_ref[:, _B_XBC + c0:_B_XBC + c0 + cstep])
        for c1 in range(c0, c0 + cstep, LANES):
            cl = slice(c1, c1 + LANES)
            ue = ext_ref[:, cl]
            u = cb_ref[:, cl] + cw_ref[SSD_CONV - 1:SSD_CONV, cl] * ue[tail:tail + tm]
            for j in range(SSD_CONV - 1):
                back = SSD_CONV - 1 - j
                u = u + cw_ref[j:j + 1, cl] * pltpu.roll(ue, back, 0)[tail:tail + tm]
            xa_ref[:, cl] = u * _sigmoid(u)
            ext_ref[0:tail, cl] = ue[tm:tm + tail]


def _proj_ssd(x2, nw, w_b, cw, cb, bsz, seq):
    tm = PROJ_ROWS_B
    ns = seq // tm
    row = lambda w: pl.BlockSpec((tm, w), lambda b, s: (b * ns + s, 0))
    widths = (SSD_INNER, SSD_CONV_DIM, LANES)
    return pl.pallas_call(
        _proj_ssd_kernel,
        grid=(bsz, ns),
        in_specs=[row(D_MODEL), _resident((1, D_MODEL)), _resident((D_MODEL, _B_COLS)),
                  _resident((SSD_CONV, SSD_CONV_DIM)), _resident((1, SSD_CONV_DIM))],
        out_specs=[row(w) for w in widths],
        out_shape=[jax.ShapeDtypeStruct((bsz * seq, w), F32) for w in widths],
        scratch_shapes=[pltpu.VMEM((tm + SUBLANES, SSD_CONV_DIM), F32)],
        compiler_params=_params("parallel", "arbitrary"),
        name="proj_ssd",
    )(x2, nw, w_b, cw, cb)


def _sortable(x):
    return x ^ ((x >> 31) & jnp.int32(0x7FFFFFFF))


def _reduce_rows(x, op):
    for sh in (4, 2, 1):
        x = op(x, pltpu.roll(x, sh, 0))
    return x[0:1, :]


def _dsa_kernel(q_ref, qi_ref, wq_ref, k_ref, vt_ref, kib_ref, o_ref, score_ref, floor_ref, sel_ref, acc_ref, *,
                topk, seq):
    i = pl.program_id(1)
    n_idx = (i + IDX_TILE // Q_BLOCK) // (IDX_TILE // Q_BLOCK)
    n_att = (i + ATT_TILE // Q_BLOCK) // (ATT_TILE // Q_BLOCK)
    kpos = lax.broadcasted_iota(jnp.int32, (IDX_TILE, Q_BLOCK), 0)
    qpos = lax.broadcasted_iota(jnp.int32, (IDX_TILE, Q_BLOCK), 1) + i * Q_BLOCK
    w_t = wq_ref[...].T
    qi_all = jnp.concatenate([qi_ref[:, h * LANES:(h + 1) * LANES] for h in range(IDX_HEADS)], axis=0)
    kf = float(topk)
    int_max = jnp.int32(2 ** 31 - 1)

    def tile(j):
        return pl.ds(pl.multiple_of(j * IDX_TILE, IDX_TILE), IDX_TILE)

    def idx_body(j, carry, diagonal):
        kmax, kmin = carry
        s = lax.dot_general(kib_ref[tile(j), :], qi_all, _NT, preferred_element_type=F32)
        acc = jnp.zeros((IDX_TILE, Q_BLOCK), F32)
        for h in range(IDX_HEADS):
            acc = acc + w_t[IDX_DIM + h:IDX_DIM + h + 1, :] * jnp.maximum(s[:, h * Q_BLOCK:(h + 1) * Q_BLOCK], 0.0)
        key = _sortable(pltpu.bitcast(acc, jnp.int32))
        near = acc.astype(BF16).astype(F32)
        near_bits = pltpu.bitcast(near, jnp.int32)
        below = pltpu.bitcast(near_bits + jnp.where(near_bits < 0, jnp.int32(0x10000), jnp.int32(-0x10000)), F32)
        floor = jnp.where(near > acc, jnp.where(near == 0.0, -MIN_NORMAL, below), near)
        score = acc
        low = high = key
        if diagonal:
            causal = kpos + j * IDX_TILE <= qpos
            low = jnp.where(causal, key, jnp.int32(INT_MIN))
            high = jnp.where(causal, key, int_max)
            floor = jnp.where(causal, floor, -jnp.inf)
            score = jnp.where(causal, acc, -jnp.inf)
        score_ref[tile(j), :] = score
        floor_ref[tile(j), :] = floor.astype(BF16)
        kmax = jnp.maximum(kmax, _fold_rows(low, SUBLANES, jnp.maximum))
        kmin = jnp.minimum(kmin, _fold_rows(high, SUBLANES, jnp.minimum))
        return kmax, kmin

    assert IDX_TILE >= Q_BLOCK
    extremes = lax.fori_loop(0, n_idx - 1, functools.partial(idx_body, diagonal=False),
                             (jnp.full((SUBLANES, Q_BLOCK), INT_MIN, jnp.int32),
                              jnp.full((SUBLANES, Q_BLOCK), 2 ** 31 - 1, jnp.int32)))
    kmax, kmin = idx_body(n_idx - 1, extremes, diagonal=True)
    kmax = _reduce_rows(kmax, jnp.maximum)
    kmin = _reduce_rows(kmin, jnp.minimum)

    def as_score(key):
        return pltpu.bitcast(_sortable(key), F32)

    def count(pred):
        def body(j, cnt):
            m = jnp.where(pred(score_ref[tile(j), :], kpos + j * IDX_TILE), 1.0, 0.0)
            return cnt + _fold_rows(m, SUBLANES)
        cnt = lax.fori_loop(0, n_idx, body, jnp.zeros((SUBLANES, Q_BLOCK), F32))
        return jnp.sum(cnt, axis=0, keepdims=True)

    n_valid = (qpos[0:1, :] + 1).astype(F32)
    few = n_valid <= kf
    flat = kmin == kmax
    start = few | flat
    thr0 = jnp.where(few, jnp.int32(LOWEST_KEY), kmin)
    state = (jnp.where(start, thr0, kmin), jnp.where(start, thr0, kmax + 1), n_valid)

    def search_pass(it, st):
        lo, hi, c_lo = st
        f_lo = pltpu.bitcast(_sortable(lo), F32)
        f_hi = pltpu.bitcast(_sortable(hi - 1), F32)
        piv = _sortable(pltpu.bitcast(0.5 * f_lo + 0.5 * f_hi, jnp.int32))
        piv = jnp.where(it % 4 == 3, lo + lax.shift_right_logical(hi - lo, 1), piv)
        piv = jnp.where(it == 0, jnp.int32(1), piv)
        piv = jnp.where(it == 1, jnp.where(hi == 1, jnp.int32(-1), piv), piv)
        piv = jnp.minimum(jnp.maximum(piv, lo + 1), hi - 1)
        piv_f = as_score(piv)
        c = count(lambda st, pos: st >= piv_f)
        live = lo != hi
        ge = c >= kf
        hit = c == kf
        lo_n = jnp.where(live & ge, piv, lo)
        c_lo_n = jnp.where(live & ge, c, c_lo)
        hi_n = jnp.where(live & jnp.logical_not(ge), piv, hi)
        hi_n = jnp.where(live & (hit | (hi_n - lo_n == 1)), lo_n, hi_n)
        return lo_n, hi_n, c_lo_n

    def count_coarse(p):
        pk = 2 * SUBLANES
        def body(j, cnt):
            m = jnp.where(floor_ref[tile(j), :] >= p, jnp.bfloat16(1), jnp.bfloat16(0))
            return cnt + _fold_rows(m, pk).astype(F32)
        cnt = lax.fori_loop(0, n_idx, body, jnp.zeros((pk, Q_BLOCK), F32))
        return jnp.sum(cnt, axis=0, keepdims=True)

    def coarse_pass(it, st):
        lo, hi, c_lo = st
        f_lo = pltpu.bitcast(_sortable(lo), F32)
        f_hi = pltpu.bitcast(_sortable(hi - 1), F32)
        p = (0.5 * f_lo + 0.5 * f_hi).astype(BF16)
        p_bits = pltpu.bitcast(p.astype(F32), jnp.int32)
        piv = _sortable(p_bits)
        expo = p_bits & jnp.int32(0x7F800000)
        ok = (piv > lo) & (piv < hi) & (expo != 0) & (expo != jnp.int32(0x7F800000))
        c = count_coarse(p)
        live = (lo != hi) & ok
        ge = c >= kf
        hit = c == kf
        lo_n = jnp.where(live & ge, piv, lo)
        c_lo_n = jnp.where(live & ge, c, c_lo)
        hi_n = jnp.where(live & jnp.logical_not(ge), piv, hi)
        hi_n = jnp.where(live & (hit | (hi_n - lo_n == 1)), lo_n, hi_n)
        return lo_n, hi_n, c_lo_n

    state = lax.fori_loop(0, 2, search_pass, state)
    state = lax.fori_loop(2, 2 + COARSE_PASSES, coarse_pass, state)
    state = lax.fori_loop(2 + COARSE_PASSES, SEARCH_PASSES, search_pass, state)

    def pending(st):
        return jnp.max(jnp.where(st[0] != st[1], 1.0, 0.0))

    def more_cond(carry):
        return jnp.logical_and(carry[1] > 0.0, carry[0] < 1024)

    def more_body(carry):
        st = search_pass(carry[0] + 1, search_pass(carry[0], carry[2:]))
        return (carry[0] + 2, pending(st)) + st

    thr_key, _, c_ge = lax.while_loop(more_cond, more_body, (jnp.int32(SEARCH_PASSES), pending(state)) + state)[2:]
    thr = as_score(thr_key)
    tied = jnp.max(c_ge) > kf

    def write_sel(pred):
        def body(j, carry):
            sel = pred(score_ref[tile(j), :], kpos + j * IDX_TILE)
            sel_ref[tile(j), :] = jnp.where(sel, 1.0, 0.0).astype(BF16)
            return carry
        lax.fori_loop(0, n_idx, body, 0)

    @pl.when(jnp.logical_not(tied))
    def _():
        write_sel(lambda kt, pos: kt >= thr)

    @pl.when(tied)
    def _():
        need = kf - count(lambda kt, pos: kt > thr)
        nbits = max(1, int(np.ceil(np.log2(seq))))

        def pos_body(p, lim):
            cand = lim + lax.shift_left(jnp.int32(1), nbits - 1 - p)
            c = count(lambda kt, pos: jnp.where(kt == thr, pos, jnp.int32(seq)) < cand)
            return jnp.where(c < need, cand, lim)

        lim = lax.fori_loop(0, nbits, pos_body, jnp.zeros((1, Q_BLOCK), jnp.int32))
        lim = jnp.where(c_ge > kf, lim, jnp.int32(seq))
        write_sel(lambda kt, pos: (kt > thr) | ((kt == thr) & (pos <= lim)))

    rows = Q_PER_KV * Q_BLOCK
    pk = 2 * SUBLANES
    qgs = [jnp.concatenate([q_ref[:, (Q_PER_KV * g + e) * LANES:(Q_PER_KV * g + e + 1) * LANES]
                            for e in range(Q_PER_KV)], axis=0) for g in range(N_KV_HEADS)]
    ones_rows = jnp.ones((pk, ATT_TILE), BF16)

    def logits(j, g):
        ks = pl.ds(pl.multiple_of(j * ATT_TILE, ATT_TILE), ATT_TILE)
        return lax.dot_general(k_ref[ks, g * LANES:(g + 1) * LANES], qgs[g], _NT, preferred_element_type=F32)

    def weighted_v(j, g, p):
        ks = pl.ds(pl.multiple_of(j * ATT_TILE, ATT_TILE), ATT_TILE)
        sel = jnp.concatenate([sel_ref[ks, :]] * Q_PER_KV, axis=1)
        v_aug = jnp.concatenate([vt_ref[g * LANES:(g + 1) * LANES, ks], ones_rows], axis=0)
        return _dot(v_aug, p.astype(BF16) * sel)

    shifts = [jnp.max(lax.dot_general(k_ref[0:SHIFT_KEYS, g * LANES:(g + 1) * LANES], qgs[g], _NT,
                                      preferred_element_type=F32), axis=0, keepdims=True)
              for g in range(N_KV_HEADS)]

    acc_ref[...] = jnp.zeros_like(acc_ref)

    def fast_body(j, carry):
        for g in range(N_KV_HEADS):
            acc_ref[g] += weighted_v(j, g, jnp.exp2(logits(j, g) - shifts[g]))
        return carry

    lax.fori_loop(0, n_att, fast_body, 0)
    accs = [acc_ref[g] for g in range(N_KV_HEADS)]
    lo_ok, hi_ok = 2.0 ** -60, 2.0 ** 60
    worst = [jnp.max(jnp.where((a[LANES:LANES + 1] >= lo_ok) & (a[LANES:LANES + 1] <= hi_ok), 0.0, 1.0))
             + jnp.max(jnp.where(jnp.abs(a[0:LANES]) < jnp.inf, 0.0, 1.0)) for a in accs]
    redo = (worst[0] + worst[1]) > 0.0

    def store(g, acc):
        o_t = acc[0:LANES] / acc[LANES:LANES + 1]
        for e in range(Q_PER_KV):
            col = (Q_PER_KV * g + e) * LANES
            o_ref[:, col:col + LANES] = o_t[:, e * Q_BLOCK:(e + 1) * Q_BLOCK].T.astype(BF16)

    @pl.when(jnp.logical_not(redo))
    def _():
        for g in range(N_KV_HEADS):
            store(g, accs[g])

    @pl.when(redo)
    def _():
        def safe_body(j, carry):
            ks = pl.ds(pl.multiple_of(j * ATT_TILE, ATT_TILE), ATT_TILE)
            sel = jnp.concatenate([sel_ref[ks, :]] * Q_PER_KV, axis=1).astype(F32)
            out = []
            for g in range(N_KV_HEADS):
                m, acc = carry[g]
                s = logits(j, g) + (sel - 1.0) * (-MASKED)
                m_new = jnp.maximum(m, jnp.max(s, axis=0, keepdims=True))
                out.append((m_new, jnp.exp2(m - m_new) * acc + weighted_v(j, g, jnp.exp2(s - m_new))))
            return tuple(out)

        init = (jnp.full((1, rows), MASKED, F32), jnp.zeros((LANES + pk, rows), F32))
        res = lax.fori_loop(0, n_att, safe_body, (init,) * N_KV_HEADS)
        for g in range(N_KV_HEADS):
            store(g, res[g][1])


def _dsa(q, k, vt, qi, kiwf, kiwb, bsz, seq):
    assert seq % IDX_TILE == 0 and seq % Q_BLOCK == 0
    assert ATT_TILE == IDX_TILE
    nq = seq // Q_BLOCK
    topk = min(TOPK_MAX, seq // 4)
    qrow = lambda w: pl.BlockSpec((Q_BLOCK, w), lambda b, i: (b * nq + i, 0))
    full = lambda w: pl.BlockSpec((seq, w), lambda b, i: (b, 0))
    return pl.pallas_call(
        functools.partial(_dsa_kernel, topk=topk, seq=seq),
        grid=(bsz, nq),
        in_specs=[qrow(ATTN_WIDTH), qrow(IDX_HEADS * LANES), qrow(LANES), full(KV_WIDTH),
                  pl.BlockSpec((KV_WIDTH, seq), lambda b, i: (0, b)), full(LANES)],
        out_specs=qrow(ATTN_WIDTH),
        out_shape=jax.ShapeDtypeStruct((bsz * seq, ATTN_WIDTH), BF16),
        scratch_shapes=[pltpu.VMEM((seq, Q_BLOCK), F32), pltpu.VMEM((seq, Q_BLOCK), BF16),
                        pltpu.VMEM((seq, Q_BLOCK), BF16),
                        pltpu.VMEM((N_KV_HEADS, LANES + 2 * SUBLANES, Q_PER_KV * Q_BLOCK), F32)],
        compiler_params=_params("parallel", "arbitrary"),
        name="dsa_attention",
    )(q, qi, kiwf, k, vt, kiwb)


def _split3(x):
    hi = x.astype(BF16)
    r = x - hi.astype(F32)
    mid = r.astype(BF16)
    return hi, mid, (r - mid.astype(F32)).astype(BF16)


def _ssd_kernel(zs_ref, xa_ref, dt_ref, dtb_ref, alog_ref, dexp_ref, nw_ref, e_ref, o_ref, st_ref):
    L = SSD_CHUNK

    @pl.when(pl.program_id(1) == 0)
    def _():
        st_ref[...] = jnp.zeros_like(st_ref)

    for sub in range(SSD_STEP_CHUNKS):
        _ssd_chunk(slice(sub * L, (sub + 1) * L), zs_ref, xa_ref, dt_ref, dtb_ref, alog_ref, dexp_ref, nw_ref, e_ref,
                   o_ref, st_ref)


def _ssd_chunk(rs, zs_ref, xa_ref, dt_ref, dtb_ref, alog_ref, dexp_ref, nw_ref, e_ref, o_ref, st_ref):
    L = SSD_CHUNK
    dtv = dt_ref[rs, :] + dtb_ref[...]
    dt = jnp.maximum(dtv, 0.0) + jnp.log1p(jnp.exp(-jnp.abs(dtv)))
    adt = dt * (-jnp.exp(alog_ref[...]))
    ri = lax.broadcasted_iota(jnp.int32, (L, L), 0)
    ci = lax.broadcasted_iota(jnp.int32, (L, L), 1)
    tril = ci <= ri
    tril_b = jnp.where(tril, 1.0, 0.0).astype(BF16)
    acs = sum(_dot(tril_b, t) for t in _split3(adt))
    last = acs[L - 1:L, :]
    acs_t = acs.T
    dt_t = dt.T
    hi = jnp.concatenate([dt * jnp.exp(last - acs), jnp.exp(acs)], axis=0)
    stack = (hi.astype(BF16), (hi - hi.astype(BF16).astype(F32)).astype(BF16))
    first_half = ci < SSD_HEAD_DIM
    gw = SSD_GROUP_WIDTH

    for g in range(SSD_GROUPS):
        gs = slice(g * gw, (g + 1) * gw)
        b_g = xa_ref[rs, SSD_INNER + g * SSD_STATE:SSD_INNER + (g + 1) * SSD_STATE]
        c_g = xa_ref[rs, SSD_INNER + SSD_GN + g * SSD_STATE:SSD_INNER + SSD_GN + (g + 1) * SSD_STATE]
        b_b, c_b = b_g.astype(BF16), c_g.astype(BF16)
        cb = lax.dot_general(c_b, b_b, _NT, preferred_element_type=F32)
        b_t = b_g.T.astype(BF16)
        ex = sum(_dot(t, e_ref[:, gs]) for t in stack)
        dtdec, eacs = ex[0:L], ex[L:2 * L]
        xs = xa_ref[rs, gs]
        s_in = st_ref[:, gs]
        y = _dot(c_b, s_in.astype(BF16)) * eacs + xs * dexp_ref[:, gs]
        st_ref[:, gs] = s_in * eacs[L - 1:L, :] + _dot(b_t, (xs * dtdec).astype(BF16))
        ys = []
        for c2 in range(gw // LANES):
            xp = xs[:, c2 * LANES:(c2 + 1) * LANES].astype(BF16)
            pair = []
            for hh in range(LANES // SSD_HEAD_DIM):
                h = g * SSD_HEADS_PER_GROUP + c2 * (LANES // SSD_HEAD_DIM) + hh
                seg = acs[:, h:h + 1] - acs_t[h:h + 1, :]
                lm = jnp.exp(jnp.where(tril, seg, -jnp.inf))
                pair.append(_dot((cb * lm * dt_t[h:h + 1, :]).astype(BF16), xp))
            ys.append(jnp.where(first_half, pair[0], pair[1]))
        y = y + jnp.concatenate(ys, axis=1)
        yz = y * zs_ref[rs, gs]
        ms = jnp.mean(yz * yz, axis=1, keepdims=True)
        o_ref[rs, gs] = (yz * lax.rsqrt(ms + NORM_EPS) * nw_ref[:, gs]).astype(BF16)


def _ssd(zs, xa, dtp, dtb, alog, dexp, nw, emat, bsz, seq):
    L = SSD_CHUNK * SSD_STEP_CHUNKS
    assert seq % L == 0
    nc = seq // L
    row = lambda w: pl.BlockSpec((L, w), lambda b, c: (b * nc + c, 0))
    return pl.pallas_call(
        _ssd_kernel,
        grid=(bsz, nc),
        in_specs=[row(SSD_INNER), row(SSD_CONV_DIM), row(LANES),
                  _resident((1, LANES)), _resident((1, LANES)), _resident((1, SSD_INNER)),
                  _resident((1, SSD_INNER)), _resident((LANES, SSD_INNER))],
        out_specs=row(SSD_INNER),
        out_shape=jax.ShapeDtypeStruct((bsz * seq, SSD_INNER), BF16),
        scratch_shapes=[pltpu.VMEM((SSD_STATE, SSD_INNER), F32)],
        compiler_params=_params("parallel", "arbitrary"),
        name="ssd_branch",
    )(zs, xa, dtp, dtb, alog, dexp, nw, emat)


def _merge_kernel(x_ref, a_ref, b_ref, nw_ref, wg_ref, wpa_ref, wps_ref, wo_ref, o_ref):
    x = x_ref[...]
    h = _rms(x, nw_ref[...]).astype(BF16)
    merged = _sigmoid(_dot(h, wg_ref[:, 0:D_MODEL])) * _dot(a_ref[...], wpa_ref[...])
    merged = merged + _sigmoid(_dot(h, wg_ref[:, D_MODEL:2 * D_MODEL])) * _dot(b_ref[...], wps_ref[...])
    o_ref[...] = x + _dot(merged.astype(BF16), wo_ref[...])


def _merge(x2, a, b, nw, wg, wpa, wps, wo):
    tokens = x2.shape[0]
    tm = MERGE_ROWS
    row = lambda w: pl.BlockSpec((tm, w), lambda i: (i, 0))
    return pl.pallas_call(
        _merge_kernel,
        grid=(tokens // tm,),
        in_specs=[row(D_MODEL), row(ATTN_WIDTH), row(SSD_INNER), _resident((1, D_MODEL)),
                  _resident((D_MODEL, 2 * D_MODEL)), _resident((ATTN_WIDTH, D_MODEL)),
                  _resident((SSD_INNER, D_MODEL)), _resident((D_MODEL, D_MODEL))],
        out_specs=row(D_MODEL),
        out_shape=jax.ShapeDtypeStruct((tokens, D_MODEL), F32),
        compiler_params=_params("parallel"),
        name="merge_out",
    )(x2, a, b, nw, wg, wpa, wps, wo)


def _ffn_kernel(x_ref, nw_ref, wup_ref, cw_ref, cb_ref, wdn_ref, fnw_ref, o_ref, act_ref, carry_ref, *, final_norm):
    tm = FFN_ROWS
    tail = SUBLANES
    nf = FFN_DIM // FFN_TILE

    @pl.when(pl.program_id(1) == 0)
    def _():
        carry_ref[...] = jnp.zeros_like(carry_ref)

    x = x_ref[...]
    h = _rms(x, nw_ref[...]).astype(BF16)
    for f in range(nf):
        parts = []
        for part in range(2):
            col = part * FFN_DIM + f * FFN_TILE
            cs = slice(col, col + FFN_TILE)
            ct = part * nf + f
            u = _dot(h, wup_ref[:, cs])
            ue = jnp.concatenate([carry_ref[ct], u], axis=0)
            cv = cb_ref[:, cs] + cw_ref[FFN_CONV - 1:FFN_CONV, cs] * u
            for j in range(FFN_CONV - 1):
                back = FFN_CONV - 1 - j
                cv = cv + cw_ref[j:j + 1, cs] * pltpu.roll(ue, back, 0)[tail:tail + tm]
            carry_ref[ct] = u[tm - tail:tm, :]
            parts.append(cv)
        act_ref[:, f * FFN_TILE:(f + 1) * FFN_TILE] = (parts[0] * _sigmoid(parts[0]) * parts[1]).astype(BF16)
    y = x + _dot(act_ref[...], wdn_ref[...])
    if final_norm:
        y = _rms(y, fnw_ref[...])
    o_ref[...] = y


def _ffn(x2, nw, wup, cw, cb, wdn, fnw, bsz, seq, final_norm):
    tm = FFN_ROWS
    ns = seq // tm
    nf = FFN_DIM // FFN_TILE
    row = pl.BlockSpec((tm, D_MODEL), lambda b, s: (b * ns + s, 0))
    return pl.pallas_call(
        functools.partial(_ffn_kernel, final_norm=final_norm),
        grid=(bsz, ns),
        in_specs=[row, _resident((1, D_MODEL)), _resident((D_MODEL, 2 * FFN_DIM)),
                  _resident((FFN_CONV, 2 * FFN_DIM)), _resident((1, 2 * FFN_DIM)),
                  _resident((FFN_DIM, D_MODEL)), _resident((1, D_MODEL))],
        out_specs=row,
        out_shape=jax.ShapeDtypeStruct((bsz * seq, D_MODEL), F32),
        scratch_shapes=[pltpu.VMEM((tm, FFN_DIM), BF16),
                        pltpu.VMEM((2 * nf, SUBLANES, FFN_TILE), F32)],
        compiler_params=_params("parallel", "arbitrary"),
        name="conv_glu_ffn",
    )(x2, nw, wup, cw, cb, wdn, fnw)


def _split_w_in(w_in):
    o = np.cumsum([0, ATTN_WIDTH, KV_WIDTH, KV_WIDTH, IDX_HEADS * IDX_DIM, IDX_DIM, IDX_HEADS,
                   SSD_INNER, SSD_CONV_DIM, SSD_HEADS, D_MODEL, D_MODEL]).tolist()
    seg = lambda j: w_in[:, o[j]:o[j + 1]]
    d = w_in.shape[0]
    qi = seg(3).reshape(d, IDX_HEADS, IDX_DIM)
    qi = jnp.pad(qi, ((0, 0), (0, 0), (0, LANES - IDX_DIM))).reshape(d, IDX_HEADS * LANES)
    kiw = jnp.pad(jnp.concatenate([seg(4), seg(5)], axis=1), ((0, 0), (0, LANES - IDX_DIM - IDX_HEADS)))
    w_a = jnp.concatenate([seg(0), seg(1), qi, kiw], axis=1).astype(BF16)
    w_vt = seg(2).T.astype(BF16)
    dt = jnp.pad(seg(8), ((0, 0), (0, LANES - SSD_HEADS)))
    w_b = jnp.concatenate([seg(6), seg(7), dt], axis=1).astype(BF16)
    w_g = jnp.concatenate([seg(9), seg(10)], axis=1).astype(BF16)
    return w_a, w_vt, w_b, w_g


def _lane_pad_row(v):
    return jnp.pad(v, (0, LANES - v.shape[0]))[None, :]


def kernel(x, norm_mix_w, w_in, ssd_conv_w, ssd_conv_b, ssd_dt_bias, ssd_a_log, ssd_d, ssd_norm_w, w_proj_attn, w_proj_ssd, w_out, norm_ffn_w, ffn_w_up, ffn_conv_w, ffn_conv_b, ffn_w_down, norm_final_w):
    bsz, seq, d = x.shape
    depth = w_in.shape[0]
    assert d == D_MODEL and seq % max(PROJ_ROWS_A, FFN_ROWS, MERGE_ROWS, IDX_TILE) == 0
    tabs_a = _rope_tables(seq, HEAD_DIM // ROPE_FRACTION)
    tabs_i = _rope_tables(seq, IDX_DIM // ROPE_FRACTION)
    emat = (jnp.arange(SSD_INNER)[None, :] // SSD_HEAD_DIM == jnp.arange(LANES)[:, None]).astype(BF16)
    x2 = x.reshape(bsz * seq, d)
    for l in range(depth):
        w_a, w_vt, w_b, w_g = _split_w_in(w_in[l])
        nw = norm_mix_w[l][None, :]
        q, k, vt, qi, kiwf, kiwb = _proj_attn(x2, nw, w_a, w_vt, tabs_a, tabs_i, seq)
        a = _dsa(q, k, vt, qi, kiwf, kiwb, bsz, seq)
        zs, xa, dtp = _proj_ssd(x2, nw, w_b, ssd_conv_w[l], ssd_conv_b[l][None, :], bsz, seq)
        b = _ssd(zs, xa, dtp, _lane_pad_row(ssd_dt_bias[l]), _lane_pad_row(ssd_a_log[l]),
                 jnp.repeat(ssd_d[l], SSD_HEAD_DIM)[None, :], ssd_norm_w[l][None, :], emat, bsz, seq)
        x2 = _merge(x2, a, b, nw, w_g, w_proj_attn[l].astype(BF16), w_proj_ssd[l].astype(BF16),
                    w_out[l].astype(BF16))
        x2 = _ffn(x2, norm_ffn_w[l][None, :], ffn_w_up[l].astype(BF16), ffn_conv_w[l], ffn_conv_b[l][None, :],
                  ffn_w_down[l].astype(BF16), norm_final_w[None, :], bsz, seq, final_norm=(l == depth - 1))
    return x2.reshape(bsz, seq, d)
```

```python
import functools

import numpy as np
import jax
import jax.numpy as jnp
from jax import lax
from jax.experimental import pallas as pl
from jax.experimental.pallas import tpu as pltpu

D_MODEL = 1024
N_HEADS = 8
HEAD_DIM = 128
N_KV_HEADS = 2
Q_PER_KV = N_HEADS // N_KV_HEADS
ATTN_WIDTH = N_HEADS * HEAD_DIM
KV_WIDTH = N_KV_HEADS * HEAD_DIM
IDX_HEADS = 8
IDX_DIM = 64
TOPK_MAX = 256
ROPE_THETA = 500000.0
ROPE_FRACTION = 4
SSD_INNER = 2 * D_MODEL
SSD_HEAD_DIM = 64
SSD_HEADS = SSD_INNER // SSD_HEAD_DIM
SSD_GROUPS = 4
SSD_HEADS_PER_GROUP = SSD_HEADS // SSD_GROUPS
SSD_STATE = 128
SSD_CONV = 4
SSD_CHUNK = 128
SSD_GN = SSD_GROUPS * SSD_STATE
SSD_CONV_DIM = SSD_INNER + 2 * SSD_GN
SSD_GROUP_WIDTH = SSD_INNER // SSD_GROUPS
FFN_DIM = 2816
FFN_CONV = 3
NORM_EPS = 1e-6

LANES = 128
SUBLANES = 8
VMEM_LIMIT = 56 * 1024 * 1024
Q_BLOCK = 256
IDX_TILE = 512
ATT_TILE = 512
SEARCH_PASSES = 18
COARSE_PASSES = 8
SSD_STEP_CHUNKS = 2
SHIFT_KEYS = 128
PROJ_ROWS_A = 512
PROJ_ROWS_B = 256
MERGE_ROWS = 512
FFN_ROWS = 512
FFN_TILE = 256
MASKED = -1e30
LOG2_E = 1.4426950408889634
INT_MIN = -(2 ** 31)
MIN_NORMAL = 2.0 ** -126
LOWEST_KEY = int(np.array(np.finfo(np.float32).min, np.float32).view(np.int32)) ^ 0x7FFFFFFF
TINY_KEY = 0x00800000

F32 = jnp.float32
BF16 = jnp.bfloat16
_NT = (((1,), (1,)), ((), ()))


def _params(*sem):
    return pltpu.CompilerParams(dimension_semantics=sem, vmem_limit_bytes=VMEM_LIMIT)


def _resident(shape):
    nd = len(shape)
    return pl.BlockSpec(shape, lambda *_: (0,) * nd, pipeline_mode=pl.Buffered(1))


def _rms(x, w):
    return x * lax.rsqrt(jnp.mean(x * x, axis=-1, keepdims=True) + NORM_EPS) * w


def _sigmoid(x):
    return 1.0 / (1.0 + jnp.exp(-x))


def _dot(a, b):
    return jnp.dot(a, b, preferred_element_type=F32)


def _fold_rows(x, rows, op=jnp.add):
    parts = [x[r:r + rows] for r in range(0, x.shape[0], rows)]
    while len(parts) > 1:
        parts = [op(parts[a], parts[a + 1]) for a in range(0, len(parts) - 1, 2)] + parts[len(parts) & ~1:]
    return parts[0]


def _rope_tables(seq, rot_dim):
    half = rot_dim // 2
    inv = ROPE_THETA ** (-jnp.arange(0, rot_dim, 2, dtype=F32) / rot_dim)
    ang = jnp.arange(seq, dtype=F32)[:, None] * inv[None, :]
    cos, sin = jnp.cos(ang), jnp.sin(ang)
    pad = LANES - rot_dim
    c = jnp.concatenate([cos, cos, jnp.ones((seq, pad), F32)], axis=1)
    s1 = jnp.concatenate([-sin, jnp.zeros((seq, half + pad), F32)], axis=1)
    s2 = jnp.concatenate([jnp.zeros((seq, half), F32), sin, jnp.zeros((seq, pad), F32)], axis=1)
    return c, s1, s2


def _rope(t, c, s1, s2, half):
    return t * c + pltpu.roll(t, LANES - half, 1) * s1 + pltpu.roll(t, half, 1) * s2


_A_Q = 0
_A_K = _A_Q + ATTN_WIDTH
_A_QI = _A_K + KV_WIDTH
_A_KIW = _A_QI + IDX_HEADS * LANES
_A_COLS = _A_KIW + LANES


def _proj_attn_kernel(x_ref, nw_ref, w_ref, wvt_ref, ca_ref, sa1_ref, sa2_ref, ci_ref, si1_ref, si2_ref,
                      q_ref, k_ref, vt_ref, qi_ref, kiwf_ref, kiwb_ref):
    h = _rms(x_ref[...], nw_ref[...]).astype(BF16)
    ca, sa1, sa2 = ca_ref[...], sa1_ref[...], sa2_ref[...]
    ci, si1, si2 = ci_ref[...], si1_ref[...], si2_ref[...]
    ha = HEAD_DIM // ROPE_FRACTION // 2
    hi = IDX_DIM // ROPE_FRACTION // 2
    scale = HEAD_DIM ** -0.5 * LOG2_E
    t = _dot(h, w_ref[:, _A_Q:_A_Q + ATTN_WIDTH])
    for j in range(N_HEADS):
        js = slice(j * LANES, (j + 1) * LANES)
        q_ref[:, js] = (_rope(t[:, js], ca, sa1, sa2, ha) * scale).astype(BF16)
    t = _dot(h, w_ref[:, _A_K:_A_K + KV_WIDTH])
    for j in range(N_KV_HEADS):
        js = slice(j * LANES, (j + 1) * LANES)
        k_ref[:, js] = _rope(t[:, js], ca, sa1, sa2, ha).astype(BF16)
    vt_ref[...] = lax.dot_general(wvt_ref[...], h, _NT, preferred_element_type=F32).astype(BF16)
    t = _dot(h, w_ref[:, _A_QI:_A_QI + IDX_HEADS * LANES])
    for j in range(IDX_HEADS):
        js = slice(j * LANES, (j + 1) * LANES)
        qi_ref[:, js] = _rope(t[:, js], ci, si1, si2, hi).astype(BF16)
    t = _rope(_dot(h, w_ref[:, _A_KIW:_A_KIW + LANES]), ci, si1, si2, hi)
    kiwf_ref[...] = t
    kiwb_ref[...] = t.astype(BF16)


def _proj_attn(x2, nw, w_a, w_vt, tabs_a, tabs_i, seq):
    tokens = x2.shape[0]
    tm = PROJ_ROWS_A
    per_seq = seq // tm
    row = lambda w: pl.BlockSpec((tm, w), lambda i: (i, 0))
    tab = pl.BlockSpec((tm, LANES), lambda i: (i % per_seq, 0))
    sds = jax.ShapeDtypeStruct
    return pl.pallas_call(
        _proj_attn_kernel,
        grid=(tokens // tm,),
        in_specs=[row(D_MODEL), _resident((1, D_MODEL)), _resident((D_MODEL, _A_COLS)),
                  _resident((KV_WIDTH, D_MODEL))] + [tab] * 6,
        out_specs=[row(ATTN_WIDTH), row(KV_WIDTH), pl.BlockSpec((KV_WIDTH, tm), lambda i: (0, i)),
                   row(IDX_HEADS * LANES), row(LANES), row(LANES)],
        out_shape=[sds((tokens, ATTN_WIDTH), BF16), sds((tokens, KV_WIDTH), BF16), sds((KV_WIDTH, tokens), BF16),
                   sds((tokens, IDX_HEADS * LANES), BF16), sds((tokens, LANES), F32), sds((tokens, LANES), BF16)],
        compiler_params=_params("parallel"),
        name="proj_attn",
    )(x2, nw, w_a, w_vt, *tabs_a, *tabs_i)


_B_Z = 0
_B_XBC = _B_Z + SSD_INNER
_B_DT = _B_XBC + SSD_CONV_DIM
_B_COLS = _B_DT + LANES


def _proj_ssd_kernel(x_ref, nw_ref, w_ref, cw_ref, cb_ref, zs_ref, xa_ref, dt_ref, ext_ref):
    tm = PROJ_ROWS_B
    tail = SUBLANES

    @pl.when(pl.program_id(1) == 0)
    def _():
        ext_ref[0:tail, :] = jnp.zeros((tail, SSD_CONV_DIM), F32)

    h = _rms(x_ref[...], nw_ref[...]).astype(BF16)
    dt_ref[...] = _dot(h, w_ref[:, _B_DT:_B_DT + LANES])
    cstep = 512
    z_chunks = list(range(0, SSD_INNER, cstep))
    for c0 in range(0, SSD_CONV_DIM, cstep):
        cs = slice(c0, c0 + cstep)
        if z_chunks:
            z0 = z_chunks.pop(0)
            z = _dot(h, w_ref[:, _B_Z + z0:_B_Z + z0 + cstep])
            zs_ref[:, z0:z0 + cstep] = z * _sigmoid(z)
        ext_ref[tail:tail + tm, cs] = _dot(h, w_ref[:, _B_XBC + c0:_B_XBC + c0 + cstep])
        for c1 in range(c0, c0 + cstep, LANES):
            cl = slice(c1, c1 + LANES)
            ue = ext_ref[:, cl]
            u = cb_ref[:, cl] + cw_ref[SSD_CONV - 1:SSD_CONV, cl] * ue[tail:tail + tm]
            for j in range(SSD_CONV - 1):
                back = SSD_CONV - 1 - j
                u = u + cw_ref[j:j + 1, cl] * pltpu.roll(ue, back, 0)[tail:tail + tm]
            xa_ref[:, cl] = u * _sigmoid(u)
            ext_ref[0:tail, cl] = ue[tm:tm + tail]


def _proj_ssd(x2, nw, w_b, cw, cb, bsz, seq):
    tm = PROJ_ROWS_B
    ns = seq // tm
    row = lambda w: pl.BlockSpec((tm, w), lambda b, s: (b * ns + s, 0))
    widths = (SSD_INNER, SSD_CONV_DIM, LANES)
    return pl.pallas_call(
        _proj_ssd_kernel,
        grid=(bsz, ns),
        in_specs=[row(D_MODEL), _resident((1, D_MODEL)), _resident((D_MODEL, _B_COLS)),
                  _resident((SSD_CONV, SSD_CONV_DIM)), _resident((1, SSD_CONV_DIM))],
        out_specs=[row(w) for w in widths],
        out_shape=[jax.ShapeDtypeStruct((bsz * seq, w), F32) for w in widths],
        scratch_shapes=[pltpu.VMEM((tm + SUBLANES, SSD_CONV_DIM), F32)],
        compiler_params=_params("parallel", "arbitrary"),
        name="proj_ssd",
    )(x2, nw, w_b, cw, cb)


def _sortable(x):
    return x ^ ((x >> 31) & jnp.int32(0x7FFFFFFF))


def _reduce_rows(x, op):
    for sh in (4, 2, 1):
        x = op(x, pltpu.roll(x, sh, 0))
    return x[0:1, :]


def _dsa_kernel(q_ref, qi_ref, wq_ref, k_ref, vt_ref, kib_ref, o_ref, score_ref, floor_ref, sel_ref, acc_ref, *,
                topk, seq):
    i = pl.program_id(1)
    n_idx = (i + IDX_TILE // Q_BLOCK) // (IDX_TILE // Q_BLOCK)
    n_att = (i + ATT_TILE // Q_BLOCK) // (ATT_TILE // Q_BLOCK)
    kpos = lax.broadcasted_iota(jnp.int32, (IDX_TILE, Q_BLOCK), 0)
    qpos = lax.broadcasted_iota(jnp.int32, (IDX_TILE, Q_BLOCK), 1) + i * Q_BLOCK
    w_t = wq_ref[...].T
    qi_all = jnp.concatenate([qi_ref[:, h * LANES:(h + 1) * LANES] for h in range(IDX_HEADS)], axis=0)
    kf = float(topk)
    int_max = jnp.int32(2 ** 31 - 1)

    def tile(j):
        return pl.ds(pl.multiple_of(j * IDX_TILE, IDX_TILE), IDX_TILE)

    def idx_body(j, carry, diagonal):
        kmax, kmin = carry
        s = lax.dot_general(kib_ref[tile(j), :], qi_all, _NT, preferred_element_type=F32)
        acc = jnp.zeros((IDX_TILE, Q_BLOCK), F32)
        for h in range(IDX_HEADS):
            acc = acc + w_t[IDX_DIM + h:IDX_DIM + h + 1, :] * jnp.maximum(s[:, h * Q_BLOCK:(h + 1) * Q_BLOCK], 0.0)
        key = _sortable(pltpu.bitcast(acc, jnp.int32))
        near = acc.astype(BF16).astype(F32)
        near_bits = pltpu.bitcast(near, jnp.int32)
        below = pltpu.bitcast(near_bits + jnp.where(near_bits < 0, jnp.int32(0x10000), jnp.int32(-0x10000)), F32)
        floor = jnp.where(near > acc, jnp.where(near == 0.0, -MIN_NORMAL, below), near)
        score = acc
        low = high = key
        if diagonal:
            causal = kpos + j * IDX_TILE <= qpos
            low = jnp.where(causal, key, jnp.int32(INT_MIN))
            high = jnp.where(causal, key, int_max)
            floor = jnp.where(causal, floor, -jnp.inf)
            score = jnp.where(causal, acc, -jnp.inf)
        score_ref[tile(j), :] = score
        floor_ref[tile(j), :] = floor.astype(BF16)
        kmax = jnp.maximum(kmax, _fold_rows(low, SUBLANES, jnp.maximum))
        kmin = jnp.minimum(kmin, _fold_rows(high, SUBLANES, jnp.minimum))
        return kmax, kmin

    assert IDX_TILE >= Q_BLOCK
    extremes = lax.fori_loop(0, n_idx - 1, functools.partial(idx_body, diagonal=False),
                             (jnp.full((SUBLANES, Q_BLOCK), INT_MIN, jnp.int32),
                              jnp.full((SUBLANES, Q_BLOCK), 2 ** 31 - 1, jnp.int32)))
    kmax, kmin = idx_body(n_idx - 1, extremes, diagonal=True)
    kmax = _reduce_rows(kmax, jnp.maximum)
    kmin = _reduce_rows(kmin, jnp.minimum)

    def as_score(key):
        return pltpu.bitcast(_sortable(key), F32)

    def count(pred):
        def body(j, cnt):
            m = jnp.where(pred(score_ref[tile(j), :], kpos + j * IDX_TILE), 1.0, 0.0)
            return cnt + _fold_rows(m, SUBLANES)
        cnt = lax.fori_loop(0, n_idx, body, jnp.zeros((SUBLANES, Q_BLOCK), F32))
        return jnp.sum(cnt, axis=0, keepdims=True)

    n_valid = (qpos[0:1, :] + 1).astype(F32)
    few = n_valid <= kf
    flat = kmin == kmax
    start = few | flat
    thr0 = jnp.where(few, jnp.int32(LOWEST_KEY), kmin)
    state = (jnp.where(start, thr0, kmin), jnp.where(start, thr0, kmax + 1), n_valid)

    def search_pass(it, st):
        lo, hi, c_lo = st
        f_lo = pltpu.bitcast(_sortable(lo), F32)
        f_hi = pltpu.bitcast(_sortable(hi - 1), F32)
        piv = _sortable(pltpu.bitcast(0.5 * f_lo + 0.5 * f_hi, jnp.int32))
        piv = jnp.where(it % 4 == 3, lo + lax.shift_right_logical(hi - lo, 1), piv)
        piv = jnp.where(it == 0, jnp.int32(TINY_KEY), piv)
        piv = jnp.where(it == 1, jnp.where(hi == TINY_KEY, jnp.int32(-1), piv), piv)
        piv = jnp.minimum(jnp.maximum(piv, lo + 1), hi - 1)
        piv_f = as_score(piv)
        c = count(lambda st, pos: st >= piv_f)
        live = lo != hi
        ge = c >= kf
        hit = c == kf
        lo_n = jnp.where(live & ge, piv, lo)
        c_lo_n = jnp.where(live & ge, c, c_lo)
        hi_n = jnp.where(live & jnp.logical_not(ge), piv, hi)
        flushed = (lo_n > -TINY_KEY - 1) & (hi_n <= TINY_KEY)
        hi_n = jnp.where(live & (hit | (hi_n - lo_n == 1) | flushed), lo_n, hi_n)
        return lo_n, hi_n, c_lo_n

    def count_coarse(p):
        pk = 2 * SUBLANES
        def body(j, cnt):
            m = jnp.where(floor_ref[tile(j), :] >= p, jnp.bfloat16(1), jnp.bfloat16(0))
            return cnt + _fold_rows(m, pk).astype(F32)
        cnt = lax.fori_loop(0, n_idx, body, jnp.zeros((pk, Q_BLOCK), F32))
        return jnp.sum(cnt, axis=0, keepdims=True)

    def coarse_pass(it, st):
        lo, hi, c_lo = st
        f_lo = pltpu.bitcast(_sortable(lo), F32)
        f_hi = pltpu.bitcast(_sortable(hi - 1), F32)
        p = (0.5 * f_lo + 0.5 * f_hi).astype(BF16)
        p_bits = pltpu.bitcast(p.astype(F32), jnp.int32)
        piv = _sortable(p_bits)
        expo = p_bits & jnp.int32(0x7F800000)
        ok = (piv > lo) & (piv < hi) & (expo != 0) & (expo != jnp.int32(0x7F800000))
        c = count_coarse(p)
        live = (lo != hi) & ok
        ge = c >= kf
        hit = c == kf
        lo_n = jnp.where(live & ge, piv, lo)
        c_lo_n = jnp.where(live & ge, c, c_lo)
        hi_n = jnp.where(live & jnp.logical_not(ge), piv, hi)
        hi_n = jnp.where(live & (hit | (hi_n - lo_n == 1)), lo_n, hi_n)
        return lo_n, hi_n, c_lo_n

    state = lax.fori_loop(0, 2, search_pass, state)
    state = lax.fori_loop(2, 2 + COARSE_PASSES, coarse_pass, state)
    state = lax.fori_loop(2 + COARSE_PASSES, SEARCH_PASSES, search_pass, state)

    def pending(st):
        return jnp.max(jnp.where(st[0] != st[1], 1.0, 0.0))

    def more_cond(carry):
        return jnp.logical_and(carry[1] > 0.0, carry[0] < 1024)

    def more_body(carry):
        st = search_pass(carry[0] + 1, search_pass(carry[0], carry[2:]))
        return (carry[0] + 2, pending(st)) + st

    thr_key, _, c_ge = lax.while_loop(more_cond, more_body, (jnp.int32(SEARCH_PASSES), pending(state)) + state)[2:]
    thr = as_score(thr_key)
    tied = jnp.max(c_ge) > kf

    def write_sel(pred):
        def body(j, carry):
            sel = pred(score_ref[tile(j), :], kpos + j * IDX_TILE)
            sel_ref[tile(j), :] = jnp.where(sel, 1.0, 0.0).astype(BF16)
            return carry
        lax.fori_loop(0, n_idx, body, 0)

    @pl.when(jnp.logical_not(tied))
    def _():
        write_sel(lambda kt, pos: kt >= thr)

    @pl.when(tied)
    def _():
        need = kf - count(lambda kt, pos: kt > thr)
        nbits = max(1, int(np.ceil(np.log2(seq))))

        def pos_body(p, lim):
            cand = lim + lax.shift_left(jnp.int32(1), nbits - 1 - p)
            c = count(lambda kt, pos: jnp.where(kt == thr, pos, jnp.int32(seq)) < cand)
            return jnp.where(c < need, cand, lim)

        lim = lax.fori_loop(0, nbits, pos_body, jnp.zeros((1, Q_BLOCK), jnp.int32))
        lim = jnp.where(c_ge > kf, lim, jnp.int32(seq))
        write_sel(lambda kt, pos: (kt > thr) | ((kt == thr) & (pos <= lim)))

    rows = Q_PER_KV * Q_BLOCK
    pk = 2 * SUBLANES
    qgs = [jnp.concatenate([q_ref[:, (Q_PER_KV * g + e) * LANES:(Q_PER_KV * g + e + 1) * LANES]
                            for e in range(Q_PER_KV)], axis=0) for g in range(N_KV_HEADS)]
    ones_rows = jnp.ones((pk, ATT_TILE), BF16)

    def logits(j, g):
        ks = pl.ds(pl.multiple_of(j * ATT_TILE, ATT_TILE), ATT_TILE)
        return lax.dot_general(k_ref[ks, g * LANES:(g + 1) * LANES], qgs[g], _NT, preferred_element_type=F32)

    def weighted_v(j, g, p):
        ks = pl.ds(pl.multiple_of(j * ATT_TILE, ATT_TILE), ATT_TILE)
        sel = jnp.concatenate([sel_ref[ks, :]] * Q_PER_KV, axis=1)
        v_aug = jnp.concatenate([vt_ref[g * LANES:(g + 1) * LANES, ks], ones_rows], axis=0)
        return _dot(v_aug, p.astype(BF16) * sel)

    shifts = [jnp.max(lax.dot_general(k_ref[0:SHIFT_KEYS, g * LANES:(g + 1) * LANES], qgs[g], _NT,
                                      preferred_element_type=F32), axis=0, keepdims=True)
              for g in range(N_KV_HEADS)]

    acc_ref[...] = jnp.zeros_like(acc_ref)

    def fast_body(j, carry):
        for g in range(N_KV_HEADS):
            acc_ref[g] += weighted_v(j, g, jnp.exp2(logits(j, g) - shifts[g]))
        return carry

    lax.fori_loop(0, n_att, fast_body, 0)
    accs = [acc_ref[g] for g in range(N_KV_HEADS)]
    lo_ok, hi_ok = 2.0 ** -60, 2.0 ** 60
    worst = [jnp.max(jnp.where((a[LANES:LANES + 1] >= lo_ok) & (a[LANES:LANES + 1] <= hi_ok), 0.0, 1.0))
             + jnp.max(jnp.where(jnp.abs(a[0:LANES]) < jnp.inf, 0.0, 1.0)) for a in accs]
    redo = (worst[0] + worst[1]) > 0.0

    def store(g, acc):
        o_t = acc[0:LANES] / acc[LANES:LANES + 1]
        for e in range(Q_PER_KV):
            col = (Q_PER_KV * g + e) * LANES
            o_ref[:, col:col + LANES] = o_t[:, e * Q_BLOCK:(e + 1) * Q_BLOCK].T.astype(BF16)

    @pl.when(jnp.logical_not(redo))
    def _():
        for g in range(N_KV_HEADS):
            store(g, accs[g])

    @pl.when(redo)
    def _():
        def safe_body(j, carry):
            ks = pl.ds(pl.multiple_of(j * ATT_TILE, ATT_TILE), ATT_TILE)
            sel = jnp.concatenate([sel_ref[ks, :]] * Q_PER_KV, axis=1).astype(F32)
            out = []
            for g in range(N_KV_HEADS):
                m, acc = carry[g]
                s = logits(j, g) + (sel - 1.0) * (-MASKED)
                m_new = jnp.maximum(m, jnp.max(s, axis=0, keepdims=True))
                out.append((m_new, jnp.exp2(m - m_new) * acc + weighted_v(j, g, jnp.exp2(s - m_new))))
            return tuple(out)

        init = (jnp.full((1, rows), MASKED, F32), jnp.zeros((LANES + pk, rows), F32))
        res = lax.fori_loop(0, n_att, safe_body, (init,) * N_KV_HEADS)
        for g in range(N_KV_HEADS):
            store(g, res[g][1])


def _dsa(q, k, vt, qi, kiwf, kiwb, bsz, seq):
    assert seq % IDX_TILE == 0 and seq % Q_BLOCK == 0
    assert ATT_TILE == IDX_TILE
    nq = seq // Q_BLOCK
    topk = min(TOPK_MAX, seq // 4)
    qrow = lambda w: pl.BlockSpec((Q_BLOCK, w), lambda b, i: (b * nq + i, 0))
    full = lambda w: pl.BlockSpec((seq, w), lambda b, i: (b, 0))
    return pl.pallas_call(
        functools.partial(_dsa_kernel, topk=topk, seq=seq),
        grid=(bsz, nq),
        in_specs=[qrow(ATTN_WIDTH), qrow(IDX_HEADS * LANES), qrow(LANES), full(KV_WIDTH),
                  pl.BlockSpec((KV_WIDTH, seq), lambda b, i: (0, b)), full(LANES)],
        out_specs=qrow(ATTN_WIDTH),
        out_shape=jax.ShapeDtypeStruct((bsz * seq, ATTN_WIDTH), BF16),
        scratch_shapes=[pltpu.VMEM((seq, Q_BLOCK), F32), pltpu.VMEM((seq, Q_BLOCK), BF16),
                        pltpu.VMEM((seq, Q_BLOCK), BF16),
                        pltpu.VMEM((N_KV_HEADS, LANES + 2 * SUBLANES, Q_PER_KV * Q_BLOCK), F32)],
        compiler_params=_params("parallel", "arbitrary"),
        name="dsa_attention",
    )(q, qi, kiwf, k, vt, kiwb)


def _split3(x):
    hi = x.astype(BF16)
    r = x - hi.astype(F32)
    mid = r.astype(BF16)
    return hi, mid, (r - mid.astype(F32)).astype(BF16)


def _ssd_kernel(zs_ref, xa_ref, dt_ref, dtb_ref, alog_ref, dexp_ref, nw_ref, e_ref, o_ref, st_ref):
    L = SSD_CHUNK

    @pl.when(pl.program_id(1) == 0)
    def _():
        st_ref[...] = jnp.zeros_like(st_ref)

    for sub in range(SSD_STEP_CHUNKS):
        _ssd_chunk(slice(sub * L, (sub + 1) * L), zs_ref, xa_ref, dt_ref, dtb_ref, alog_ref, dexp_ref, nw_ref, e_ref,
                   o_ref, st_ref)


def _ssd_chunk(rs, zs_ref, xa_ref, dt_ref, dtb_ref, alog_ref, dexp_ref, nw_ref, e_ref, o_ref, st_ref):
    L = SSD_CHUNK
    dtv = dt_ref[rs, :] + dtb_ref[...]
    dt = jnp.maximum(dtv, 0.0) + jnp.log1p(jnp.exp(-jnp.abs(dtv)))
    adt = dt * (-jnp.exp(alog_ref[...]))
    ri = lax.broadcasted_iota(jnp.int32, (L, L), 0)
    ci = lax.broadcasted_iota(jnp.int32, (L, L), 1)
    tril = ci <= ri
    tril_b = jnp.where(tril, 1.0, 0.0).astype(BF16)
    acs = sum(_dot(tril_b, t) for t in _split3(adt))
    last = acs[L - 1:L, :]
    acs_t = acs.T
    dt_t = dt.T
    hi = jnp.concatenate([dt * jnp.exp(last - acs), jnp.exp(acs)], axis=0)
    stack = (hi.astype(BF16), (hi - hi.astype(BF16).astype(F32)).astype(BF16))
    first_half = ci < SSD_HEAD_DIM
    gw = SSD_GROUP_WIDTH

    for g in range(SSD_GROUPS):
        gs = slice(g * gw, (g + 1) * gw)
        b_g = xa_ref[rs, SSD_INNER + g * SSD_STATE:SSD_INNER + (g + 1) * SSD_STATE]
        c_g = xa_ref[rs, SSD_INNER + SSD_GN + g * SSD_STATE:SSD_INNER + SSD_GN + (g + 1) * SSD_STATE]
        b_b, c_b = b_g.astype(BF16), c_g.astype(BF16)
        cb = lax.dot_general(c_b, b_b, _NT, preferred_element_type=F32)
        b_t = b_g.T.astype(BF16)
        ex = sum(_dot(t, e_ref[:, gs]) for t in stack)
        dtdec, eacs = ex[0:L], ex[L:2 * L]
        xs = xa_ref[rs, gs]
        s_in = st_ref[:, gs]
        y = _dot(c_b, s_in.astype(BF16)) * eacs + xs * dexp_ref[:, gs]
        st_ref[:, gs] = s_in * eacs[L - 1:L, :] + _dot(b_t, (xs * dtdec).astype(BF16))
        ys = []
        for c2 in range(gw // LANES):
            xp = xs[:, c2 * LANES:(c2 + 1) * LANES].astype(BF16)
            pair = []
            for hh in range(LANES // SSD_HEAD_DIM):
                h = g * SSD_HEADS_PER_GROUP + c2 * (LANES // SSD_HEAD_DIM) + hh
                seg = acs[:, h:h + 1] - acs_t[h:h + 1, :]
                lm = jnp.exp(jnp.where(tril, seg, -jnp.inf))
                pair.append(_dot((cb * lm * dt_t[h:h + 1, :]).astype(BF16), xp))
            ys.append(jnp.where(first_half, pair[0], pair[1]))
        y = y + jnp.concatenate(ys, axis=1)
        yz = y * zs_ref[rs, gs]
        ms = jnp.mean(yz * yz, axis=1, keepdims=True)
        o_ref[rs, gs] = (yz * lax.rsqrt(ms + NORM_EPS) * nw_ref[:, gs]).astype(BF16)


def _ssd(zs, xa, dtp, dtb, alog, dexp, nw, emat, bsz, seq):
    L = SSD_CHUNK * SSD_STEP_CHUNKS
    assert seq % L == 0
    nc = seq // L
    row = lambda w: pl.BlockSpec((L, w), lambda b, c: (b * nc + c, 0))
    return pl.pallas_call(
        _ssd_kernel,
        grid=(bsz, nc),
        in_specs=[row(SSD_INNER), row(SSD_CONV_DIM), row(LANES),
                  _resident((1, LANES)), _resident((1, LANES)), _resident((1, SSD_INNER)),
                  _resident((1, SSD_INNER)), _resident((LANES, SSD_INNER))],
        out_specs=row(SSD_INNER),
        out_shape=jax.ShapeDtypeStruct((bsz * seq, SSD_INNER), BF16),
        scratch_shapes=[pltpu.VMEM((SSD_STATE, SSD_INNER), F32)],
        compiler_params=_params("parallel", "arbitrary"),
        name="ssd_branch",
    )(zs, xa, dtp, dtb, alog, dexp, nw, emat)


def _merge_kernel(x_ref, a_ref, b_ref, nw_ref, wg_ref, wpa_ref, wps_ref, wo_ref, o_ref):
    x = x_ref[...]
    h = _rms(x, nw_ref[...]).astype(BF16)
    merged = _sigmoid(_dot(h, wg_ref[:, 0:D_MODEL])) * _dot(a_ref[...], wpa_ref[...])
    merged = merged + _sigmoid(_dot(h, wg_ref[:, D_MODEL:2 * D_MODEL])) * _dot(b_ref[...], wps_ref[...])
    o_ref[...] = x + _dot(merged.astype(BF16), wo_ref[...])


def _merge(x2, a, b, nw, wg, wpa, wps, wo):
    tokens = x2.shape[0]
    tm = MERGE_ROWS
    row = lambda w: pl.BlockSpec((tm, w), lambda i: (i, 0))
    return pl.pallas_call(
        _merge_kernel,
        grid=(tokens // tm,),
        in_specs=[row(D_MODEL), row(ATTN_WIDTH), row(SSD_INNER), _resident((1, D_MODEL)),
                  _resident((D_MODEL, 2 * D_MODEL)), _resident((ATTN_WIDTH, D_MODEL)),
                  _resident((SSD_INNER, D_MODEL)), _resident((D_MODEL, D_MODEL))],
        out_specs=row(D_MODEL),
        out_shape=jax.ShapeDtypeStruct((tokens, D_MODEL), F32),
        compiler_params=_params("parallel"),
        name="merge_out",
    )(x2, a, b, nw, wg, wpa, wps, wo)


def _ffn_kernel(x_ref, nw_ref, wup_ref, cw_ref, cb_ref, wdn_ref, fnw_ref, o_ref, act_ref, carry_ref, *, final_norm):
    tm = FFN_ROWS
    tail = SUBLANES
    nf = FFN_DIM // FFN_TILE

    @pl.when(pl.program_id(1) == 0)
    def _():
        carry_ref[...] = jnp.zeros_like(carry_ref)

    x = x_ref[...]
    h = _rms(x, nw_ref[...]).astype(BF16)
    for f in range(nf):
        parts = []
        for part in range(2):
            col = part * FFN_DIM + f * FFN_TILE
            cs = slice(col, col + FFN_TILE)
            ct = part * nf + f
            u = _dot(h, wup_ref[:, cs])
            ue = jnp.concatenate([carry_ref[ct], u], axis=0)
            cv = cb_ref[:, cs] + cw_ref[FFN_CONV - 1:FFN_CONV, cs] * u
            for j in range(FFN_CONV - 1):
                back = FFN_CONV - 1 - j
                cv = cv + cw_ref[j:j + 1, cs] * pltpu.roll(ue, back, 0)[tail:tail + tm]
            carry_ref[ct] = u[tm - tail:tm, :]
            parts.append(cv)
        act_ref[:, f * FFN_TILE:(f + 1) * FFN_TILE] = (parts[0] * _sigmoid(parts[0]) * parts[1]).astype(BF16)
    y = x + _dot(act_ref[...], wdn_ref[...])
    if final_norm:
        y = _rms(y, fnw_ref[...])
    o_ref[...] = y


def _ffn(x2, nw, wup, cw, cb, wdn, fnw, bsz, seq, final_norm):
    tm = FFN_ROWS
    ns = seq // tm
    nf = FFN_DIM // FFN_TILE
    row = pl.BlockSpec((tm, D_MODEL), lambda b, s: (b * ns + s, 0))
    return pl.pallas_call(
        functools.partial(_ffn_kernel, final_norm=final_norm),
        grid=(bsz, ns),
        in_specs=[row, _resident((1, D_MODEL)), _resident((D_MODEL, 2 * FFN_DIM)),
                  _resident((FFN_CONV, 2 * FFN_DIM)), _resident((1, 2 * FFN_DIM)),
                  _resident((FFN_DIM, D_MODEL)), _resident((1, D_MODEL))],
        out_specs=row,
        out_shape=jax.ShapeDtypeStruct((bsz * seq, D_MODEL), F32),
        scratch_shapes=[pltpu.VMEM((tm, FFN_DIM), BF16),
                        pltpu.VMEM((2 * nf, SUBLANES, FFN_TILE), F32)],
        compiler_params=_params("parallel", "arbitrary"),
        name="conv_glu_ffn",
    )(x2, nw, wup, cw, cb, wdn, fnw)


def _split_w_in(w_in):
    o = np.cumsum([0, ATTN_WIDTH, KV_WIDTH, KV_WIDTH, IDX_HEADS * IDX_DIM, IDX_DIM, IDX_HEADS,
                   SSD_INNER, SSD_CONV_DIM, SSD_HEADS, D_MODEL, D_MODEL]).tolist()
    seg = lambda j: w_in[:, o[j]:o[j + 1]]
    d = w_in.shape[0]
    qi = seg(3).reshape(d, IDX_HEADS, IDX_DIM)
    qi = jnp.pad(qi, ((0, 0), (0, 0), (0, LANES - IDX_DIM))).reshape(d, IDX_HEADS * LANES)
    kiw = jnp.pad(jnp.concatenate([seg(4), seg(5)], axis=1), ((0, 0), (0, LANES - IDX_DIM - IDX_HEADS)))
    w_a = jnp.concatenate([seg(0), seg(1), qi, kiw], axis=1).astype(BF16)
    w_vt = seg(2).T.astype(BF16)
    dt = jnp.pad(seg(8), ((0, 0), (0, LANES - SSD_HEADS)))
    w_b = jnp.concatenate([seg(6), seg(7), dt], axis=1).astype(BF16)
    w_g = jnp.concatenate([seg(9), seg(10)], axis=1).astype(BF16)
    return w_a, w_vt, w_b, w_g


def _lane_pad_row(v):
    return jnp.pad(v, (0, LANES - v.shape[0]))[None, :]


def kernel(x, norm_mix_w, w_in, ssd_conv_w, ssd_conv_b, ssd_dt_bias, ssd_a_log, ssd_d, ssd_norm_w, w_proj_attn, w_proj_ssd, w_out, norm_ffn_w, ffn_w_up, ffn_conv_w, ffn_conv_b, ffn_w_down, norm_final_w):
    bsz, seq, d = x.shape
    depth = w_in.shape[0]
    assert d == D_MODEL and seq % max(PROJ_ROWS_A, FFN_ROWS, MERGE_ROWS, IDX_TILE) == 0
    tabs_a = _rope_tables(seq, HEAD_DIM // ROPE_FRACTION)
    tabs_i = _rope_tables(seq, IDX_DIM // ROPE_FRACTION)
    emat = (jnp.arange(SSD_INNER)[None, :] // SSD_HEAD_DIM == jnp.arange(LANES)[:, None]).astype(BF16)
    x2 = x.reshape(bsz * seq, d)
    for l in range(depth):
        w_a, w_vt, w_b, w_g = _split_w_in(w_in[l])
        nw = norm_mix_w[l][None, :]
        q, k, vt, qi, kiwf, kiwb = _proj_attn(x2, nw, w_a, w_vt, tabs_a, tabs_i, seq)
        a = _dsa(q, k, vt, qi, kiwf, kiwb, bsz, seq)
        zs, xa, dtp = _proj_ssd(x2, nw, w_b, ssd_conv_w[l], ssd_conv_b[l][None, :], bsz, seq)
        b = _ssd(zs, xa, dtp, _lane_pad_row(ssd_dt_bias[l]), _lane_pad_row(ssd_a_log[l]),
                 jnp.repeat(ssd_d[l], SSD_HEAD_DIM)[None, :], ssd_norm_w[l][None, :], emat, bsz, seq)
        x2 = _merge(x2, a, b, nw, w_g, w_proj_attn[l].astype(BF16), w_proj_ssd[l].astype(BF16),
                    w_out[l].astype(BF16))
        x2 = _ffn(x2, norm_ffn_w[l][None, :], ffn_w_up[l].astype(BF16), ffn_conv_w[l], ffn_conv_b[l][None, :],
                  ffn_w_down[l].astype(BF16), norm_final_w[None, :], bsz, seq, final_norm=(l == depth - 1))
    return x2.reshape(bsz, seq, d)
```
